```python
import jax
import jax.numpy as jnp
from jax import lax

D_MODEL = 2048
BATCH = 8
SEQ = 4096
DEPTH = 4

GRID_W = 64
CTX_LEN = 256
HEAD_DIM = 128
N_ATTN_HEADS = D_MODEL // HEAD_DIM
N_KV_HEADS = N_ATTN_HEADS // 4
N_DELTA_HEADS = D_MODEL // HEAD_DIM
DELTA_DK = HEAD_DIM
DELTA_DV = HEAD_DIM
WINDOW = 128
ATTN_BLOCK = 128
CHUNK = 64
CONV_K = 5
FFN_HIDDEN = ((8 * D_MODEL + 767) // 768) * 256
ROPE_BASE = 10000.0
EPS = 1e-6
NEG_INF = -1e30
ATT_Q = N_ATTN_HEADS * HEAD_DIM
ATT_KV = N_KV_HEADS * HEAD_DIM
DN_QK = N_DELTA_HEADS * DELTA_DK
DN_V = N_DELTA_HEADS * DELTA_DV
IN_SPLITS = (ATT_Q, ATT_KV, ATT_KV, DN_QK, DN_QK, DN_V, DN_V, 2 * N_DELTA_HEADS, 2 * N_DELTA_HEADS, D_MODEL, D_MODEL)
IN_WIDTH = ATT_Q + 2 * ATT_KV + 2 * DN_QK + 2 * DN_V + 4 * N_DELTA_HEADS + 2 * D_MODEL

kernel_name = 'hybrid_swa_gdn_prefix_dit'


def _rmsnorm(x, gain):
    xf = x.astype(jnp.float32)
    y = xf * lax.rsqrt(jnp.mean(xf * xf, axis=-1, keepdims=True) + EPS)
    return (y * gain.astype(jnp.float32)).astype(x.dtype)


def _split_in(z):
    points, acc = [], 0
    for width in IN_SPLITS[:-1]:
        acc += width
        points.append(acc)
    return jnp.split(z, points, axis=-1)


def _axial_rope(t, cos_r, sin_r, cos_c, sin_c):
    half, quarter = HEAD_DIM // 2, HEAD_DIM // 4
    tf = t.astype(jnp.float32)

    def rot(seg, cos, sin):
        cos, sin = cos[None, :, None, :], sin[None, :, None, :]
        a, b = seg[..., :quarter], seg[..., quarter:]
        return jnp.concatenate([a * cos - b * sin, b * cos + a * sin], axis=-1)

    out = jnp.concatenate([rot(tf[..., :half], cos_r, sin_r), rot(tf[..., half:], cos_c, sin_c)], axis=-1)
    return out.astype(t.dtype)


def _window_attention(q, k, v, kc, vc, sink):
    B, L, H, hd = q.shape
    KV = k.shape[2]
    G = H // KV
    BL = ATTN_BLOCK
    nb = L // BL
    scale = hd ** -0.5
    qb = q.reshape(B, nb, BL, KV, G, hd)

    def band(t):
        tp = jnp.pad(t, ((0, 0), (BL, BL), (0, 0), (0, 0))).reshape(B, nb + 2, BL, KV, hd)
        return jnp.concatenate([tp[:, :-2], tp[:, 1:-1], tp[:, 2:]], axis=2)

    kb, vb = band(k), band(v)
    s_loc = jnp.einsum('bnqkgd,bnjkd->bnkgqj', qb, kb).astype(jnp.float32) * scale
    s_ctx = jnp.einsum('bnqkgd,bckd->bnkgqc', qb, kc).astype(jnp.float32) * scale
    qi = jnp.arange(BL)[:, None]
    kj = jnp.arange(3 * BL)[None, :]
    kpos = (jnp.arange(nb)[:, None, None] - 1) * BL + kj[None]
    valid = (jnp.abs(kj - BL - qi)[None] <= WINDOW) & (kpos >= 0) & (kpos < L)
    s_loc = jnp.where(valid[None, :, None, None], s_loc, NEG_INF)
    sk = sink.astype(jnp.float32).reshape(KV, G)[None, None, :, :, None]
    m = jnp.maximum(jnp.maximum(s_loc.max(-1), s_ctx.max(-1)), sk)
    p_loc = jnp.exp(s_loc - m[..., None])
    p_ctx = jnp.exp(s_ctx - m[..., None])
    denom = (p_loc.sum(-1) + p_ctx.sum(-1) + jnp.exp(sk - m))[..., None]
    o = (jnp.einsum('bnkgqj,bnjkd->bnqkgd', (p_loc / denom).astype(v.dtype), vb)
         + jnp.einsum('bnkgqc,bckd->bnqkgd', (p_ctx / denom).astype(vc.dtype), vc))
    return o.reshape(B, L, H * hd)


def _context_attention(qc, kc, vc, sink):
    B, C, H, hd = qc.shape
    KV = kc.shape[2]
    G = H // KV
    qg = qc.reshape(B, C, KV, G, hd)
    s = jnp.einsum('bqkgd,bckd->bkgqc', qg, kc).astype(jnp.float32) * hd ** -0.5
    sk = sink.astype(jnp.float32).reshape(KV, G)[None, :, :, None]
    m = jnp.maximum(s.max(-1), sk)
    p = jnp.exp(s - m[..., None])
    p = p / (p.sum(-1) + jnp.exp(sk - m))[..., None]
    o = jnp.einsum('bkgqc,bckd->bqkgd', p.astype(vc.dtype), vc)
    return o.reshape(B, C, H * hd)


def _short_conv(t, w):
    ch = t.shape[-1]
    return lax.conv_general_dilated(t, w[:, None, :].astype(t.dtype), window_strides=(1,),
                                    padding=[(CONV_K // 2, CONV_K // 2)],
                                    dimension_numbers=('NWC', 'WIO', 'NWC'), feature_group_count=ch)


def _l2norm(t):
    tf = t.astype(jnp.float32)
    return tf * lax.rsqrt(jnp.sum(tf * tf, axis=-1, keepdims=True) + EPS)


def _deltanet_inputs(dq, dk, dv, b_raw, a_raw, conv_w, a_log, dt_bias):
    B, L, _ = dq.shape
    qkv = jax.nn.silu(_short_conv(jnp.concatenate([dq, dk, dv], axis=-1), conv_w))
    q, k, v = jnp.split(qkv, [DN_QK, 2 * DN_QK], axis=-1)
    q = _l2norm(q.reshape(B, L, N_DELTA_HEADS, DELTA_DK))
    k = _l2norm(k.reshape(B, L, N_DELTA_HEADS, DELTA_DK))
    v = v.reshape(B, L, N_DELTA_HEADS, DELTA_DV).astype(jnp.float32)
    beta = jax.nn.sigmoid(b_raw.astype(jnp.float32)).reshape(B, L, 2, N_DELTA_HEADS)
    g = -jnp.exp(a_log.astype(jnp.float32)) * jax.nn.softplus(
        a_raw.astype(jnp.float32).reshape(B, L, 2, N_DELTA_HEADS) + dt_bias.astype(jnp.float32))
    return q, k, v, beta, g


def _gated_delta_rule(q, k, v, beta, g, s0):
    B, L, H, dk = q.shape
    dv = v.shape[-1]
    n = L // CHUNK

    def chunks(t):
        t = t.astype(jnp.float32).reshape((B, n, CHUNK, H) + t.shape[3:])
        return jnp.moveaxis(t, (1, 3), (0, 2))

    qc, kc, vc = chunks(q) * dk ** -0.5, chunks(k), chunks(v)
    bc = chunks(beta)
    gc = jnp.cumsum(chunks(g), axis=-1)
    idx = jnp.arange(CHUNK)
    incl = idx[:, None] >= idx[None, :]
    strict = idx[:, None] > idx[None, :]
    decay = jnp.exp(jnp.where(incl, gc[..., :, None] - gc[..., None, :], -jnp.inf))
    kb = kc * bc[..., None]
    a_mat = jnp.where(strict, jnp.einsum('nbhid,nbhjd->nbhij', kb, kc) * decay, 0.0)
    rhs = jnp.concatenate([vc * bc[..., None], kb * jnp.exp(gc)[..., None]], axis=-1)
    sol = lax.linalg.triangular_solve(jnp.eye(CHUNK, dtype=jnp.float32) + a_mat, rhs,
                                      left_side=True, lower=True)
    u, w = sol[..., :dv], sol[..., dv:]
    qk = jnp.einsum('nbhid,nbhjd->nbhij', qc, kc) * decay
    q_dec = qc * jnp.exp(gc)[..., None]
    k_dec = kc * jnp.exp(gc[..., -1:] - gc)[..., None]
    g_last = jnp.exp(gc[..., -1])

    def step(s, xs):
        u_i, w_i, qk_i, q_i, k_i, gl_i = xs
        v_new = u_i - jnp.einsum('bhcd,bhde->bhce', w_i, s)
        o = jnp.einsum('bhcd,bhde->bhce', q_i, s) + jnp.einsum('bhij,bhje->bhie', qk_i, v_new)
        s = s * gl_i[..., None, None] + jnp.einsum('bhcd,bhce->bhde', k_i, v_new)
        return s, o

    s_fin, o = lax.scan(step, s0, (u, w, qk, q_dec, k_dec, g_last))
    o = jnp.moveaxis(o, (0, 2), (1, 3)).reshape(B, L, H, dv)
    return o, s_fin


def _flip(t):
    return t[:, ::-1]


def _same(t):
    return t


def _bidir_deltanet(lat, ctd):
    q, k, v, beta, g = lat
    qc, kc, vc, bc, gc = ctd
    s0 = jnp.zeros((qc.shape[0], N_DELTA_HEADS, DELTA_DK, DELTA_DV), jnp.float32)
    outs_l, outs_c = [], []
    for d in range(2):
        f = _flip if d == 1 else _same
        oc, s_c = _gated_delta_rule(f(qc), f(kc), f(vc), f(bc[:, :, d]), f(gc[:, :, d]), s0)
        ol, _ = _gated_delta_rule(f(q), f(k), f(v), f(beta[:, :, d]), f(g[:, :, d]), s_c)
        outs_l.append(f(ol))
        outs_c.append(f(oc))
    return outs_l[0] + outs_l[1], outs_c[0] + outs_c[1]


def _deltanet_out(o, z, gain):
    B, L = o.shape[:2]
    on = o * lax.rsqrt(jnp.mean(o * o, axis=-1, keepdims=True) + EPS) * gain.astype(jnp.float32)
    return (on.reshape(B, L, DN_V) * jax.nn.silu(z.astype(jnp.float32))).astype(z.dtype)


def _merge(att, dn, ga, gd, w_attn_proj, w_delta_proj, w_out):
    y = jax.nn.sigmoid(ga) * (att @ w_attn_proj) + jax.nn.sigmoid(gd) * (dn @ w_delta_proj)
    return y @ w_out


def _swiglu(h, w_gate_up, w_down):
    gate, up = jnp.split(h @ w_gate_up, 2, axis=-1)
    return (jax.nn.silu(gate) * up) @ w_down


def _layer(x, xc, mod, mod_c, norm1, norm2, w_in, conv_w, sink, a_log, dt_bias, dn_norm,
           w_attn_proj, w_delta_proj, w_out, w_gate_up, w_down, rope, need_ctx):
    sh1, sc1, gt1, sh2, sc2, gt2 = jnp.split(mod[:, None, :], 6, axis=-1)
    shc1, scc1, gtc1, shc2, scc2, gtc2 = jnp.split(mod_c, 6, axis=-1)
    B, L, _ = x.shape
    C = xc.shape[1]
    h = _rmsnorm(x, norm1) * (1 + sc1) + sh1
    hc = _rmsnorm(xc, norm1) * (1 + scc1) + shc1
    aq, ak, av, dq, dk, dv, dz, db, da, ga, gd = _split_in(h @ w_in)
    aqc, akc, avc, dqc, dkc, dvc, dzc, dbc, dac, gac, gdc = _split_in(hc @ w_in)
    q = _axial_rope(aq.reshape(B, L, N_ATTN_HEADS, HEAD_DIM), *rope)
    k = _axial_rope(ak.reshape(B, L, N_KV_HEADS, HEAD_DIM), *rope)
    v = av.reshape(B, L, N_KV_HEADS, HEAD_DIM)
    kc = akc.reshape(B, C, N_KV_HEADS, HEAD_DIM)
    vc = avc.reshape(B, C, N_KV_HEADS, HEAD_DIM)
    att = _window_attention(q, k, v, kc, vc, sink)
    lat = _deltanet_inputs(dq, dk, dv, db, da, conv_w, a_log, dt_bias)
    ctd = _deltanet_inputs(dqc, dkc, dvc, dbc, dac, conv_w, a_log, dt_bias)
    o_lat, o_ctx = _bidir_deltanet(lat, ctd)
    dn = _deltanet_out(o_lat, dz, dn_norm)
    x = x + gt1 * _merge(att, dn, ga, gd, w_attn_proj, w_delta_proj, w_out)
    x = x + gt2 * _swiglu(_rmsnorm(x, norm2) * (1 + sc2) + sh2, w_gate_up, w_down)
    if need_ctx:
        att_c = _context_attention(aqc.reshape(B, C, N_ATTN_HEADS, HEAD_DIM), kc, vc, sink)
        dn_c = _deltanet_out(o_ctx, dzc, dn_norm)
        xc = xc + gtc1 * _merge(att_c, dn_c, gac, gdc, w_attn_proj, w_delta_proj, w_out)
        xc = xc + gtc2 * _swiglu(_rmsnorm(xc, norm2) * (1 + scc2) + shc2, w_gate_up, w_down)
    return x, xc


def setup_inputs(seed: int = 0) -> dict:
    key = jax.random.key(seed)
    ks = jax.random.split(key, 24)
    f32 = jnp.float32
    D = D_MODEL

    def nrm(k, shape, scale):
        return jax.random.normal(k, shape, f32) * scale

    dt = jnp.exp(jax.random.uniform(ks[12], (DEPTH, 2, N_DELTA_HEADS), f32, jnp.log(1e-3), jnp.log(1e-1)))
    return {
        'x': nrm(ks[0], (BATCH, SEQ, D), 1.0),
        'c': nrm(ks[1], (BATCH, D), 1.0),
        'ctx': nrm(ks[2], (BATCH, CTX_LEN, D), 1.0),
        'c_ctx': nrm(ks[3], (D,), 1.0),
        'w_mod': nrm(ks[4], (DEPTH, D, 6 * D), 0.5 * D ** -0.5),
        'b_mod': nrm(ks[5], (DEPTH, 6 * D), 0.02),
        'norm1': 1.0 + nrm(ks[6], (DEPTH, D), 0.02),
        'norm2': 1.0 + nrm(ks[7], (DEPTH, D), 0.02),
        'w_in': nrm(ks[8], (DEPTH, D, IN_WIDTH), D ** -0.5),
        'conv_w': nrm(ks[9], (DEPTH, CONV_K, 2 * DN_QK + DN_V), CONV_K ** -0.5),
        'attn_sink': nrm(ks[10], (DEPTH, N_ATTN_HEADS), 0.5),
        'a_log': jnp.log(jax.random.uniform(ks[11], (DEPTH, 2, N_DELTA_HEADS), f32, 1.0, 16.0)),
        'dt_bias': dt + jnp.log(-jnp.expm1(-dt)),
        'dn_norm': 1.0 + nrm(ks[13], (DEPTH, DELTA_DV), 0.02),
        'w_attn_proj': nrm(ks[14], (DEPTH, ATT_Q, D), ATT_Q ** -0.5),
        'w_delta_proj': nrm(ks[15], (DEPTH, DN_V, D), DN_V ** -0.5),
        'w_out': nrm(ks[16], (DEPTH, D, D), D ** -0.5),
        'w_gate_up': nrm(ks[17], (DEPTH, D, 2 * FFN_HIDDEN), D ** -0.5),
        'w_down': nrm(ks[18], (DEPTH, FFN_HIDDEN, D), FFN_HIDDEN ** -0.5),
        'final_norm': 1.0 + nrm(ks[19], (D,), 0.02),
    }


def reference(x, c, ctx, c_ctx, w_mod, b_mod, norm1, norm2, w_in, conv_w, attn_sink, a_log, dt_bias,
              dn_norm, w_attn_proj, w_delta_proj, w_out, w_gate_up, w_down, final_norm):
    L = x.shape[1]
    rows = L // GRID_W
    row = jnp.repeat(jnp.arange(rows), GRID_W).astype(jnp.float32)
    col = jnp.tile(jnp.arange(GRID_W), rows).astype(jnp.float32)
    inv_freq = ROPE_BASE ** (-jnp.arange(HEAD_DIM // 4, dtype=jnp.float32) / (HEAD_DIM // 4))
    ang_r = row[:, None] * inv_freq[None]
    ang_c = col[:, None] * inv_freq[None]
    rope = (jnp.cos(ang_r), jnp.sin(ang_r), jnp.cos(ang_c), jnp.sin(ang_c))
    silu_c = jax.nn.silu(c)
    silu_cc = jax.nn.silu(c_ctx)
    xc = ctx
    for l in range(DEPTH):
        mod = silu_c @ w_mod[l] + b_mod[l]
        mod_c = silu_cc @ w_mod[l] + b_mod[l]
        x, xc = _layer(x, xc, mod, mod_c, norm1[l], norm2[l], w_in[l], conv_w[l], attn_sink[l],
                       a_log[l], dt_bias[l], dn_norm[l], w_attn_proj[l], w_delta_proj[l], w_out[l],
                       w_gate_up[l], w_down[l], rope, l < DEPTH - 1)
    return _rmsnorm(x, final_norm)
```

```python
import functools

import jax
import jax.numpy as jnp
from jax import lax
from jax.experimental import pallas as pl
from jax.experimental.pallas import tpu as pltpu

F32 = jnp.float32
BF16 = jnp.bfloat16

HEAD_DIM = 128
WINDOW = 128
ATTN_BLOCK = 128
GRID_W = 64
CONV_K = 5
ROPE_BASE = 10000.0
EPS = 1e-6
NEG_INF = -1e30
CHUNK = 128
LANES = 128
SUBLANES = 8
VMEM_CAP_BYTES = 60 * 1024 * 1024


def _vmem_limit(block_bytes, scratch_bytes=0):
    est = 2 * block_bytes + scratch_bytes
    return int(min(VMEM_CAP_BYTES, max(32 * 1024 * 1024, est * 3 // 2)))


def _params(sem, block_bytes, scratch_bytes=0):
    return pltpu.CompilerParams(dimension_semantics=sem,
                                vmem_limit_bytes=_vmem_limit(block_bytes, scratch_bytes))


def _pick(n, candidates):
    for c in candidates:
        if n % c == 0:
            return c
    raise ValueError(f"no tile in {candidates} divides {n}")


def _nbytes(shape, dtype):
    n = 1
    for s in shape:
        n *= s
    return n * jnp.dtype(dtype).itemsize


def _sigmoid(x):
    return 1.0 / (1.0 + jnp.exp(-x))


def _silu(x):
    return x * _sigmoid(x)


def _dot(a, b):
    return jnp.dot(a, b, preferred_element_type=F32)


def _dot_nt(a, b):
    return lax.dot_general(a, b, (((1,), (1,)), ((), ())), preferred_element_type=F32)


def _split2(x):
    hi = x.astype(BF16)
    lo = (x - hi.astype(F32)).astype(BF16)
    return hi, lo


def _split3(x):
    hi = x.astype(BF16)
    r = x - hi.astype(F32)
    mid = r.astype(BF16)
    lo = (r - mid.astype(F32)).astype(BF16)
    return hi, mid, lo


def _mod_kernel(c_ref, w_ref, b_ref, o_ref):
    s = _silu(c_ref[...])
    s_hi, s_lo = _split2(s)
    w = w_ref[0]
    w_hi, w_lo = _split2(w)
    acc = _dot(s_hi, w_hi) + _dot(s_hi, w_lo) + _dot(s_lo, w_hi)
    o_ref[0] = acc + b_ref[0]


def _modulation(cc, w_mod, b_mod):
    depth, d, n = w_mod.shape
    rows = cc.shape[0]
    tn = _pick(n, (512, 256, 128))
    blocks = _nbytes((rows, d), F32) + _nbytes((d, tn), F32) + _nbytes((1, tn), F32) + _nbytes((rows, tn), F32)
    return pl.pallas_call(
        _mod_kernel,
        grid=(depth, n // tn),
        in_specs=[pl.BlockSpec((rows, d), lambda l, j: (0, 0)),
                  pl.BlockSpec((1, d, tn), lambda l, j: (l, 0, j)),
                  pl.BlockSpec((1, 1, tn), lambda l, j: (l, 0, j))],
        out_specs=pl.BlockSpec((1, rows, tn), lambda l, j: (l, 0, j)),
        out_shape=jax.ShapeDtypeStruct((depth, rows, n), F32),
        compiler_params=_params(("parallel", "parallel"), blocks, 3 * _nbytes((d, tn), F32)),
        name="modulation",
    )(cc, w_mod, b_mod.reshape(depth, 1, n))


def _norm_mod(x, gain, shift, scale):
    y = x * lax.rsqrt(jnp.mean(x * x, axis=-1, keepdims=True) + EPS) * gain
    return y * (1.0 + scale) + shift


def _inproj_kernel(x_ref, mod_ref, g_ref, w_ref, wba_ref, cos_ref, sin_ref,
                   qkv_ref, rest_ref, ba_ref, h_ref, *, n_rope, n_qkv):
    j = pl.program_id(1)

    @pl.when(j == 0)
    def _():
        h = _norm_mod(x_ref[...], g_ref[...], mod_ref[0, 0:1, :], mod_ref[0, 1:2, :]).astype(BF16)
        h_ref[...] = h
        ba_ref[...] = _dot(h, wba_ref[...])

    acc = _dot(h_ref[...], w_ref[...])
    tn = acc.shape[1]

    @pl.when(j < n_rope)
    def _():
        reps = tn // HEAD_DIM
        cos = jnp.concatenate([cos_ref[...]] * reps, axis=1)
        sin = jnp.concatenate([sin_ref[...]] * reps, axis=1)
        lane = lax.broadcasted_iota(jnp.int32, acc.shape, 1)
        quarter = HEAD_DIM // 4
        partner = jnp.where((lane & quarter) == 0,
                            pltpu.roll(acc, tn - quarter, 1), pltpu.roll(acc, quarter, 1))
        qkv_ref[...] = (acc * cos + partner * sin).astype(BF16)

    @pl.when((j >= n_rope) & (j < n_qkv))
    def _():
        qkv_ref[...] = acc.astype(BF16)

    @pl.when(j >= n_qkv)
    def _():
        rest_ref[...] = acc


def _inproj(x, mods, mod_row, gain, w_main, w_ba, cos, sin, *, tm, tn, qk_width, qkv_width):
    m, d = x.shape
    n = w_main.shape[1]
    n_rope, n_qkv = qk_width // tn, qkv_width // tn
    nj = n // tn
    blocks = (_nbytes((tm, d), F32) + _nbytes((6, d), F32) + _nbytes((d, tn), BF16) + _nbytes((d, LANES), BF16)
              + 2 * _nbytes((tm, LANES), F32) + _nbytes((tm, tn), BF16) + _nbytes((tm, tn), F32)
              + _nbytes((tm, LANES), F32))
    return pl.pallas_call(
        functools.partial(_inproj_kernel, n_rope=n_rope, n_qkv=n_qkv),
        grid=(m // tm, nj),
        in_specs=[pl.BlockSpec((tm, d), lambda i, j: (i, 0)),
                  pl.BlockSpec((1, 6, d), lambda i, j: (mod_row(i, tm), 0, 0)),
                  pl.BlockSpec((1, d), lambda i, j: (0, 0)),
                  pl.BlockSpec((d, tn), lambda i, j: (0, j)),
                  pl.BlockSpec((d, LANES), lambda i, j: (0, 0)),
                  pl.BlockSpec((tm, LANES), lambda i, j: (i, 0)),
                  pl.BlockSpec((tm, LANES), lambda i, j: (i, 0))],
        out_specs=[pl.BlockSpec((tm, tn), lambda i, j: (i, jnp.minimum(j, n_qkv - 1))),
                   pl.BlockSpec((tm, tn), lambda i, j: (i, jnp.maximum(j - n_qkv, 0))),
                   pl.BlockSpec((tm, LANES), lambda i, j: (i, 0))],
        out_shape=[jax.ShapeDtypeStruct((m, qkv_width), BF16),
                   jax.ShapeDtypeStruct((m, n - qkv_width), F32),
                   jax.ShapeDtypeStruct((m, LANES), F32)],
        scratch_shapes=[pltpu.VMEM((tm, d), BF16)],
        compiler_params=_params(("parallel", "arbitrary"), blocks,
                                _nbytes((tm, d), BF16) + 4 * _nbytes((tm, tn), F32)),
        name="inproj",
    )(x, mods, gain, w_main, w_ba, cos, sin)


def _softmax_pv(s, sink_col, v):
    m = jnp.maximum(jnp.max(s, axis=-1, keepdims=True), sink_col)
    p = jnp.exp(s - m)
    denom = jnp.sum(p, axis=-1, keepdims=True) + jnp.exp(sink_col - m)
    return _dot((p * (1.0 / denom)).astype(BF16), v)


def _stack_heads(q, groups):
    return jnp.concatenate([q[:, g * HEAD_DIM:(g + 1) * HEAD_DIM] for g in range(groups)], axis=0)


def _unstack_heads(o, groups, rows):
    return jnp.concatenate([o[g * rows:(g + 1) * rows] for g in range(groups)], axis=1)


def _sink_col(sink_ref, kvh, groups, rows):
    return jnp.concatenate([jnp.full((rows, 1), sink_ref[kvh * groups + g], F32) for g in range(groups)], axis=0)


def _attn_kernel(sink_ref, q_ref, kp_ref, kc_ref, kn_ref, vp_ref, vc_ref, vn_ref, kx_ref, vx_ref, o_ref,
                 *, nb, groups):
    kvh, n = pl.program_id(1), pl.program_id(2)
    bl = ATTN_BLOCK
    q = _stack_heads(q_ref[...], groups)
    k = jnp.concatenate([kp_ref[...], kc_ref[...], kn_ref[...], kx_ref[...]], axis=0)
    v = jnp.concatenate([vp_ref[...], vc_ref[...], vn_ref[...], vx_ref[...]], axis=0)
    s = _dot_nt(q, k) * (HEAD_DIM ** -0.5)
    row = lax.broadcasted_iota(jnp.int32, (groups * bl, 3 * bl), 0) % bl
    col = lax.broadcasted_iota(jnp.int32, (groups * bl, 3 * bl), 1)
    valid = ((jnp.abs(col - bl - row) <= WINDOW)
             & ((col >= bl) | (n > 0)) & ((col < 2 * bl) | (n < nb - 1)))
    s = jnp.concatenate([jnp.where(valid, s[:, :3 * bl], NEG_INF), s[:, 3 * bl:]], axis=1)
    o = _softmax_pv(s, _sink_col(sink_ref, kvh, groups, bl), v)
    o_ref[...] = _unstack_heads(o, groups, bl).astype(BF16)


def _attention(qkv_l, qkv_c, sink, *, batch, seq, ctx_len, d_model, kv_width):
    bl = ATTN_BLOCK
    nb = seq // bl
    n_kv = kv_width // HEAD_DIM
    groups = d_model // kv_width
    gw = groups * HEAD_DIM
    kcol, vcol = d_model // HEAD_DIM, (d_model + kv_width) // HEAD_DIM

    def kv_spec(col0, shift):
        def imap(b, h, n):
            return (b * nb + jnp.clip(n + shift, 0, nb - 1), col0 + h)
        return pl.BlockSpec((bl, HEAD_DIM), imap)

    ctx_k = pl.BlockSpec((ctx_len, HEAD_DIM), lambda b, h, n: (b, kcol + h))
    ctx_v = pl.BlockSpec((ctx_len, HEAD_DIM), lambda b, h, n: (b, vcol + h))
    blocks = (2 * _nbytes((bl, gw), BF16) + 6 * _nbytes((bl, HEAD_DIM), BF16)
              + 2 * _nbytes((ctx_len, HEAD_DIM), BF16))
    temps = 6 * _nbytes((groups * bl, 3 * bl + ctx_len), F32)
    return pl.pallas_call(
        functools.partial(_attn_kernel, nb=nb, groups=groups),
        grid=(batch, n_kv, nb),
        in_specs=[pl.BlockSpec(memory_space=pltpu.SMEM),
                  pl.BlockSpec((bl, gw), lambda b, h, n: (b * nb + n, h)),
                  kv_spec(kcol, -1), kv_spec(kcol, 0), kv_spec(kcol, 1),
                  kv_spec(vcol, -1), kv_spec(vcol, 0), kv_spec(vcol, 1),
                  ctx_k, ctx_v],
        out_specs=pl.BlockSpec((bl, gw), lambda b, h, n: (b * nb + n, h)),
        out_shape=jax.ShapeDtypeStruct((batch * seq, d_model), BF16),
        compiler_params=_params(("parallel", "parallel", "arbitrary"), blocks, temps),
        name="window_attention",
    )(sink, qkv_l, qkv_l, qkv_l, qkv_l, qkv_l, qkv_l, qkv_l, qkv_c, qkv_c)


def _attn_ctx_kernel(sink_ref, q_ref, kx_ref, vx_ref, o_ref, *, groups):
    kvh = pl.program_id(1)
    rows = q_ref.shape[0]
    q = _stack_heads(q_ref[...], groups)
    s = _dot_nt(q, kx_ref[...]) * (HEAD_DIM ** -0.5)
    o = _softmax_pv(s, _sink_col(sink_ref, kvh, groups, rows), vx_ref[...])
    o_ref[...] = _unstack_heads(o, groups, rows).astype(BF16)


def _attention_ctx(qkv_c, sink, *, batch, ctx_len, d_model, kv_width):
    n_kv = kv_width // HEAD_DIM
    groups = d_model // kv_width
    gw = groups * HEAD_DIM
    kcol, vcol = d_model // HEAD_DIM, (d_model + kv_width) // HEAD_DIM
    blocks = 2 * _nbytes((ctx_len, gw), BF16) + 2 * _nbytes((ctx_len, HEAD_DIM), BF16)
    temps = 6 * _nbytes((groups * ctx_len, ctx_len), F32)
    return pl.pallas_call(
        functools.partial(_attn_ctx_kernel, groups=groups),
        grid=(batch, n_kv),
        in_specs=[pl.BlockSpec(memory_space=pltpu.SMEM),
                  pl.BlockSpec((ctx_len, gw), lambda b, h: (b, h)),
                  pl.BlockSpec((ctx_len, HEAD_DIM), lambda b, h: (b, kcol + h)),
                  pl.BlockSpec((ctx_len, HEAD_DIM), lambda b, h: (b, vcol + h))],
        out_specs=pl.BlockSpec((ctx_len, gw), lambda b, h: (b, h)),
        out_shape=jax.ShapeDtypeStruct((batch * ctx_len, d_model), BF16),
        compiler_params=_params(("parallel", "parallel"), blocks, temps),
        name="context_attention",
    )(sink, qkv_c, qkv_c, qkv_c)


CONV_PAD = SUBLANES
CONV_ROWS = 256


def _conv_seq(x_ref, w, o_ref, pad_ref, l2_flag):
    s = x_ref.shape[0]
    zeros = jnp.zeros((CONV_PAD, LANES), F32)
    pad_ref[pl.ds(0, CONV_PAD), :] = zeros
    pad_ref[pl.ds(CONV_PAD, s), :] = x_ref[...]
    pad_ref[pl.ds(CONV_PAD + s, CONV_PAD), :] = zeros
    rows = min(CONV_ROWS, s)
    win = rows + 2 * CONV_PAD

    def body(it, carry):
        r0 = pl.multiple_of(it * rows, rows)
        window = pad_ref[pl.ds(r0, win), :]
        y = jnp.zeros((rows, LANES), F32)
        for k in range(CONV_K):
            off = CONV_PAD - CONV_K // 2 + k
            y = y + pltpu.roll(window, win - off, 0)[:rows] * w[k:k + 1, :]
        y = _silu(y)
        normed = y * lax.rsqrt(jnp.sum(y * y, axis=-1, keepdims=True) + EPS)
        o_ref[pl.ds(r0, rows), :] = jnp.where(l2_flag, normed, y)
        return carry

    lax.fori_loop(0, s // rows, body, 0)


def _dnprep_kernel(xl_ref, xc_ref, w_ref, ol_ref, oc_ref, padl_ref, padc_ref, *, n_l2):
    l2_flag = pl.program_id(1) < n_l2
    w = w_ref[...]
    _conv_seq(xl_ref, w, ol_ref, padl_ref, l2_flag)
    _conv_seq(xc_ref, w, oc_ref, padc_ref, l2_flag)


def _dn_prep(rest_l, rest_c, conv_w, *, batch, seq, ctx_len, d_model):
    ncol = 3 * d_model // LANES
    blocks = 2 * (_nbytes((seq, LANES), F32) + _nbytes((ctx_len, LANES), F32)) + _nbytes((8, LANES), F32)
    scratch = _nbytes((seq + 2 * CONV_PAD, LANES), F32) + _nbytes((ctx_len + 2 * CONV_PAD, LANES), F32)
    return pl.pallas_call(
        functools.partial(_dnprep_kernel, n_l2=2 * d_model // LANES),
        grid=(batch, ncol),
        in_specs=[pl.BlockSpec((seq, LANES), lambda b, c: (b, c)),
                  pl.BlockSpec((ctx_len, LANES), lambda b, c: (b, c)),
                  pl.BlockSpec((CONV_K, LANES), lambda b, c: (0, c))],
        out_specs=[pl.BlockSpec((seq, LANES), lambda b, c: (b, c)),
                   pl.BlockSpec((ctx_len, LANES), lambda b, c: (b, c))],
        out_shape=[jax.ShapeDtypeStruct((batch * seq, 3 * d_model), F32),
                   jax.ShapeDtypeStruct((batch * ctx_len, 3 * d_model), F32)],
        scratch_shapes=[pltpu.VMEM((seq + 2 * CONV_PAD, LANES), F32),
                        pltpu.VMEM((ctx_len + 2 * CONV_PAD, LANES), F32)],
        compiler_params=_params(("parallel", "parallel"), blocks, scratch),
        name="deltanet_prep",
    )(rest_l, rest_c, conv_w)


def _gates_kernel(ba_ref, alog_ref, dtb_ref, o_ref, *, n_heads):
    tm = ba_ref.shape[0]
    x = ba_ref[...]
    lane = lax.broadcasted_iota(jnp.int32, (CHUNK, LANES), 1)
    z = x + dtb_ref[...]
    softplus = jnp.maximum(z, 0.0) + jnp.log(1.0 + jnp.exp(-jnp.abs(z)))
    g = -jnp.exp(alog_ref[...]) * softplus
    beta = _sigmoid(x)
    r = lax.broadcasted_iota(jnp.int32, (CHUNK, CHUNK), 0)
    c = lax.broadcasted_iota(jnp.int32, (CHUNK, CHUNK), 1)
    tri_prefix = (c <= r).astype(BF16)
    tri_suffix = (c >= r).astype(BF16)
    for i in range(tm // CHUNK):
        rows = slice(i * CHUNK, (i + 1) * CHUNK)
        parts = _split3(g[rows])
        pre = sum(_dot(tri_prefix, p) for p in parts)
        suf = sum(_dot(tri_suffix, p) for p in parts)
        csum = jnp.where(lane < 3 * n_heads, pre, suf)
        o_ref[rows, :] = jnp.where(lane < 2 * n_heads, beta[rows], csum)


def _gates(ba, alog_vec, dtb_vec, *, n_heads):
    m = ba.shape[0]
    tm = _pick(m, (1024, 512, 256, 128))
    blocks = 2 * _nbytes((tm, LANES), F32)
    return pl.pallas_call(
        functools.partial(_gates_kernel, n_heads=n_heads),
        grid=(m // tm,),
        in_specs=[pl.BlockSpec((tm, LANES), lambda i: (i, 0)),
                  pl.BlockSpec((1, LANES), lambda i: (0, 0)),
                  pl.BlockSpec((1, LANES), lambda i: (0, 0))],
        out_specs=pl.BlockSpec((tm, LANES), lambda i: (i, 0)),
        out_shape=jax.ShapeDtypeStruct((m, LANES), F32),
        compiler_params=_params(("parallel",), blocks, 8 * _nbytes((tm, LANES), F32)),
        name="deltanet_gates",
    )(ba, alog_vec, dtb_vec)


def _dot3(a_hi, a_lo, b_hi, b_lo):
    return _dot(a_hi, b_hi) + _dot(a_hi, b_lo) + _dot(a_lo, b_hi)


def _unit_tri_inverse(a):
    n = a.shape[0]
    row = lax.broadcasted_iota(jnp.int32, (n, n), 0)
    col = lax.broadcasted_iota(jnp.int32, (n, n), 1)
    a_hi, a_lo = _split2(a)
    x = jnp.where(row == col, 1.0, 0.0) - jnp.where((row // 2) == (col // 2), a, 0.0)
    s = 2
    while s < n:
        couple = ((row // (2 * s)) == (col // (2 * s))) & ((row // s) != (col // s))
        e_hi = jnp.where(couple, a_hi, jnp.zeros_like(a_hi))
        e_lo = jnp.where(couple, a_lo, jnp.zeros_like(a_lo))
        x_hi, x_lo = _split2(x)
        ex_hi, ex_lo = _split2(_dot3(e_hi, e_lo, x_hi, x_lo))
        x = x - _dot3(x_hi, x_lo, ex_hi, ex_lo)
        s *= 2
    return x


def _scan_prep_chunk(refs, ci, dst, sel, scr):
    q_ref, k_ref, v_ref, p_ref, crow_refs = refs
    u_s, w_s, qk_s, qd_s, kdt_s, gl_s = scr
    c = CHUNK
    r0 = pl.multiple_of(ci * c, c)
    d0 = pl.multiple_of(dst * c, c)
    q = q_ref[pl.ds(r0, c), :] * (HEAD_DIM ** -0.5)
    k = k_ref[pl.ds(r0, c), :]
    v = v_ref[pl.ds(r0, c), :]
    gates = sum(_dot(part, sel) for part in _split3(p_ref[pl.ds(r0, c), :]))
    k16, q16 = k.astype(BF16), q.astype(BF16)
    kk = _dot_nt(k16, k16)
    qk = _dot_nt(q16, k16)
    row = lax.broadcasted_iota(jnp.int32, (c, c), 0)
    col = lax.broadcasted_iota(jnp.int32, (c, c), 1)
    eye16 = (row == col).astype(BF16)
    for d in range(2):
        beta = gates[:, d * LANES:(d + 1) * LANES]
        ccol = gates[:, (2 + d) * LANES:(3 + d) * LANES]
        crow = crow_refs[d][0, :, pl.ds(r0, c)]
        incl = (row >= col) if d == 0 else (row <= col)
        strict = (row > col) if d == 0 else (row < col)
        decay = jnp.exp(jnp.where(incl, ccol - crow, NEG_INF))
        a = jnp.where(strict, beta * kk * decay, 0.0)
        x = _unit_tri_inverse(a)
        x_hi, x_lo = _split2(x)
        expc = jnp.exp(ccol)
        ru_hi, ru_lo = _split2(v * beta)
        u = _dot(x_hi, ru_hi) + _dot(x_lo, ru_hi) + _dot(x_hi, ru_lo)
        rw_hi = (k * beta * expc).astype(BF16)
        w = _dot(x_hi, rw_hi) + _dot(x_lo, rw_hi)
        clast = ccol[c - 1:c, :] if d == 0 else ccol[0:1, :]
        kdec = (k * jnp.exp(clast - ccol)).astype(BF16)
        u_s[d, pl.ds(d0, c), :] = u
        w_s[d, pl.ds(d0, c), :] = w.astype(BF16)
        qk_s[d, pl.ds(d0, c), :] = (qk * decay).astype(BF16)
        qd_s[d, pl.ds(d0, c), :] = (q * expc).astype(BF16)
        kdt_s[d, pl.ds(d0, c), :] = _dot_nt(eye16, kdec).astype(BF16)
        gl_s[d, pl.ds(pl.multiple_of(dst * SUBLANES, SUBLANES), SUBLANES), :] = jnp.broadcast_to(
            jnp.exp(clast), (SUBLANES, LANES))


def _scan_step(d, ci, state, scr, o_s):
    u_s, w_s, qk_s, qd_s, kdt_s, gl_s = scr
    c = CHUNK
    r0 = pl.multiple_of(ci * c, c)
    rows = pl.ds(r0, c)
    s16 = state.astype(BF16)
    ws = _dot(jnp.concatenate([w_s[d, rows, :], qd_s[d, rows, :]], axis=0), s16)
    v_new = (u_s[d, rows, :] - ws[:c]).astype(BF16)
    o_s[d, rows, :] = ws[c:] + _dot(qk_s[d, rows, :], v_new)
    gl = gl_s[d, pl.ds(pl.multiple_of(ci * SUBLANES, SUBLANES), SUBLANES), :][0:1, :]
    return state * gl + _dot(kdt_s[d, rows, :], v_new)


def _scan_kernel(ql, kl, vl, pl_l, crfl, crbl, zl, qx, kx, vx, px, crfx, crbx, zx, gain_ref,
                 outl_ref, outx_ref, u_s, w_s, qk_s, qd_s, kdt_s, gl_s, o_s, *, n_heads):
    h = pl.program_id(1)
    c = CHUNK
    ncx, ncl = qx.shape[0] // c, ql.shape[0] // c
    scr = (u_s, w_s, qk_s, qd_s, kdt_s, gl_s)
    r = lax.broadcasted_iota(jnp.int32, (LANES, 4 * LANES), 0)
    t = lax.broadcasted_iota(jnp.int32, (LANES, 4 * LANES), 1) // LANES
    sel = (r == t * n_heads + h).astype(BF16)

    refs_x = (qx, kx, vx, px, (crfx, crbx))
    refs_l = (ql, kl, vl, pl_l, (crfl, crbl))

    def prep_x(i, carry):
        _scan_prep_chunk(refs_x, i, i, sel, scr)
        return carry

    def prep_l(i, carry):
        _scan_prep_chunk(refs_l, i, ncx + i, sel, scr)
        return carry

    lax.fori_loop(0, ncx, prep_x, 0)
    lax.fori_loop(0, ncl, prep_l, 0)

    def serial(base, n):
        def body(it, carry):
            sf, sb = carry
            sf = _scan_step(0, base + it, sf, scr, o_s)
            sb = _scan_step(1, base + n - 1 - it, sb, scr, o_s)
            return sf, sb
        return body

    zero = jnp.zeros((HEAD_DIM, HEAD_DIM), F32)
    states = lax.fori_loop(0, ncx, serial(0, ncx), (zero, zero))
    lax.fori_loop(0, ncl, serial(ncx, ncl), states)

    gain = gain_ref[...]

    def finish(z_ref, out_ref, base, n):
        def body(i, carry):
            r0 = pl.multiple_of(i * c, c)
            src = pl.ds(pl.multiple_of((base + i) * c, c), c)
            o = o_s[0, src, :] + o_s[1, src, :]
            on = o * lax.rsqrt(jnp.mean(o * o, axis=-1, keepdims=True) + EPS) * gain
            out_ref[pl.ds(r0, c), :] = (on * _silu(z_ref[pl.ds(r0, c), :])).astype(BF16)
            return carry
        lax.fori_loop(0, n, body, 0)

    finish(zx, outx_ref, 0, ncx)
    finish(zl, outl_ref, ncx, ncl)


def _delta_scan(dn_l, dn_c, gates_l, gates_c, gates_lt, gates_ct, rest_l, rest_c, gain,
                *, batch, seq, ctx_len, d_model):
    n_heads = d_model // HEAD_DIM
    t = seq + ctx_len
    zcol = 3 * d_model // LANES

    def col_spec(rows, col0):
        return pl.BlockSpec((rows, LANES), lambda b, h: (b, col0 + h))

    def gate_spec(rows):
        return pl.BlockSpec((rows, LANES), lambda b, h: (b, 0))

    def crow_spec(rows, d):
        return pl.BlockSpec((1, 1, rows), lambda b, h: ((2 + d) * n_heads + h, 0, b))

    def side(rows):
        return [col_spec(rows, 0), col_spec(rows, n_heads), col_spec(rows, 2 * n_heads), gate_spec(rows),
                crow_spec(rows, 0), crow_spec(rows, 1), col_spec(rows, zcol)]

    blocks = (5 * _nbytes((t, LANES), F32) + 2 * _nbytes((SUBLANES, t), F32) + _nbytes((t, LANES), BF16))
    scratch_shapes = [pltpu.VMEM((2, t, LANES), F32),
                      pltpu.VMEM((2, t, LANES), BF16),
                      pltpu.VMEM((2, t, LANES), BF16),
                      pltpu.VMEM((2, t, LANES), BF16),
                      pltpu.VMEM((2, t, LANES), BF16),
                      pltpu.VMEM((2, (t // CHUNK) * SUBLANES, LANES), F32),
                      pltpu.VMEM((2, t, LANES), F32)]
    scratch = 2 * 2 * _nbytes((t, LANES), F32) + 4 * 2 * _nbytes((t, LANES), BF16)
    return pl.pallas_call(
        functools.partial(_scan_kernel, n_heads=n_heads),
        grid=(batch, n_heads),
        in_specs=side(seq) + side(ctx_len) + [pl.BlockSpec((1, LANES), lambda b, h: (0, 0))],
        out_specs=[pl.BlockSpec((seq, LANES), lambda b, h: (b, h)),
                   pl.BlockSpec((ctx_len, LANES), lambda b, h: (b, h))],
        out_shape=[jax.ShapeDtypeStruct((batch * seq, d_model), BF16),
                   jax.ShapeDtypeStruct((batch * ctx_len, d_model), BF16)],
        scratch_shapes=scratch_shapes,
        compiler_params=_params(("parallel", "arbitrary"), blocks, scratch + 4 * 1024 * 1024),
        name="delta_scan",
    )(dn_l, dn_l, dn_l, gates_l, gates_lt, gates_lt, rest_l,
      dn_c, dn_c, dn_c, gates_c, gates_ct, gates_ct, rest_c, gain)


def _merge_kernel(att_ref, dn_ref, ga_ref, gd_ref, wa_ref, wd_ref, o_ref):
    y = (_sigmoid(ga_ref[...]) * _dot(att_ref[...], wa_ref[...])
         + _sigmoid(gd_ref[...]) * _dot(dn_ref[...], wd_ref[...]))
    o_ref[...] = y.astype(BF16)


def _merge(att, dn, rest, w_attn, w_delta, *, tm, tn):
    m, d = att.shape
    ga0, gd0 = 4 * d // tn, 5 * d // tn
    blocks = (2 * _nbytes((tm, d), BF16) + 2 * _nbytes((tm, tn), F32) + 2 * _nbytes((d, tn), BF16)
              + _nbytes((tm, tn), BF16))
    return pl.pallas_call(
        _merge_kernel,
        grid=(m // tm, d // tn),
        in_specs=[pl.BlockSpec((tm, d), lambda i, j: (i, 0)),
                  pl.BlockSpec((tm, d), lambda i, j: (i, 0)),
                  pl.BlockSpec((tm, tn), lambda i, j: (i, ga0 + j)),
                  pl.BlockSpec((tm, tn), lambda i, j: (i, gd0 + j)),
                  pl.BlockSpec((d, tn), lambda i, j: (0, j)),
                  pl.BlockSpec((d, tn), lambda i, j: (0, j))],
        out_specs=pl.BlockSpec((tm, tn), lambda i, j: (i, j)),
        out_shape=jax.ShapeDtypeStruct((m, d), BF16),
        compiler_params=_params(("parallel", "arbitrary"), blocks, 4 * _nbytes((tm, tn), F32)),
        name="merge",
    )(att, dn, rest, rest, w_attn, w_delta)


def _resid_kernel(a_ref, w_ref, x_ref, mod_ref, o_ref, *, gate_row):
    o_ref[...] = x_ref[...] + mod_ref[0, gate_row:gate_row + 1, :] * _dot(a_ref[...], w_ref[...])


def _resid_matmul(a, w, x, mods, mod_row, gate_row, *, tm, tn, name):
    m, k = a.shape
    d = w.shape[1]
    blocks = (_nbytes((tm, k), BF16) + _nbytes((k, tn), BF16) + 2 * _nbytes((tm, tn), F32)
              + _nbytes((6, tn), F32))
    return pl.pallas_call(
        functools.partial(_resid_kernel, gate_row=gate_row),
        grid=(m // tm, d // tn),
        in_specs=[pl.BlockSpec((tm, k), lambda i, j: (i, 0)),
                  pl.BlockSpec((k, tn), lambda i, j: (0, j)),
                  pl.BlockSpec((tm, tn), lambda i, j: (i, j)),
                  pl.BlockSpec((1, 6, tn), lambda i, j: (mod_row(i, tm), 0, j))],
        out_specs=pl.BlockSpec((tm, tn), lambda i, j: (i, j)),
        out_shape=jax.ShapeDtypeStruct((m, d), F32),
        compiler_params=_params(("parallel", "arbitrary"), blocks, 2 * _nbytes((tm, tn), F32)),
        name=name,
    )(a, w, x, mods)


def _ffn_up_kernel(x_ref, mod_ref, g_ref, wg_ref, wu_ref, o_ref, h_ref):
    @pl.when(pl.program_id(1) == 0)
    def _():
        h_ref[...] = _norm_mod(x_ref[...], g_ref[...], mod_ref[0, 3:4, :], mod_ref[0, 4:5, :]).astype(BF16)

    h = h_ref[...]
    o_ref[...] = (_silu(_dot(h, wg_ref[...])) * _dot(h, wu_ref[...])).astype(BF16)


def _ffn_up(x, mods, mod_row, gain, w_gate_up, *, tm, tn):
    m, d = x.shape
    f = w_gate_up.shape[1] // 2
    nj = f // tn
    blocks = (_nbytes((tm, d), F32) + _nbytes((6, d), F32) + 2 * _nbytes((d, tn), BF16)
              + _nbytes((tm, tn), BF16))
    return pl.pallas_call(
        _ffn_up_kernel,
        grid=(m // tm, nj),
        in_specs=[pl.BlockSpec((tm, d), lambda i, j: (i, 0)),
                  pl.BlockSpec((1, 6, d), lambda i, j: (mod_row(i, tm), 0, 0)),
                  pl.BlockSpec((1, d), lambda i, j: (0, 0)),
                  pl.BlockSpec((d, tn), lambda i, j: (0, j)),
                  pl.BlockSpec((d, tn), lambda i, j: (0, nj + j))],
        out_specs=pl.BlockSpec((tm, tn), lambda i, j: (i, j)),
        out_shape=jax.ShapeDtypeStruct((m, f), BF16),
        scratch_shapes=[pltpu.VMEM((tm, d), BF16)],
        compiler_params=_params(("parallel", "arbitrary"), blocks,
                                _nbytes((tm, d), BF16) + 4 * _nbytes((tm, tn), F32)),
        name="ffn_up",
    )(x, mods, gain, w_gate_up, w_gate_up)


def _final_norm_kernel(x_ref, g_ref, o_ref):
    x = x_ref[...]
    o_ref[...] = x * lax.rsqrt(jnp.mean(x * x, axis=-1, keepdims=True) + EPS) * g_ref[...]


def _final_norm(x, gain, *, tm):
    m, d = x.shape
    return pl.pallas_call(
        _final_norm_kernel,
        grid=(m // tm,),
        in_specs=[pl.BlockSpec((tm, d), lambda i: (i, 0)), pl.BlockSpec((1, d), lambda i: (0, 0))],
        out_specs=pl.BlockSpec((tm, d), lambda i: (i, 0)),
        out_shape=jax.ShapeDtypeStruct((m, d), F32),
        compiler_params=_params(("parallel",), 2 * _nbytes((tm, d), F32), 2 * _nbytes((tm, d), F32)),
        name="final_norm",
    )(x, gain)


def _rope_tables(seq):
    quarter = HEAD_DIM // 4
    pos = jnp.arange(seq)
    row = (pos // GRID_W).astype(F32)
    col = (pos % GRID_W).astype(F32)
    inv_freq = ROPE_BASE ** (-jnp.arange(quarter, dtype=F32) / quarter)
    ang_r, ang_c = row[:, None] * inv_freq[None], col[:, None] * inv_freq[None]
    cos = jnp.concatenate([jnp.cos(ang_r)] * 2 + [jnp.cos(ang_c)] * 2, axis=1)
    sin = jnp.concatenate([-jnp.sin(ang_r), jnp.sin(ang_r), -jnp.sin(ang_c), jnp.sin(ang_c)], axis=1)
    return cos, sin


def kernel(x, c, ctx, c_ctx, w_mod, b_mod, norm1, norm2, w_in, conv_w, attn_sink, a_log, dt_bias, dn_norm,
           w_attn_proj, w_delta_proj, w_out, w_gate_up, w_down, final_norm):
    batch, seq, d = x.shape
    ctx_len = ctx.shape[1]
    depth = w_mod.shape[0]
    n_heads = d // HEAD_DIM
    kv_width = (n_heads // 4) * HEAD_DIM
    qkv_width = d + 2 * kv_width
    assert seq % CHUNK == 0 and ctx_len % CHUNK == 0 and seq % ATTN_BLOCK == 0
    assert 4 * n_heads <= LANES and (batch * seq) % ctx_len == 0

    off_ba = qkv_width + 4 * d
    w_main = jnp.concatenate([w_in[:, :, :off_ba], w_in[:, :, off_ba + 4 * n_heads:]], axis=2).astype(BF16)
    w_ba = jnp.pad(w_in[:, :, off_ba:off_ba + 4 * n_heads], ((0, 0), (0, 0), (0, LANES - 4 * n_heads))).astype(BF16)
    w_attn16, w_delta16, w_out16 = w_attn_proj.astype(BF16), w_delta_proj.astype(BF16), w_out.astype(BF16)
    w_gu16, w_down16 = w_gate_up.astype(BF16), w_down.astype(BF16)

    lane_pad = ((0, 0), (2 * n_heads, LANES - 4 * n_heads))
    alog_vec = jnp.pad(a_log.reshape(depth, 2 * n_heads), lane_pad).reshape(depth, 1, LANES)
    dtb_vec = jnp.pad(dt_bias.reshape(depth, 2 * n_heads), lane_pad).reshape(depth, 1, LANES)

    cos_l, sin_l = _rope_tables(seq)
    cos_l, sin_l = jnp.tile(cos_l, (batch, 1)), jnp.tile(sin_l, (batch, 1))
    cos_c = jnp.ones((batch * ctx_len, LANES), F32)
    sin_c = jnp.zeros((batch * ctx_len, LANES), F32)

    mod_rows = ((batch + 1 + SUBLANES - 1) // SUBLANES) * SUBLANES
    cc = jnp.concatenate([c, c_ctx[None], jnp.zeros((mod_rows - batch - 1, d), F32)], axis=0)
    mods_all = _modulation(cc, w_mod, b_mod).reshape(depth, mod_rows, 6, d)

    tm_l = _pick(seq, (1024, 512, 256, 128))
    tm_c = _pick(batch * ctx_len, (1024, 512, 256, 128))
    tn = _pick(kv_width, (512, 256, 128))

    def row_l(i, tm):
        return (i * tm) // seq

    def row_c(i, tm):
        return batch

    xl = x.reshape(batch * seq, d)
    xc = ctx.reshape(batch * ctx_len, d)
    dims = dict(batch=batch, seq=seq, ctx_len=ctx_len, d_model=d)

    for l in range(depth):
        mods = mods_all[l]
        need_ctx = l < depth - 1
        g1, g2 = norm1[l][None], norm2[l][None]
        proj = functools.partial(_inproj, gain=g1, w_main=w_main[l], w_ba=w_ba[l], tn=tn,
                                 qk_width=d + kv_width, qkv_width=qkv_width)
        qkv_l, rest_l, ba_l = proj(xl, mods, row_l, cos=cos_l, sin=sin_l, tm=tm_l)
        qkv_c, rest_c, ba_c = proj(xc, mods, row_c, cos=cos_c, sin=sin_c, tm=tm_c)

        att_l = _attention(qkv_l, qkv_c, attn_sink[l], kv_width=kv_width, **dims)

        dn_l, dn_c = _dn_prep(rest_l, rest_c, conv_w[l], **dims)
        gates_l = _gates(ba_l, alog_vec[l], dtb_vec[l], n_heads=n_heads)
        gates_c = _gates(ba_c, alog_vec[l], dtb_vec[l], n_heads=n_heads)
        gates_lt = gates_l.T.reshape(LANES, 1, batch * seq)
        gates_ct = gates_c.T.reshape(LANES, 1, batch * ctx_len)
        do_l, do_c = _delta_scan(dn_l, dn_c, gates_l, gates_c, gates_lt, gates_ct, rest_l, rest_c,
                                 dn_norm[l][None], **dims)

        streams = [(xl, att_l, do_l, rest_l, row_l, tm_l)]
        if need_ctx:
            att_c = _attention_ctx(qkv_c, attn_sink[l], batch=batch, ctx_len=ctx_len, d_model=d, kv_width=kv_width)
            streams.append((xc, att_c, do_c, rest_c, row_c, tm_c))
        outs = []
        for xs, att, dn, rest, row, tm in streams:
            y = _merge(att, dn, rest, w_attn16[l], w_delta16[l], tm=tm, tn=tn)
            xs = _resid_matmul(y, w_out16[l], xs, mods, row, 2, tm=tm, tn=tn, name="out_proj")
            act = _ffn_up(xs, mods, row, g2, w_gu16[l], tm=tm, tn=tn)
            xs = _resid_matmul(act, w_down16[l], xs, mods, row, 5, tm=min(tm, 512), tn=tn, name="ffn_down")
            outs.append(xs)
        xl = outs[0]
        if need_ctx:
            xc = outs[1]

    return _final_norm(xl, final_norm[None], tm=tm_l).reshape(batch, seq, d)
```

```python
import functools

import jax
import jax.numpy as jnp
from jax import lax
from jax.experimental import pallas as pl
from jax.experimental.pallas import tpu as pltpu

F32 = jnp.float32
BF16 = jnp.bfloat16

HEAD_DIM = 128
WINDOW = 128
ATTN_BLOCK = 128
GRID_W = 64
CONV_K = 5
ROPE_BASE = 10000.0
EPS = 1e-6
NEG_INF = -1e30
CHUNK = 128
LANES = 128
SUBLANES = 8
VMEM_CAP_BYTES = 60 * 1024 * 1024


def _vmem_limit(block_bytes, scratch_bytes=0):
    est = 2 * block_bytes + scratch_bytes
    return int(min(VMEM_CAP_BYTES, max(32 * 1024 * 1024, est * 3 // 2)))


def _params(sem, block_bytes, scratch_bytes=0):
    return pltpu.CompilerParams(dimension_semantics=sem,
                                vmem_limit_bytes=_vmem_limit(block_bytes, scratch_bytes))


def _pick(n, candidates):
    for c in candidates:
        if n % c == 0:
            return c
    raise ValueError(f"no tile in {candidates} divides {n}")


def _nbytes(shape, dtype):
    n = 1
    for s in shape:
        n *= s
    return n * jnp.dtype(dtype).itemsize


def _sigmoid(x):
    return 1.0 / (1.0 + jnp.exp(-x))


def _silu(x):
    return x * _sigmoid(x)


def _dot(a, b):
    return jnp.dot(a, b, preferred_element_type=F32)


def _dot_nt(a, b):
    return lax.dot_general(a, b, (((1,), (1,)), ((), ())), preferred_element_type=F32)


def _split2(x):
    hi = x.astype(BF16)
    lo = (x - hi.astype(F32)).astype(BF16)
    return hi, lo


def _split3(x):
    hi = x.astype(BF16)
    r = x - hi.astype(F32)
    mid = r.astype(BF16)
    lo = (r - mid.astype(F32)).astype(BF16)
    return hi, mid, lo


def _mod_kernel(c_ref, w_ref, b_ref, o_ref):
    s = _silu(c_ref[...])
    s_hi, s_lo = _split2(s)
    w = w_ref[0]
    w_hi, w_lo = _split2(w)
    acc = _dot(s_hi, w_hi) + _dot(s_hi, w_lo) + _dot(s_lo, w_hi)
    o_ref[0] = acc + b_ref[0]


def _modulation(cc, w_mod, b_mod):
    depth, d, n = w_mod.shape
    rows = cc.shape[0]
    tn = _pick(n, (512, 256, 128))
    blocks = _nbytes((rows, d), F32) + _nbytes((d, tn), F32) + _nbytes((1, tn), F32) + _nbytes((rows, tn), F32)
    return pl.pallas_call(
        _mod_kernel,
        grid=(depth, n // tn),
        in_specs=[pl.BlockSpec((rows, d), lambda l, j: (0, 0)),
                  pl.BlockSpec((1, d, tn), lambda l, j: (l, 0, j)),
                  pl.BlockSpec((1, 1, tn), lambda l, j: (l, 0, j))],
        out_specs=pl.BlockSpec((1, rows, tn), lambda l, j: (l, 0, j)),
        out_shape=jax.ShapeDtypeStruct((depth, rows, n), F32),
        compiler_params=_params(("parallel", "parallel"), blocks, 3 * _nbytes((d, tn), F32)),
        name="modulation",
    )(cc, w_mod, b_mod.reshape(depth, 1, n))


def _norm_mod(x, gain, shift, scale):
    y = x * lax.rsqrt(jnp.mean(x * x, axis=-1, keepdims=True) + EPS) * gain
    return y * (1.0 + scale) + shift


def _inproj_kernel(x_ref, mod_ref, g_ref, w_ref, wba_ref, cos_ref, sin_ref,
                   qkv_ref, rest_ref, ba_ref, h_ref, *, n_rope, n_qkv):
    j = pl.program_id(1)

    @pl.when(j == 0)
    def _():
        h = _norm_mod(x_ref[...], g_ref[...], mod_ref[0, 0:1, :], mod_ref[0, 1:2, :]).astype(BF16)
        h_ref[...] = h
        ba_ref[...] = _dot(h, wba_ref[...])

    acc = _dot(h_ref[...], w_ref[...])
    tn = acc.shape[1]

    @pl.when(j < n_rope)
    def _():
        reps = tn // HEAD_DIM
        cos = jnp.concatenate([cos_ref[...]] * reps, axis=1)
        sin = jnp.concatenate([sin_ref[...]] * reps, axis=1)
        lane = lax.broadcasted_iota(jnp.int32, acc.shape, 1)
        quarter = HEAD_DIM // 4
        partner = jnp.where((lane & quarter) == 0,
                            pltpu.roll(acc, tn - quarter, 1), pltpu.roll(acc, quarter, 1))
        qkv_ref[...] = (acc * cos + partner * sin).astype(BF16)

    @pl.when((j >= n_rope) & (j < n_qkv))
    def _():
        qkv_ref[...] = acc.astype(BF16)

    @pl.when(j >= n_qkv)
    def _():
        rest_ref[...] = acc


def _inproj(x, mods, mod_row, gain, w_main, w_ba, cos, sin, *, tm, tn, qk_width, qkv_width):
    m, d = x.shape
    n = w_main.shape[1]
    n_rope, n_qkv = qk_width // tn, qkv_width // tn
    nj = n // tn
    blocks = (_nbytes((tm, d), F32) + _nbytes((6, d), F32) + _nbytes((d, tn), BF16) + _nbytes((d, LANES), BF16)
              + 2 * _nbytes((tm, LANES), F32) + _nbytes((tm, tn), BF16) + _nbytes((tm, tn), F32)
              + _nbytes((tm, LANES), F32))
    return pl.pallas_call(
        functools.partial(_inproj_kernel, n_rope=n_rope, n_qkv=n_qkv),
        grid=(m // tm, nj),
        in_specs=[pl.BlockSpec((tm, d), lambda i, j: (i, 0)),
                  pl.BlockSpec((1, 6, d), lambda i, j: (mod_row(i, tm), 0, 0)),
                  pl.BlockSpec((1, d), lambda i, j: (0, 0)),
                  pl.BlockSpec((d, tn), lambda i, j: (0, j)),
                  pl.BlockSpec((d, LANES), lambda i, j: (0, 0)),
                  pl.BlockSpec((tm, LANES), lambda i, j: (i, 0)),
                  pl.BlockSpec((tm, LANES), lambda i, j: (i, 0))],
        out_specs=[pl.BlockSpec((tm, tn), lambda i, j: (i, jnp.minimum(j, n_qkv - 1))),
                   pl.BlockSpec((tm, tn), lambda i, j: (i, jnp.maximum(j - n_qkv, 0))),
                   pl.BlockSpec((tm, LANES), lambda i, j: (i, 0))],
        out_shape=[jax.ShapeDtypeStruct((m, qkv_width), BF16),
                   jax.ShapeDtypeStruct((m, n - qkv_width), F32),
                   jax.ShapeDtypeStruct((m, LANES), F32)],
        scratch_shapes=[pltpu.VMEM((tm, d), BF16)],
        compiler_params=_params(("parallel", "arbitrary"), blocks,
                                _nbytes((tm, d), BF16) + 4 * _nbytes((tm, tn), F32)),
        name="inproj",
    )(x, mods, gain, w_main, w_ba, cos, sin)


def _softmax_pv(s, sink_col, v):
    m = jnp.maximum(jnp.max(s, axis=-1, keepdims=True), sink_col)
    p = jnp.exp(s - m)
    denom = jnp.sum(p, axis=-1, keepdims=True) + jnp.exp(sink_col - m)
    return _dot((p * (1.0 / denom)).astype(BF16), v)


def _stack_heads(q, groups):
    return jnp.concatenate([q[:, g * HEAD_DIM:(g + 1) * HEAD_DIM] for g in range(groups)], axis=0)


def _unstack_heads(o, groups, rows):
    return jnp.concatenate([o[g * rows:(g + 1) * rows] for g in range(groups)], axis=1)


def _sink_col(sink_ref, kvh, groups, rows):
    return jnp.concatenate([jnp.full((rows, 1), sink_ref[kvh * groups + g], F32) for g in range(groups)], axis=0)


def _attn_kernel(sink_ref, q_ref, kp_ref, kc_ref, kn_ref, vp_ref, vc_ref, vn_ref, kx_ref, vx_ref, o_ref,
                 *, nb, groups):
    kvh, n = pl.program_id(1), pl.program_id(2)
    bl = ATTN_BLOCK
    k = jnp.concatenate([kp_ref[...], kc_ref[...], kn_ref[...], kx_ref[...]], axis=0)
    v = jnp.concatenate([vp_ref[...], vc_ref[...], vn_ref[...], vx_ref[...]], axis=0)
    row = lax.broadcasted_iota(jnp.int32, (bl, 3 * bl), 0)
    col = lax.broadcasted_iota(jnp.int32, (bl, 3 * bl), 1)
    valid = ((jnp.abs(col - bl - row) <= WINDOW)
             & ((col >= bl) | (n > 0)) & ((col < 2 * bl) | (n < nb - 1)))

    def scores(g):
        s = _dot_nt(q_ref[:, g * HEAD_DIM:(g + 1) * HEAD_DIM], k) * (HEAD_DIM ** -0.5)
        return jnp.concatenate([jnp.where(valid, s[:, :3 * bl], NEG_INF), s[:, 3 * bl:]], axis=1)

    pending = scores(0)
    for g in range(groups):
        s = pending
        if g + 1 < groups:
            pending = scores(g + 1)
        o = _softmax_pv(s, sink_ref[kvh * groups + g], v)
        o_ref[:, g * HEAD_DIM:(g + 1) * HEAD_DIM] = o.astype(BF16)


def _attention(qkv_l, qkv_c, sink, *, batch, seq, ctx_len, d_model, kv_width):
    bl = ATTN_BLOCK
    nb = seq // bl
    n_kv = kv_width // HEAD_DIM
    groups = d_model // kv_width
    gw = groups * HEAD_DIM
    kcol, vcol = d_model // HEAD_DIM, (d_model + kv_width) // HEAD_DIM

    def kv_spec(col0, shift):
        def imap(b, h, n):
            return (b * nb + jnp.clip(n + shift, 0, nb - 1), col0 + h)
        return pl.BlockSpec((bl, HEAD_DIM), imap)

    ctx_k = pl.BlockSpec((ctx_len, HEAD_DIM), lambda b, h, n: (b, kcol + h))
    ctx_v = pl.BlockSpec((ctx_len, HEAD_DIM), lambda b, h, n: (b, vcol + h))
    blocks = (2 * _nbytes((bl, gw), BF16) + 6 * _nbytes((bl, HEAD_DIM), BF16)
              + 2 * _nbytes((ctx_len, HEAD_DIM), BF16))
    temps = 6 * _nbytes((groups * bl, 3 * bl + ctx_len), F32)
    return pl.pallas_call(
        functools.partial(_attn_kernel, nb=nb, groups=groups),
        grid=(batch, n_kv, nb),
        in_specs=[pl.BlockSpec(memory_space=pltpu.SMEM),
                  pl.BlockSpec((bl, gw), lambda b, h, n: (b * nb + n, h)),
                  kv_spec(kcol, -1), kv_spec(kcol, 0), kv_spec(kcol, 1),
                  kv_spec(vcol, -1), kv_spec(vcol, 0), kv_spec(vcol, 1),
                  ctx_k, ctx_v],
        out_specs=pl.BlockSpec((bl, gw), lambda b, h, n: (b * nb + n, h)),
        out_shape=jax.ShapeDtypeStruct((batch * seq, d_model), BF16),
        compiler_params=_params(("parallel", "parallel", "arbitrary"), blocks, temps),
        name="window_attention",
    )(sink, qkv_l, qkv_l, qkv_l, qkv_l, qkv_l, qkv_l, qkv_l, qkv_c, qkv_c)


def _attn_ctx_kernel(sink_ref, q_ref, kx_ref, vx_ref, o_ref, *, groups):
    kvh = pl.program_id(1)
    rows = q_ref.shape[0]
    q = _stack_heads(q_ref[...], groups)
    s = _dot_nt(q, kx_ref[...]) * (HEAD_DIM ** -0.5)
    o = _softmax_pv(s, _sink_col(sink_ref, kvh, groups, rows), vx_ref[...])
    o_ref[...] = _unstack_heads(o, groups, rows).astype(BF16)


def _attention_ctx(qkv_c, sink, *, batch, ctx_len, d_model, kv_width):
    n_kv = kv_width // HEAD_DIM
    groups = d_model // kv_width
    gw = groups * HEAD_DIM
    kcol, vcol = d_model // HEAD_DIM, (d_model + kv_width) // HEAD_DIM
    blocks = 2 * _nbytes((ctx_len, gw), BF16) + 2 * _nbytes((ctx_len, HEAD_DIM), BF16)
    temps = 6 * _nbytes((groups * ctx_len, ctx_len), F32)
    return pl.pallas_call(
        functools.partial(_attn_ctx_kernel, groups=groups),
        grid=(batch, n_kv),
        in_specs=[pl.BlockSpec(memory_space=pltpu.SMEM),
                  pl.BlockSpec((ctx_len, gw), lambda b, h: (b, h)),
                  pl.BlockSpec((ctx_len, HEAD_DIM), lambda b, h: (b, kcol + h)),
                  pl.BlockSpec((ctx_len, HEAD_DIM), lambda b, h: (b, vcol + h))],
        out_specs=pl.BlockSpec((ctx_len, gw), lambda b, h: (b, h)),
        out_shape=jax.ShapeDtypeStruct((batch * ctx_len, d_model), BF16),
        compiler_params=_params(("parallel", "parallel"), blocks, temps),
        name="context_attention",
    )(sink, qkv_c, qkv_c, qkv_c)


CONV_PAD = SUBLANES
CONV_ROWS = 256


def _conv_seq(x_ref, w, o_ref, pad_ref, l2_flag):
    s = x_ref.shape[0]
    zeros = jnp.zeros((CONV_PAD, LANES), F32)
    pad_ref[pl.ds(0, CONV_PAD), :] = zeros
    pad_ref[pl.ds(CONV_PAD, s), :] = x_ref[...]
    pad_ref[pl.ds(CONV_PAD + s, CONV_PAD), :] = zeros
    rows = min(CONV_ROWS, s)

    def conv_silu(r0):
        y = jnp.zeros((rows, LANES), F32)
        for k in range(CONV_K):
            off = CONV_PAD - CONV_K // 2 + k
            y = y + pad_ref[pl.ds(r0 + off, rows), :] * w[k:k + 1, :]
        return _silu(y)

    def body_l2(it, carry):
        r0 = pl.multiple_of(it * rows, rows)
        y = conv_silu(r0)
        o_ref[pl.ds(r0, rows), :] = y * lax.rsqrt(jnp.sum(y * y, axis=-1, keepdims=True) + EPS)
        return carry

    def body_plain(it, carry):
        r0 = pl.multiple_of(it * rows, rows)
        o_ref[pl.ds(r0, rows), :] = conv_silu(r0)
        return carry

    trips = s // rows
    unroll = 2 if trips % 2 == 0 else 1

    @pl.when(l2_flag)
    def _():
        lax.fori_loop(0, trips, body_l2, 0, unroll=unroll)

    @pl.when(jnp.logical_not(l2_flag))
    def _():
        lax.fori_loop(0, trips, body_plain, 0, unroll=unroll)


def _dnprep_kernel(xl_ref, xc_ref, w_ref, ol_ref, oc_ref, padl_ref, padc_ref, *, n_l2):
    l2_flag = pl.program_id(1) < n_l2
    w = w_ref[...]
    _conv_seq(xl_ref, w, ol_ref, padl_ref, l2_flag)
    _conv_seq(xc_ref, w, oc_ref, padc_ref, l2_flag)


def _dn_prep(rest_l, rest_c, conv_w, *, batch, seq, ctx_len, d_model):
    ncol = 3 * d_model // LANES
    blocks = 2 * (_nbytes((seq, LANES), F32) + _nbytes((ctx_len, LANES), F32)) + _nbytes((8, LANES), F32)
    scratch = _nbytes((seq + 2 * CONV_PAD, LANES), F32) + _nbytes((ctx_len + 2 * CONV_PAD, LANES), F32)
    return pl.pallas_call(
        functools.partial(_dnprep_kernel, n_l2=2 * d_model // LANES),
        grid=(batch, ncol),
        in_specs=[pl.BlockSpec((seq, LANES), lambda b, c: (b, c)),
                  pl.BlockSpec((ctx_len, LANES), lambda b, c: (b, c)),
                  pl.BlockSpec((CONV_K, LANES), lambda b, c: (0, c))],
        out_specs=[pl.BlockSpec((seq, LANES), lambda b, c: (b, c)),
                   pl.BlockSpec((ctx_len, LANES), lambda b, c: (b, c))],
        out_shape=[jax.ShapeDtypeStruct((batch * seq, 3 * d_model), F32),
                   jax.ShapeDtypeStruct((batch * ctx_len, 3 * d_model), F32)],
        scratch_shapes=[pltpu.VMEM((seq + 2 * CONV_PAD, LANES), F32),
                        pltpu.VMEM((ctx_len + 2 * CONV_PAD, LANES), F32)],
        compiler_params=_params(("parallel", "parallel"), blocks, scratch),
        name="deltanet_prep",
    )(rest_l, rest_c, conv_w)


def _gates_kernel(ba_ref, alog_ref, dtb_ref, o_ref, *, n_heads):
    tm = ba_ref.shape[0]
    x = ba_ref[...]
    lane = lax.broadcasted_iota(jnp.int32, (CHUNK, LANES), 1)
    z = x + dtb_ref[...]
    softplus = jnp.maximum(z, 0.0) + jnp.log(1.0 + jnp.exp(-jnp.abs(z)))
    g = -jnp.exp(alog_ref[...]) * softplus
    beta = _sigmoid(x)
    r = lax.broadcasted_iota(jnp.int32, (CHUNK, CHUNK), 0)
    c = lax.broadcasted_iota(jnp.int32, (CHUNK, CHUNK), 1)
    tri_prefix = (c <= r).astype(BF16)
    tri_suffix = (c >= r).astype(BF16)
    for i in range(tm // CHUNK):
        rows = slice(i * CHUNK, (i + 1) * CHUNK)
        parts = _split3(g[rows])
        pre = sum(_dot(tri_prefix, p) for p in parts)
        suf = sum(_dot(tri_suffix, p) for p in parts)
        csum = jnp.where(lane < 3 * n_heads, pre, suf)
        o_ref[rows, :] = jnp.where(lane < 2 * n_heads, beta[rows], csum)


def _gates(ba, alog_vec, dtb_vec, *, n_heads):
    m = ba.shape[0]
    tm = _pick(m, (1024, 512, 256, 128))
    blocks = 2 * _nbytes((tm, LANES), F32)
    return pl.pallas_call(
        functools.partial(_gates_kernel, n_heads=n_heads),
        grid=(m // tm,),
        in_specs=[pl.BlockSpec((tm, LANES), lambda i: (i, 0)),
                  pl.BlockSpec((1, LANES), lambda i: (0, 0)),
                  pl.BlockSpec((1, LANES), lambda i: (0, 0))],
        out_specs=pl.BlockSpec((tm, LANES), lambda i: (i, 0)),
        out_shape=jax.ShapeDtypeStruct((m, LANES), F32),
        compiler_params=_params(("parallel",), blocks, 8 * _nbytes((tm, LANES), F32)),
        name="deltanet_gates",
    )(ba, alog_vec, dtb_vec)


def _dot_split(a, b):
    n = b[0].shape[1]
    lhs = jnp.concatenate(a, axis=1)
    half = jnp.concatenate(b, axis=1)
    out = _dot(lhs, jnp.concatenate([half, half], axis=0))
    return out[:, :n] + out[:, n:]


def _unit_tri_inverse_many(mats):
    n = mats[0].shape[0]
    row = lax.broadcasted_iota(jnp.int32, (n, n), 0)
    col = lax.broadcasted_iota(jnp.int32, (n, n), 1)
    splits = [_split2(a) for a in mats]
    eye = jnp.where(row == col, 1.0, 0.0)
    pair = (row // 2) == (col // 2)
    xs = [eye - jnp.where(pair, a, 0.0) for a in mats]
    s = 2
    while s < n:
        couple = ((row // (2 * s)) == (col // (2 * s))) & ((row // s) != (col // s))
        zero16 = jnp.zeros((n, n), BF16)
        es = [(jnp.where(couple, hi, zero16), jnp.where(couple, lo, zero16)) for hi, lo in splits]
        xsp = [_split2(x) for x in xs]
        exs = [_split2(_dot_split(e, xp)) for e, xp in zip(es, xsp)]
        xs = [x - _dot_split(xp, ex) for x, xp, ex in zip(xs, xsp, exs)]
        s *= 2
    return xs


def _scan_prep_group(refs, ci0, dst0, group, sel, scr):
    q_ref, k_ref, v_ref, p_ref, crow_refs = refs
    u_s, w_s, qk_s, qd_s, kdt_s, gl_s = scr
    c = CHUNK
    row = lax.broadcasted_iota(jnp.int32, (c, c), 0)
    col = lax.broadcasted_iota(jnp.int32, (c, c), 1)
    eye16 = (row == col).astype(BF16)
    r0s = [pl.multiple_of((ci0 + g) * c, c) for g in range(group)]
    qs = [q_ref[pl.ds(r0, c), :] * (HEAD_DIM ** -0.5) for r0 in r0s]
    ks = [k_ref[pl.ds(r0, c), :] for r0 in r0s]
    vs = [v_ref[pl.ds(r0, c), :] for r0 in r0s]
    sel2 = jnp.concatenate([sel, sel], axis=0)
    parts = [_split3(p_ref[pl.ds(r0, c), :]) for r0 in r0s]
    gates = [_dot(jnp.concatenate([hi, mid], axis=1), sel2) + _dot(lo, sel) for hi, mid, lo in parts]
    k16s = [k.astype(BF16) for k in ks]
    kks = [_dot_nt(k16, k16) for k16 in k16s]
    qks = [_dot_nt(q.astype(BF16), k16) for q, k16 in zip(qs, k16s)]
    probs = [(g, d) for g in range(group) for d in range(2)]
    betas = [gates[g][:, d * LANES:(d + 1) * LANES] for g, d in probs]
    ccols = [gates[g][:, (2 + d) * LANES:(3 + d) * LANES] for g, d in probs]
    decays, mats = [], []
    for (g, d), beta, ccol in zip(probs, betas, ccols):
        crow = crow_refs[d][0, :, pl.ds(r0s[g], c)]
        incl = (row >= col) if d == 0 else (row <= col)
        strict = (row > col) if d == 0 else (row < col)
        decay = jnp.exp(jnp.where(incl, ccol - crow, NEG_INF))
        decays.append(decay)
        mats.append(jnp.where(strict, beta * kks[g] * decay, 0.0))
    xsp = [_split2(x) for x in _unit_tri_inverse_many(mats)]
    expcs = [jnp.exp(ccol) for ccol in ccols]
    rus = [_split2(vs[g] * beta) for (g, d), beta in zip(probs, betas)]
    us = [_dot_split(xp, ru) for xp, ru in zip(xsp, rus)]
    rws = [(ks[g] * beta * expc).astype(BF16) for (g, d), beta, expc in zip(probs, betas, expcs)]
    ws = [_dot(jnp.concatenate(xp, axis=1), jnp.concatenate([rw, rw], axis=0)) for xp, rw in zip(xsp, rws)]
    clasts = [ccol[c - 1:c, :] if d == 0 else ccol[0:1, :] for (g, d), ccol in zip(probs, ccols)]
    kdts = [_dot_nt(eye16, (ks[g] * jnp.exp(clast - ccol)).astype(BF16))
            for (g, d), clast, ccol in zip(probs, clasts, ccols)]
    for i, (g, d) in enumerate(probs):
        rows = pl.ds(pl.multiple_of((dst0 + g) * c, c), c)
        u_s[d, rows, :] = us[i]
        w_s[d, rows, :] = ws[i].astype(BF16)
        qk_s[d, rows, :] = (qks[g] * decays[i]).astype(BF16)
        qd_s[d, rows, :] = (qs[g] * expcs[i]).astype(BF16)
        kdt_s[d, rows, :] = kdts[i].astype(BF16)
        gl_s[d, pl.ds(pl.multiple_of((dst0 + g) * SUBLANES, SUBLANES), SUBLANES), :] = jnp.broadcast_to(
            jnp.exp(clasts[i]), (SUBLANES, LANES))


def _scan_step_pair(chunks, states, scr):
    u_s, w_s, qk_s, qd_s, kdt_s, gl_s = scr
    c = CHUNK
    rows = [pl.ds(pl.multiple_of(ci * c, c), c) for ci in chunks]
    dirs = range(2)
    s16 = [st.astype(BF16) for st in states]
    ws = [_dot(jnp.concatenate([w_s[d, rows[d], :], qd_s[d, rows[d], :]], axis=0), s16[d]) for d in dirs]
    v_new = [(u_s[d, rows[d], :] - ws[d][:c]).astype(BF16) for d in dirs]
    outs = [ws[d][c:] + _dot(qk_s[d, rows[d], :], v_new[d]) for d in dirs]
    gls = [gl_s[d, pl.ds(pl.multiple_of(chunks[d] * SUBLANES, SUBLANES), SUBLANES), :][0:1, :] for d in dirs]
    new_states = [states[d] * gls[d] + _dot(kdt_s[d, rows[d], :], v_new[d]) for d in dirs]
    return tuple(new_states), outs


PREP_GROUP = 4


def _scan_kernel(ql, kl, vl, pl_l, crfl, crbl, zl, qx, kx, vx, px, crfx, crbx, zx, gain_ref,
                 outl_ref, outx_ref, u_s, w_s, qk_s, qd_s, kdt_s, gl_s, o_s, *, n_heads):
    h = pl.program_id(1)
    c = CHUNK
    ncx, ncl = qx.shape[0] // c, ql.shape[0] // c
    scr = (u_s, w_s, qk_s, qd_s, kdt_s, gl_s)
    r = lax.broadcasted_iota(jnp.int32, (LANES, 4 * LANES), 0)
    t = lax.broadcasted_iota(jnp.int32, (LANES, 4 * LANES), 1) // LANES
    sel = (r == t * n_heads + h).astype(BF16)

    def prep(refs, n, base):
        group = PREP_GROUP if n % PREP_GROUP == 0 else (2 if n % 2 == 0 else 1)

        def body(i, carry):
            _scan_prep_group(refs, i * group, base + i * group, group, sel, scr)
            return carry
        lax.fori_loop(0, n // group, body, 0)

    prep((qx, kx, vx, px, (crfx, crbx)), ncx, 0)
    prep((ql, kl, vl, pl_l, (crfl, crbl)), ncl, ncx)

    gain = gain_ref[...]

    def finish(z_ref, out_ref, i, o):
        rows = pl.ds(pl.multiple_of(i * c, c), c)
        on = o * lax.rsqrt(jnp.mean(o * o, axis=-1, keepdims=True) + EPS) * gain
        out_ref[rows, :] = (on * _silu(z_ref[rows, :])).astype(BF16)

    def serial(states, base, n, z_ref, out_ref):
        assert n % 2 == 0

        def chunk_rows(i):
            return pl.ds(pl.multiple_of((base + i) * c, c), c)

        def first_half(it, states):
            jf, jb = it, n - 1 - it
            states, outs = _scan_step_pair((base + jf, base + jb), states, scr)
            o_s[0, chunk_rows(jf), :] = outs[0]
            o_s[1, chunk_rows(jb), :] = outs[1]
            return states

        def second_half(it, states):
            jf, jb = it, n - 1 - it
            states, outs = _scan_step_pair((base + jf, base + jb), states, scr)
            finish(z_ref, out_ref, jf, outs[0] + o_s[1, chunk_rows(jf), :])
            finish(z_ref, out_ref, jb, outs[1] + o_s[0, chunk_rows(jb), :])
            return states

        states = lax.fori_loop(0, n // 2, first_half, states)
        return lax.fori_loop(n // 2, n, second_half, states)

    zero = jnp.zeros((HEAD_DIM, HEAD_DIM), F32)
    states = serial((zero, zero), 0, ncx, zx, outx_ref)
    serial(states, ncx, ncl, zl, outl_ref)


def _delta_scan(dn_l, dn_c, gates_l, gates_c, gates_lt, gates_ct, rest_l, rest_c, gain,
                *, batch, seq, ctx_len, d_model):
    n_heads = d_model // HEAD_DIM
    t = seq + ctx_len
    zcol = 3 * d_model // LANES

    def col_spec(rows, col0):
        return pl.BlockSpec((rows, LANES), lambda b, h: (b, col0 + h))

    def gate_spec(rows):
        return pl.BlockSpec((rows, LANES), lambda b, h: (b, 0))

    def crow_spec(rows, d):
        return pl.BlockSpec((1, 1, rows), lambda b, h: ((2 + d) * n_heads + h, 0, b))

    def side(rows):
        return [col_spec(rows, 0), col_spec(rows, n_heads), col_spec(rows, 2 * n_heads), gate_spec(rows),
                crow_spec(rows, 0), crow_spec(rows, 1), col_spec(rows, zcol)]

    blocks = (5 * _nbytes((t, LANES), F32) + 2 * _nbytes((SUBLANES, t), F32) + _nbytes((t, LANES), BF16))
    scratch_shapes = [pltpu.VMEM((2, t, LANES), F32),
                      pltpu.VMEM((2, t, LANES), BF16),
                      pltpu.VMEM((2, t, LANES), BF16),
                      pltpu.VMEM((2, t, LANES), BF16),
                      pltpu.VMEM((2, t, LANES), BF16),
                      pltpu.VMEM((2, (t // CHUNK) * SUBLANES, LANES), F32),
                      pltpu.VMEM((2, t, LANES), F32)]
    scratch = 2 * 2 * _nbytes((t, LANES), F32) + 4 * 2 * _nbytes((t, LANES), BF16)
    return pl.pallas_call(
        functools.partial(_scan_kernel, n_heads=n_heads),
        grid=(batch, n_heads),
        in_specs=side(seq) + side(ctx_len) + [pl.BlockSpec((1, LANES), lambda b, h: (0, 0))],
        out_specs=[pl.BlockSpec((seq, LANES), lambda b, h: (b, h)),
                   pl.BlockSpec((ctx_len, LANES), lambda b, h: (b, h))],
        out_shape=[jax.ShapeDtypeStruct((batch * seq, d_model), BF16),
                   jax.ShapeDtypeStruct((batch * ctx_len, d_model), BF16)],
        scratch_shapes=scratch_shapes,
        compiler_params=_params(("parallel", "arbitrary"), blocks, scratch + 4 * 1024 * 1024),
        name="delta_scan",
    )(dn_l, dn_l, dn_l, gates_l, gates_lt, gates_lt, rest_l,
      dn_c, dn_c, dn_c, gates_c, gates_ct, gates_ct, rest_c, gain)


def _merge_kernel(att_ref, dn_ref, ga_ref, gd_ref, wa_ref, wd_ref, o_ref):
    y = (_sigmoid(ga_ref[...]) * _dot(att_ref[...], wa_ref[...])
         + _sigmoid(gd_ref[...]) * _dot(dn_ref[...], wd_ref[...]))
    o_ref[...] = y.astype(BF16)


def _merge(att, dn, rest, w_attn, w_delta, *, tm, tn):
    m, d = att.shape
    ga0, gd0 = 4 * d // tn, 5 * d // tn
    blocks = (2 * _nbytes((tm, d), BF16) + 2 * _nbytes((tm, tn), F32) + 2 * _nbytes((d, tn), BF16)
              + _nbytes((tm, tn), BF16))
    return pl.pallas_call(
        _merge_kernel,
        grid=(m // tm, d // tn),
        in_specs=[pl.BlockSpec((tm, d), lambda i, j: (i, 0)),
                  pl.BlockSpec((tm, d), lambda i, j: (i, 0)),
                  pl.BlockSpec((tm, tn), lambda i, j: (i, ga0 + j)),
                  pl.BlockSpec((tm, tn), lambda i, j: (i, gd0 + j)),
                  pl.BlockSpec((d, tn), lambda i, j: (0, j)),
                  pl.BlockSpec((d, tn), lambda i, j: (0, j))],
        out_specs=pl.BlockSpec((tm, tn), lambda i, j: (i, j)),
        out_shape=jax.ShapeDtypeStruct((m, d), BF16),
        compiler_params=_params(("parallel", "arbitrary"), blocks, 4 * _nbytes((tm, tn), F32)),
        name="merge",
    )(att, dn, rest, rest, w_attn, w_delta)


def _resid_kernel(a_ref, w_ref, x_ref, mod_ref, o_ref, *, gate_row):
    o_ref[...] = x_ref[...] + mod_ref[0, gate_row:gate_row + 1, :] * _dot(a_ref[...], w_ref[...])


def _resid_matmul(a, w, x, mods, mod_row, gate_row, *, tm, tn, name):
    m, k = a.shape
    d = w.shape[1]
    blocks = (_nbytes((tm, k), BF16) + _nbytes((k, tn), BF16) + 2 * _nbytes((tm, tn), F32)
              + _nbytes((6, tn), F32))
    return pl.pallas_call(
        functools.partial(_resid_kernel, gate_row=gate_row),
        grid=(m // tm, d // tn),
        in_specs=[pl.BlockSpec((tm, k), lambda i, j: (i, 0)),
                  pl.BlockSpec((k, tn), lambda i, j: (0, j)),
                  pl.BlockSpec((tm, tn), lambda i, j: (i, j)),
                  pl.BlockSpec((1, 6, tn), lambda i, j: (mod_row(i, tm), 0, j))],
        out_specs=pl.BlockSpec((tm, tn), lambda i, j: (i, j)),
        out_shape=jax.ShapeDtypeStruct((m, d), F32),
        compiler_params=_params(("parallel", "arbitrary"), blocks, 2 * _nbytes((tm, tn), F32)),
        name=name,
    )(a, w, x, mods)


def _ffn_up_kernel(x_ref, mod_ref, g_ref, wg_ref, wu_ref, o_ref, h_ref):
    @pl.when(pl.program_id(1) == 0)
    def _():
        h_ref[...] = _norm_mod(x_ref[...], g_ref[...], mod_ref[0, 3:4, :], mod_ref[0, 4:5, :]).astype(BF16)

    h = h_ref[...]
    o_ref[...] = (_silu(_dot(h, wg_ref[...])) * _dot(h, wu_ref[...])).astype(BF16)


def _ffn_up(x, mods, mod_row, gain, w_gate_up, *, tm, tn):
    m, d = x.shape
    f = w_gate_up.shape[1] // 2
    nj = f // tn
    blocks = (_nbytes((tm, d), F32) + _nbytes((6, d), F32) + 2 * _nbytes((d, tn), BF16)
              + _nbytes((tm, tn), BF16))
    return pl.pallas_call(
        _ffn_up_kernel,
        grid=(m // tm, nj),
        in_specs=[pl.BlockSpec((tm, d), lambda i, j: (i, 0)),
                  pl.BlockSpec((1, 6, d), lambda i, j: (mod_row(i, tm), 0, 0)),
                  pl.BlockSpec((1, d), lambda i, j: (0, 0)),
                  pl.BlockSpec((d, tn), lambda i, j: (0, j)),
                  pl.BlockSpec((d, tn), lambda i, j: (0, nj + j))],
        out_specs=pl.BlockSpec((tm, tn), lambda i, j: (i, j)),
        out_shape=jax.ShapeDtypeStruct((m, f), BF16),
        scratch_shapes=[pltpu.VMEM((tm, d), BF16)],
        compiler_params=_params(("parallel", "arbitrary"), blocks,
                                _nbytes((tm, d), BF16) + 4 * _nbytes((tm, tn), F32)),
        name="ffn_up",
    )(x, mods, gain, w_gate_up, w_gate_up)


def _final_norm_kernel(x_ref, g_ref, o_ref):
    x = x_ref[...]
    o_ref[...] = x * lax.rsqrt(jnp.mean(x * x, axis=-1, keepdims=True) + EPS) * g_ref[...]


def _final_norm(x, gain, *, tm):
    m, d = x.shape
    return pl.pallas_call(
        _final_norm_kernel,
        grid=(m // tm,),
        in_specs=[pl.BlockSpec((tm, d), lambda i: (i, 0)), pl.BlockSpec((1, d), lambda i: (0, 0))],
        out_specs=pl.BlockSpec((tm, d), lambda i: (i, 0)),
        out_shape=jax.ShapeDtypeStruct((m, d), F32),
        compiler_params=_params(("parallel",), 2 * _nbytes((tm, d), F32), 2 * _nbytes((tm, d), F32)),
        name="final_norm",
    )(x, gain)


def _rope_tables(seq):
    quarter = HEAD_DIM // 4
    pos = jnp.arange(seq)
    row = (pos // GRID_W).astype(F32)
    col = (pos % GRID_W).astype(F32)
    inv_freq = ROPE_BASE ** (-jnp.arange(quarter, dtype=F32) / quarter)
    ang_r, ang_c = row[:, None] * inv_freq[None], col[:, None] * inv_freq[None]
    cos = jnp.concatenate([jnp.cos(ang_r)] * 2 + [jnp.cos(ang_c)] * 2, axis=1)
    sin = jnp.concatenate([-jnp.sin(ang_r), jnp.sin(ang_r), -jnp.sin(ang_c), jnp.sin(ang_c)], axis=1)
    return cos, sin


def kernel(x, c, ctx, c_ctx, w_mod, b_mod, norm1, norm2, w_in, conv_w, attn_sink, a_log, dt_bias, dn_norm,
           w_attn_proj, w_delta_proj, w_out, w_gate_up, w_down, final_norm):
    batch, seq, d = x.shape
    ctx_len = ctx.shape[1]
    depth = w_mod.shape[0]
    n_heads = d // HEAD_DIM
    kv_width = (n_heads // 4) * HEAD_DIM
    qkv_width = d + 2 * kv_width
    assert seq % CHUNK == 0 and ctx_len % CHUNK == 0 and seq % ATTN_BLOCK == 0
    assert 4 * n_heads <= LANES and (batch * seq) % ctx_len == 0

    off_ba = qkv_width + 4 * d
    w_main = jnp.concatenate([w_in[:, :, :off_ba], w_in[:, :, off_ba + 4 * n_heads:]], axis=2).astype(BF16)
    w_ba = jnp.pad(w_in[:, :, off_ba:off_ba + 4 * n_heads], ((0, 0), (0, 0), (0, LANES - 4 * n_heads))).astype(BF16)
    w_attn16, w_delta16, w_out16 = w_attn_proj.astype(BF16), w_delta_proj.astype(BF16), w_out.astype(BF16)
    w_gu16, w_down16 = w_gate_up.astype(BF16), w_down.astype(BF16)

    lane_pad = ((0, 0), (2 * n_heads, LANES - 4 * n_heads))
    alog_vec = jnp.pad(a_log.reshape(depth, 2 * n_heads), lane_pad).reshape(depth, 1, LANES)
    dtb_vec = jnp.pad(dt_bias.reshape(depth, 2 * n_heads), lane_pad).reshape(depth, 1, LANES)

    cos_l, sin_l = _rope_tables(seq)
    cos_l, sin_l = jnp.tile(cos_l, (batch, 1)), jnp.tile(sin_l, (batch, 1))
    cos_c = jnp.ones((batch * ctx_len, LANES), F32)
    sin_c = jnp.zeros((batch * ctx_len, LANES), F32)

    mod_rows = ((batch + 1 + SUBLANES - 1) // SUBLANES) * SUBLANES
    cc = jnp.concatenate([c, c_ctx[None], jnp.zeros((mod_rows - batch - 1, d), F32)], axis=0)
    mods_all = _modulation(cc, w_mod, b_mod).reshape(depth, mod_rows, 6, d)

    tm_l = _pick(seq, (1024, 512, 256, 128))
    tm_c = _pick(batch * ctx_len, (1024, 512, 256, 128))
    tn = _pick(kv_width, (512, 256, 128))

    def row_l(i, tm):
        return (i * tm) // seq

    def row_c(i, tm):
        return batch

    xl = x.reshape(batch * seq, d)
    xc = ctx.reshape(batch * ctx_len, d)
    dims = dict(batch=batch, seq=seq, ctx_len=ctx_len, d_model=d)

    for l in range(depth):
        mods = mods_all[l]
        need_ctx = l < depth - 1
        g1, g2 = norm1[l][None], norm2[l][None]
        proj = functools.partial(_inproj, gain=g1, w_main=w_main[l], w_ba=w_ba[l], tn=tn,
                                 qk_width=d + kv_width, qkv_width=qkv_width)
        qkv_l, rest_l, ba_l = proj(xl, mods, row_l, cos=cos_l, sin=sin_l, tm=tm_l)
        qkv_c, rest_c, ba_c = proj(xc, mods, row_c, cos=cos_c, sin=sin_c, tm=tm_c)

        att_l = _attention(qkv_l, qkv_c, attn_sink[l], kv_width=kv_width, **dims)

        dn_l, dn_c = _dn_prep(rest_l, rest_c, conv_w[l], **dims)
        gates_l = _gates(ba_l, alog_vec[l], dtb_vec[l], n_heads=n_heads)
        gates_c = _gates(ba_c, alog_vec[l], dtb_vec[l], n_heads=n_heads)
        gates_lt = gates_l.T.reshape(LANES, 1, batch * seq)
        gates_ct = gates_c.T.reshape(LANES, 1, batch * ctx_len)
        do_l, do_c = _delta_scan(dn_l, dn_c, gates_l, gates_c, gates_lt, gates_ct, rest_l, rest_c,
                                 dn_norm[l][None], **dims)

        streams = [(xl, att_l, do_l, rest_l, row_l, tm_l)]
        if need_ctx:
            att_c = _attention_ctx(qkv_c, attn_sink[l], batch=batch, ctx_len=ctx_len, d_model=d, kv_width=kv_width)
            streams.append((xc, att_c, do_c, rest_c, row_c, tm_c))
        outs = []
        for xs, att, dn, rest, row, tm in streams:
            y = _merge(att, dn, rest, w_attn16[l], w_delta16[l], tm=tm, tn=tn)
            xs = _resid_matmul(y, w_out16[l], xs, mods, row, 2, tm=tm, tn=tn, name="out_proj")
            act = _ffn_up(xs, mods, row, g2, w_gu16[l], tm=tm, tn=tn)
            xs = _resid_matmul(act, w_down16[l], xs, mods, row, 5, tm=min(tm, 512), tn=tn, name="ffn_down")
            outs.append(xs)
        xl = outs[0]
        if need_ctx:
            xc = outs[1]

    return _final_norm(xl, final_norm[None], tm=tm_l).reshape(batch, seq, d)
```

```python
import functools

import jax
import jax.numpy as jnp
from jax import lax
from jax.experimental import pallas as pl
from jax.experimental.pallas import tpu as pltpu

F32 = jnp.float32
BF16 = jnp.bfloat16

HEAD_DIM = 128
WINDOW = 128
ATTN_BLOCK = 128
GRID_W = 64
CONV_K = 5
ROPE_BASE = 10000.0
EPS = 1e-6
NEG_INF = -1e30
CHUNK = 128
LANES = 128
SUBLANES = 8
VMEM_CAP_BYTES = 60 * 1024 * 1024


def _vmem_limit(block_bytes, scratch_bytes=0):
    est = 2 * block_bytes + scratch_bytes
    return int(min(VMEM_CAP_BYTES, max(32 * 1024 * 1024, est * 3 // 2)))


def _params(sem, block_bytes, scratch_bytes=0):
    return pltpu.CompilerParams(dimension_semantics=sem,
                                vmem_limit_bytes=_vmem_limit(block_bytes, scratch_bytes))


def _pick(n, candidates):
    for c in candidates:
        if n % c == 0:
            return c
    raise ValueError(f"no tile in {candidates} divides {n}")


def _nbytes(shape, dtype):
    n = 1
    for s in shape:
        n *= s
    return n * jnp.dtype(dtype).itemsize


def _sigmoid(x):
    return 1.0 / (1.0 + jnp.exp(-x))


def _silu(x):
    return x * _sigmoid(x)


def _dot(a, b):
    return jnp.dot(a, b, preferred_element_type=F32)


def _dot_nt(a, b):
    return lax.dot_general(a, b, (((1,), (1,)), ((), ())), preferred_element_type=F32)


def _split2(x):
    hi = x.astype(BF16)
    lo = (x - hi.astype(F32)).astype(BF16)
    return hi, lo


def _split3(x):
    hi = x.astype(BF16)
    r = x - hi.astype(F32)
    mid = r.astype(BF16)
    lo = (r - mid.astype(F32)).astype(BF16)
    return hi, mid, lo


def _mod_kernel(c_ref, w_ref, b_ref, o_ref):
    s = _silu(c_ref[...])
    s_hi, s_lo = _split2(s)
    w = w_ref[0]
    w_hi, w_lo = _split2(w)
    acc = _dot(s_hi, w_hi) + _dot(s_hi, w_lo) + _dot(s_lo, w_hi)
    o_ref[0] = acc + b_ref[0]


def _modulation(cc, w_mod, b_mod):
    depth, d, n = w_mod.shape
    rows = cc.shape[0]
    tn = _pick(n, (512, 256, 128))
    blocks = _nbytes((rows, d), F32) + _nbytes((d, tn), F32) + _nbytes((1, tn), F32) + _nbytes((rows, tn), F32)
    return pl.pallas_call(
        _mod_kernel,
        grid=(depth, n // tn),
        in_specs=[pl.BlockSpec((rows, d), lambda l, j: (0, 0)),
                  pl.BlockSpec((1, d, tn), lambda l, j: (l, 0, j)),
                  pl.BlockSpec((1, 1, tn), lambda l, j: (l, 0, j))],
        out_specs=pl.BlockSpec((1, rows, tn), lambda l, j: (l, 0, j)),
        out_shape=jax.ShapeDtypeStruct((depth, rows, n), F32),
        compiler_params=_params(("parallel", "parallel"), blocks, 3 * _nbytes((d, tn), F32)),
        name="modulation",
    )(cc, w_mod, b_mod.reshape(depth, 1, n))


def _norm_mod(x, gain, shift, scale):
    y = x * lax.rsqrt(jnp.mean(x * x, axis=-1, keepdims=True) + EPS) * gain
    return y * (1.0 + scale) + shift


def _inproj_kernel(x_ref, mod_ref, g_ref, w_ref, wba_ref, cos_ref, sin_ref, z_ref, ba_ref, h_ref, *, qk_width):
    j = pl.program_id(1)

    @pl.when(j == 0)
    def _():
        h = _norm_mod(x_ref[...], g_ref[...], mod_ref[0, 0:1, :], mod_ref[0, 1:2, :]).astype(BF16)
        h_ref[...] = h
        ba_ref[...] = _dot(h, wba_ref[...])

    acc = _dot(h_ref[...], w_ref[...])
    tn = acc.shape[1]
    full, rem = divmod(qk_width, tn)

    def rope(t):
        width = t.shape[1]
        cos = jnp.concatenate([cos_ref[...]] * (width // HEAD_DIM), axis=1)
        sin = jnp.concatenate([sin_ref[...]] * (width // HEAD_DIM), axis=1)
        lane = lax.broadcasted_iota(jnp.int32, t.shape, 1)
        quarter = HEAD_DIM // 4
        partner = jnp.where((lane & quarter) == 0, pltpu.roll(t, width - quarter, 1), pltpu.roll(t, quarter, 1))
        return t * cos + partner * sin

    @pl.when(j < full)
    def _():
        z_ref[...] = rope(acc).astype(BF16)

    if rem:
        @pl.when(j == full)
        def _():
            z_ref[...] = jnp.concatenate([rope(acc[:, :rem]), acc[:, rem:]], axis=1).astype(BF16)

    @pl.when(j >= full + (1 if rem else 0))
    def _():
        z_ref[...] = acc.astype(BF16)


def _inproj(x, mods, mod_row, gain, w_main, w_ba, cos, sin, *, tm, tn, qk_width):
    m, d = x.shape
    n = w_main.shape[1]
    blocks = (_nbytes((tm, d), F32) + _nbytes((6, d), F32) + _nbytes((d, tn), BF16) + _nbytes((d, LANES), BF16)
              + 2 * _nbytes((tm, LANES), F32) + _nbytes((tm, tn), BF16) + _nbytes((tm, LANES), F32))
    return pl.pallas_call(
        functools.partial(_inproj_kernel, qk_width=qk_width),
        grid=(m // tm, n // tn),
        in_specs=[pl.BlockSpec((tm, d), lambda i, j: (i, 0)),
                  pl.BlockSpec((1, 6, d), lambda i, j: (mod_row(i, tm), 0, 0)),
                  pl.BlockSpec((1, d), lambda i, j: (0, 0)),
                  pl.BlockSpec((d, tn), lambda i, j: (0, j)),
                  pl.BlockSpec((d, LANES), lambda i, j: (0, 0)),
                  pl.BlockSpec((tm, LANES), lambda i, j: (i, 0)),
                  pl.BlockSpec((tm, LANES), lambda i, j: (i, 0))],
        out_specs=[pl.BlockSpec((tm, tn), lambda i, j: (i, j)),
                   pl.BlockSpec((tm, LANES), lambda i, j: (i, 0))],
        out_shape=[jax.ShapeDtypeStruct((m, n), BF16),
                   jax.ShapeDtypeStruct((m, LANES), F32)],
        scratch_shapes=[pltpu.VMEM((tm, d), BF16)],
        compiler_params=_params(("parallel", "arbitrary"), blocks,
                                _nbytes((tm, d), BF16) + 3 * _nbytes((tm, tn), F32)),
        name="inproj",
    )(x, mods, gain, w_main, w_ba, cos, sin)


def _softmax_pv(s, sink_col, v):
    m = jnp.maximum(jnp.max(s, axis=-1, keepdims=True), sink_col)
    p = jnp.exp(s - m)
    denom = jnp.sum(p, axis=-1, keepdims=True) + jnp.exp(sink_col - m)
    return _dot((p * (1.0 / denom)).astype(BF16), v)


def _stack_heads(q, groups):
    return jnp.concatenate([q[:, g * HEAD_DIM:(g + 1) * HEAD_DIM] for g in range(groups)], axis=0)


def _unstack_heads(o, groups, rows):
    return jnp.concatenate([o[g * rows:(g + 1) * rows] for g in range(groups)], axis=1)


def _sink_col(sink_ref, kvh, groups, rows):
    return jnp.concatenate([jnp.full((rows, 1), sink_ref[kvh * groups + g], F32) for g in range(groups)], axis=0)


def _attn_kernel(sink_ref, q_ref, k0, k1, k2, k3, v0, v1, v2, v3, kx_ref, vx_ref, o_ref, *, npairs, groups):
    kvh, n = pl.program_id(1), pl.program_id(2)
    bl = ATTN_BLOCK
    k = jnp.concatenate([k0[...], k1[...], k2[...], k3[...], kx_ref[...]], axis=0)
    v = jnp.concatenate([v0[...], v1[...], v2[...], v3[...], vx_ref[...]], axis=0)
    row = lax.broadcasted_iota(jnp.int32, (2 * bl, 3 * bl), 0)
    col = lax.broadcasted_iota(jnp.int32, (2 * bl, 3 * bl), 1)
    sub = row // bl
    valid = ((jnp.abs(col - bl - (row % bl)) <= WINDOW)
             & ((col >= bl) | (sub == 1) | (n > 0)) & ((col < 2 * bl) | (sub == 0) | (n < npairs - 1)))

    def scores(g):
        s = _dot_nt(q_ref[:, g * HEAD_DIM:(g + 1) * HEAD_DIM], k) * (HEAD_DIM ** -0.5)
        local = jnp.concatenate([s[:bl, :3 * bl], s[bl:, bl:4 * bl]], axis=0)
        return jnp.concatenate([jnp.where(valid, local, NEG_INF), s[:, 4 * bl:]], axis=1)

    def probs(s, sink):
        m = jnp.maximum(jnp.max(s, axis=-1, keepdims=True), sink)
        p = jnp.exp(s - m)
        denom = jnp.sum(p, axis=-1, keepdims=True) + jnp.exp(sink - m)
        p = (p * (1.0 / denom)).astype(BF16)
        zeros = jnp.zeros((bl, bl), BF16)
        return jnp.concatenate([jnp.concatenate([p[:bl, :3 * bl], zeros, p[:bl, 3 * bl:]], axis=1),
                                jnp.concatenate([zeros, p[bl:, :3 * bl], p[bl:, 3 * bl:]], axis=1)], axis=0)

    pending = scores(0)
    for g in range(groups):
        s = pending
        if g + 1 < groups:
            pending = scores(g + 1)
        o_ref[:, g * HEAD_DIM:(g + 1) * HEAD_DIM] = _dot(probs(s, sink_ref[kvh * groups + g]), v).astype(BF16)


def _attention(z_l, z_c, sink, *, batch, seq, ctx_len, d_model, kv_width):
    bl = ATTN_BLOCK
    nb = seq // bl
    assert nb % 2 == 0
    npairs = nb // 2
    n_kv = kv_width // HEAD_DIM
    groups = d_model // kv_width
    gw = groups * HEAD_DIM
    kcol, vcol = d_model // HEAD_DIM, (d_model + kv_width) // HEAD_DIM

    def kv_spec(col0, shift):
        def imap(b, h, n):
            return (b * nb + jnp.clip(2 * n + shift, 0, nb - 1), col0 + h)
        return pl.BlockSpec((bl, HEAD_DIM), imap)

    ctx_k = pl.BlockSpec((ctx_len, HEAD_DIM), lambda b, h, n: (b, kcol + h))
    ctx_v = pl.BlockSpec((ctx_len, HEAD_DIM), lambda b, h, n: (b, vcol + h))
    blocks = (2 * _nbytes((2 * bl, gw), BF16) + 8 * _nbytes((bl, HEAD_DIM), BF16)
              + 2 * _nbytes((ctx_len, HEAD_DIM), BF16))
    temps = 8 * _nbytes((2 * bl, 4 * bl + ctx_len), F32)
    return pl.pallas_call(
        functools.partial(_attn_kernel, npairs=npairs, groups=groups),
        grid=(batch, n_kv, npairs),
        in_specs=[pl.BlockSpec(memory_space=pltpu.SMEM),
                  pl.BlockSpec((2 * bl, gw), lambda b, h, n: (b * npairs + n, h))]
                 + [kv_spec(kcol, shift) for shift in (-1, 0, 1, 2)]
                 + [kv_spec(vcol, shift) for shift in (-1, 0, 1, 2)]
                 + [ctx_k, ctx_v],
        out_specs=pl.BlockSpec((2 * bl, gw), lambda b, h, n: (b * npairs + n, h)),
        out_shape=jax.ShapeDtypeStruct((batch * seq, d_model), BF16),
        compiler_params=_params(("parallel", "parallel", "arbitrary"), blocks, temps),
        name="window_attention",
    )(sink, *([z_l] * 9), z_c, z_c)


def _attn_ctx_kernel(sink_ref, q_ref, kx_ref, vx_ref, o_ref, *, groups):
    kvh = pl.program_id(1)
    rows = q_ref.shape[0]
    q = _stack_heads(q_ref[...], groups)
    s = _dot_nt(q, kx_ref[...]) * (HEAD_DIM ** -0.5)
    o = _softmax_pv(s, _sink_col(sink_ref, kvh, groups, rows), vx_ref[...])
    o_ref[...] = _unstack_heads(o, groups, rows).astype(BF16)


def _attention_ctx(qkv_c, sink, *, batch, ctx_len, d_model, kv_width):
    n_kv = kv_width // HEAD_DIM
    groups = d_model // kv_width
    gw = groups * HEAD_DIM
    kcol, vcol = d_model // HEAD_DIM, (d_model + kv_width) // HEAD_DIM
    blocks = 2 * _nbytes((ctx_len, gw), BF16) + 2 * _nbytes((ctx_len, HEAD_DIM), BF16)
    temps = 6 * _nbytes((groups * ctx_len, ctx_len), F32)
    return pl.pallas_call(
        functools.partial(_attn_ctx_kernel, groups=groups),
        grid=(batch, n_kv),
        in_specs=[pl.BlockSpec(memory_space=pltpu.SMEM),
                  pl.BlockSpec((ctx_len, gw), lambda b, h: (b, h)),
                  pl.BlockSpec((ctx_len, HEAD_DIM), lambda b, h: (b, kcol + h)),
                  pl.BlockSpec((ctx_len, HEAD_DIM), lambda b, h: (b, vcol + h))],
        out_specs=pl.BlockSpec((ctx_len, gw), lambda b, h: (b, h)),
        out_shape=jax.ShapeDtypeStruct((batch * ctx_len, d_model), BF16),
        compiler_params=_params(("parallel", "parallel"), blocks, temps),
        name="context_attention",
    )(sink, qkv_c, qkv_c, qkv_c)


CONV_PAD = SUBLANES
CONV_ROWS = 256


def _conv_seq(x_ref, w, o_ref, pad_ref, l2_flag, l2_scale):
    s = x_ref.shape[0]
    zeros = jnp.zeros((CONV_PAD, LANES), F32)
    pad_ref[pl.ds(0, CONV_PAD), :] = zeros
    pad_ref[pl.ds(CONV_PAD, s), :] = x_ref[...].astype(F32)
    pad_ref[pl.ds(CONV_PAD + s, CONV_PAD), :] = zeros
    rows = min(CONV_ROWS, s)

    def conv_silu(r0):
        y = jnp.zeros((rows, LANES), F32)
        for k in range(CONV_K):
            off = CONV_PAD - CONV_K // 2 + k
            y = y + pad_ref[pl.ds(r0 + off, rows), :] * w[k:k + 1, :]
        return _silu(y)

    def body_l2(it, carry):
        r0 = pl.multiple_of(it * rows, rows)
        y = conv_silu(r0)
        inv = lax.rsqrt(jnp.sum(y * y, axis=-1, keepdims=True) + EPS) * l2_scale
        o_ref[pl.ds(r0, rows), :] = (y * inv).astype(BF16)
        return carry

    def body_plain(it, carry):
        r0 = pl.multiple_of(it * rows, rows)
        o_ref[pl.ds(r0, rows), :] = conv_silu(r0).astype(BF16)
        return carry

    trips = s // rows
    unroll = 2 if trips % 2 == 0 else 1

    @pl.when(l2_flag)
    def _():
        lax.fori_loop(0, trips, body_l2, 0, unroll=unroll)

    @pl.when(jnp.logical_not(l2_flag))
    def _():
        lax.fori_loop(0, trips, body_plain, 0, unroll=unroll)


def _dnprep_kernel(xl_ref, xc_ref, w_ref, ol_ref, oc_ref, padl_ref, padc_ref, *, n_q):
    col = pl.program_id(1)
    l2_flag = col < 2 * n_q
    l2_scale = jnp.where(col < n_q, HEAD_DIM ** -0.5, 1.0)
    w = w_ref[...]
    _conv_seq(xl_ref, w, ol_ref, padl_ref, l2_flag, l2_scale)
    _conv_seq(xc_ref, w, oc_ref, padc_ref, l2_flag, l2_scale)


def _dn_prep(z_l, z_c, conv_w, *, col0, batch, seq, ctx_len, d_model):
    ncol = 3 * d_model // LANES
    blocks = 2 * (_nbytes((seq, LANES), BF16) + _nbytes((ctx_len, LANES), BF16)) + _nbytes((8, LANES), F32)
    scratch = _nbytes((seq + 2 * CONV_PAD, LANES), F32) + _nbytes((ctx_len + 2 * CONV_PAD, LANES), F32)
    return pl.pallas_call(
        functools.partial(_dnprep_kernel, n_q=d_model // LANES),
        grid=(batch, ncol),
        in_specs=[pl.BlockSpec((seq, LANES), lambda b, c: (b, col0 + c)),
                  pl.BlockSpec((ctx_len, LANES), lambda b, c: (b, col0 + c)),
                  pl.BlockSpec((CONV_K, LANES), lambda b, c: (0, c))],
        out_specs=[pl.BlockSpec((seq, LANES), lambda b, c: (b, c)),
                   pl.BlockSpec((ctx_len, LANES), lambda b, c: (b, c))],
        out_shape=[jax.ShapeDtypeStruct((batch * seq, 3 * d_model), BF16),
                   jax.ShapeDtypeStruct((batch * ctx_len, 3 * d_model), BF16)],
        scratch_shapes=[pltpu.VMEM((seq + 2 * CONV_PAD, LANES), F32),
                        pltpu.VMEM((ctx_len + 2 * CONV_PAD, LANES), F32)],
        compiler_params=_params(("parallel", "parallel"), blocks, scratch + 16 * _nbytes((CONV_ROWS, LANES), F32)),
        name="deltanet_prep",
    )(z_l, z_c, conv_w)


def _gates_kernel(ba_ref, alog_ref, dtb_ref, o_ref, *, n_heads):
    tm = ba_ref.shape[0]
    x = ba_ref[...]
    lane = lax.broadcasted_iota(jnp.int32, (CHUNK, LANES), 1)
    z = x + dtb_ref[...]
    softplus = jnp.maximum(z, 0.0) + jnp.log(1.0 + jnp.exp(-jnp.abs(z)))
    g = -jnp.exp(alog_ref[...]) * softplus
    beta = _sigmoid(x)
    r = lax.broadcasted_iota(jnp.int32, (CHUNK, CHUNK), 0)
    c = lax.broadcasted_iota(jnp.int32, (CHUNK, CHUNK), 1)
    tri_prefix = (c <= r).astype(BF16)
    tri_suffix = (c >= r).astype(BF16)
    for i in range(tm // CHUNK):
        rows = slice(i * CHUNK, (i + 1) * CHUNK)
        parts = _split3(g[rows])
        pre = sum(_dot(tri_prefix, p) for p in parts)
        suf = sum(_dot(tri_suffix, p) for p in parts)
        csum = jnp.where(lane < 3 * n_heads, pre, suf)
        o_ref[rows, :] = jnp.where(lane < 2 * n_heads, beta[rows], csum)


def _gates(ba, alog_vec, dtb_vec, *, n_heads):
    m = ba.shape[0]
    tm = _pick(m, (1024, 512, 256, 128))
    blocks = 2 * _nbytes((tm, LANES), F32)
    return pl.pallas_call(
        functools.partial(_gates_kernel, n_heads=n_heads),
        grid=(m // tm,),
        in_specs=[pl.BlockSpec((tm, LANES), lambda i: (i, 0)),
                  pl.BlockSpec((1, LANES), lambda i: (0, 0)),
                  pl.BlockSpec((1, LANES), lambda i: (0, 0))],
        out_specs=pl.BlockSpec((tm, LANES), lambda i: (i, 0)),
        out_shape=jax.ShapeDtypeStruct((m, LANES), F32),
        compiler_params=_params(("parallel",), blocks, 8 * _nbytes((tm, LANES), F32)),
        name="deltanet_gates",
    )(ba, alog_vec, dtb_vec)


def _dot_split(a, b):
    n = b[0].shape[1]
    lhs = jnp.concatenate(a, axis=1)
    half = jnp.concatenate(b, axis=1)
    out = _dot(lhs, jnp.concatenate([half, half], axis=0))
    return out[:, :n] + out[:, n:]


def _unit_tri_inverse_many(mats):
    n = mats[0].shape[0]
    row = lax.broadcasted_iota(jnp.int32, (n, n), 0)
    col = lax.broadcasted_iota(jnp.int32, (n, n), 1)
    splits = [_split2(a) for a in mats]
    eye = jnp.where(row == col, 1.0, 0.0)
    pair = (row // 2) == (col // 2)
    xs = [eye - jnp.where(pair, a, 0.0) for a in mats]
    s = 2
    while s < n:
        couple = ((row // (2 * s)) == (col // (2 * s))) & ((row // s) != (col // s))
        zero16 = jnp.zeros((n, n), BF16)
        es = [(jnp.where(couple, hi, zero16), jnp.where(couple, lo, zero16)) for hi, lo in splits]
        xsp = [_split2(x) for x in xs]
        exs = [_split2(_dot_split(e, xp)) for e, xp in zip(es, xsp)]
        xs = [x - _dot_split(xp, ex) for x, xp, ex in zip(xs, xsp, exs)]
        s *= 2
    return xs


def _scan_prep_group(refs, ci0, dst0, group, sel, scr):
    q_ref, k_ref, v_ref, p_ref, crow_refs = refs
    u_s, w_s, qk_s, qd_s, kdt_s, gl_s = scr
    c = CHUNK
    row = lax.broadcasted_iota(jnp.int32, (c, c), 0)
    col = lax.broadcasted_iota(jnp.int32, (c, c), 1)
    eye16 = (row == col).astype(BF16)
    r0s = [pl.multiple_of((ci0 + g) * c, c) for g in range(group)]
    q16s = [q_ref[pl.ds(r0, c), :] for r0 in r0s]
    k16s = [k_ref[pl.ds(r0, c), :] for r0 in r0s]
    qs = [q.astype(F32) for q in q16s]
    ks = [k.astype(F32) for k in k16s]
    vs = [v_ref[pl.ds(r0, c), :].astype(F32) for r0 in r0s]
    sel2 = jnp.concatenate([sel, sel], axis=0)
    parts = [_split3(p_ref[pl.ds(r0, c), :]) for r0 in r0s]
    gates = [_dot(jnp.concatenate([hi, mid], axis=1), sel2) + _dot(lo, sel) for hi, mid, lo in parts]
    kks = [_dot_nt(k16, k16) for k16 in k16s]
    qks = [_dot_nt(q16, k16) for q16, k16 in zip(q16s, k16s)]
    probs = [(g, d) for g in range(group) for d in range(2)]
    betas = [gates[g][:, d * LANES:(d + 1) * LANES] for g, d in probs]
    ccols = [gates[g][:, (2 + d) * LANES:(3 + d) * LANES] for g, d in probs]
    decays, mats = [], []
    for (g, d), beta, ccol in zip(probs, betas, ccols):
        crow = crow_refs[d][0, :, pl.ds(r0s[g], c)]
        incl = (row >= col) if d == 0 else (row <= col)
        strict = (row > col) if d == 0 else (row < col)
        decay = jnp.exp(jnp.where(incl, ccol - crow, NEG_INF))
        decays.append(decay)
        mats.append(jnp.where(strict, beta * kks[g] * decay, 0.0))
    xsp = [_split2(x) for x in _unit_tri_inverse_many(mats)]
    expcs = [jnp.exp(ccol) for ccol in ccols]
    rus = [_split2(vs[g] * beta) for (g, d), beta in zip(probs, betas)]
    us = [_dot_split(xp, ru) for xp, ru in zip(xsp, rus)]
    rws = [(ks[g] * beta * expc).astype(BF16) for (g, d), beta, expc in zip(probs, betas, expcs)]
    ws = [_dot(jnp.concatenate(xp, axis=1), jnp.concatenate([rw, rw], axis=0)) for xp, rw in zip(xsp, rws)]
    clasts = [ccol[c - 1:c, :] if d == 0 else ccol[0:1, :] for (g, d), ccol in zip(probs, ccols)]
    kdts = [_dot_nt(eye16, (ks[g] * jnp.exp(clast - ccol)).astype(BF16))
            for (g, d), clast, ccol in zip(probs, clasts, ccols)]
    for i, (g, d) in enumerate(probs):
        rows = pl.ds(pl.multiple_of((dst0 + g) * c, c), c)
        u_s[d, rows, :] = us[i]
        w_s[d, rows, :] = ws[i].astype(BF16)
        qk_s[d, rows, :] = (qks[g] * decays[i]).astype(BF16)
        qd_s[d, rows, :] = (qs[g] * expcs[i]).astype(BF16)
        kdt_s[d, rows, :] = kdts[i].astype(BF16)
        gl_s[d, pl.ds(pl.multiple_of((dst0 + g) * SUBLANES, SUBLANES), SUBLANES), :] = jnp.broadcast_to(
            jnp.exp(clasts[i]), (SUBLANES, LANES))


def _scan_step_pair(chunks, states, scr):
    u_s, w_s, qk_s, qd_s, kdt_s, gl_s = scr
    c = CHUNK
    rows = [pl.ds(pl.multiple_of(ci * c, c), c) for ci in chunks]
    dirs = range(2)
    s16 = [st.astype(BF16) for st in states]
    ws = [_dot(jnp.concatenate([w_s[d, rows[d], :], qd_s[d, rows[d], :]], axis=0), s16[d]) for d in dirs]
    v_new = [(u_s[d, rows[d], :] - ws[d][:c]).astype(BF16) for d in dirs]
    outs = [ws[d][c:] + _dot(qk_s[d, rows[d], :], v_new[d]) for d in dirs]
    gls = [gl_s[d, pl.ds(pl.multiple_of(chunks[d] * SUBLANES, SUBLANES), SUBLANES), :][0:1, :] for d in dirs]
    new_states = [states[d] * gls[d] + _dot(kdt_s[d, rows[d], :], v_new[d]) for d in dirs]
    return tuple(new_states), outs


PREP_GROUP = 4


def _scan_kernel(ql, kl, vl, pl_l, crfl, crbl, zl, qx, kx, vx, px, crfx, crbx, zx, gain_ref,
                 outl_ref, outx_ref, u_s, w_s, qk_s, qd_s, kdt_s, gl_s, o_s, *, n_heads):
    h = pl.program_id(1)
    c = CHUNK
    ncx, ncl = qx.shape[0] // c, ql.shape[0] // c
    scr = (u_s, w_s, qk_s, qd_s, kdt_s, gl_s)
    r = lax.broadcasted_iota(jnp.int32, (LANES, 4 * LANES), 0)
    t = lax.broadcasted_iota(jnp.int32, (LANES, 4 * LANES), 1) // LANES
    sel = (r == t * n_heads + h).astype(BF16)

    def prep(refs, n, base):
        group = PREP_GROUP if n % PREP_GROUP == 0 else (2 if n % 2 == 0 else 1)

        def body(i, carry):
            _scan_prep_group(refs, i * group, base + i * group, group, sel, scr)
            return carry
        lax.fori_loop(0, n // group, body, 0)

    prep((qx, kx, vx, px, (crfx, crbx)), ncx, 0)
    prep((ql, kl, vl, pl_l, (crfl, crbl)), ncl, ncx)

    gain = gain_ref[...]

    def finish(z_ref, out_ref, i, o):
        rows = pl.ds(pl.multiple_of(i * c, c), c)
        on = o * lax.rsqrt(jnp.mean(o * o, axis=-1, keepdims=True) + EPS) * gain
        out_ref[rows, :] = (on * _silu(z_ref[rows, :].astype(F32))).astype(BF16)

    def serial(states, base, n, z_ref, out_ref):
        assert n % 2 == 0

        def chunk_rows(i):
            return pl.ds(pl.multiple_of((base + i) * c, c), c)

        def first_half(it, states):
            jf, jb = it, n - 1 - it
            states, outs = _scan_step_pair((base + jf, base + jb), states, scr)
            o_s[0, chunk_rows(jf), :] = outs[0]
            o_s[1, chunk_rows(jb), :] = outs[1]
            return states

        def second_half(it, states):
            jf, jb = it, n - 1 - it
            states, outs = _scan_step_pair((base + jf, base + jb), states, scr)
            finish(z_ref, out_ref, jf, outs[0] + o_s[1, chunk_rows(jf), :])
            finish(z_ref, out_ref, jb, outs[1] + o_s[0, chunk_rows(jb), :])
            return states

        states = lax.fori_loop(0, n // 2, first_half, states)
        return lax.fori_loop(n // 2, n, second_half, states)

    zero = jnp.zeros((HEAD_DIM, HEAD_DIM), F32)
    states = serial((zero, zero), 0, ncx, zx, outx_ref)
    serial(states, ncx, ncl, zl, outl_ref)


def _delta_scan(dn_l, dn_c, gates_l, gates_c, gates_lt, gates_ct, z_l, z_c, gain,
                *, zcol, batch, seq, ctx_len, d_model):
    n_heads = d_model // HEAD_DIM
    t = seq + ctx_len

    def col_spec(rows, col0):
        return pl.BlockSpec((rows, LANES), lambda b, h: (b, col0 + h))

    def gate_spec(rows):
        return pl.BlockSpec((rows, LANES), lambda b, h: (b, 0))

    def crow_spec(rows, d):
        return pl.BlockSpec((1, 1, rows), lambda b, h: ((2 + d) * n_heads + h, 0, b))

    def side(rows):
        return [col_spec(rows, 0), col_spec(rows, n_heads), col_spec(rows, 2 * n_heads), gate_spec(rows),
                crow_spec(rows, 0), crow_spec(rows, 1), col_spec(rows, zcol)]

    blocks = (_nbytes((t, LANES), F32) + 2 * _nbytes((SUBLANES, t), F32) + 5 * _nbytes((t, LANES), BF16))
    scratch_shapes = [pltpu.VMEM((2, t, LANES), F32),
                      pltpu.VMEM((2, t, LANES), BF16),
                      pltpu.VMEM((2, t, LANES), BF16),
                      pltpu.VMEM((2, t, LANES), BF16),
                      pltpu.VMEM((2, t, LANES), BF16),
                      pltpu.VMEM((2, (t // CHUNK) * SUBLANES, LANES), F32),
                      pltpu.VMEM((2, t, LANES), F32)]
    scratch = 2 * 2 * _nbytes((t, LANES), F32) + 4 * 2 * _nbytes((t, LANES), BF16)
    return pl.pallas_call(
        functools.partial(_scan_kernel, n_heads=n_heads),
        grid=(batch, n_heads),
        in_specs=side(seq) + side(ctx_len) + [pl.BlockSpec((1, LANES), lambda b, h: (0, 0))],
        out_specs=[pl.BlockSpec((seq, LANES), lambda b, h: (b, h)),
                   pl.BlockSpec((ctx_len, LANES), lambda b, h: (b, h))],
        out_shape=[jax.ShapeDtypeStruct((batch * seq, d_model), BF16),
                   jax.ShapeDtypeStruct((batch * ctx_len, d_model), BF16)],
        scratch_shapes=scratch_shapes,
        compiler_params=_params(("parallel", "arbitrary"), blocks, scratch + 4 * 1024 * 1024),
        name="delta_scan",
    )(dn_l, dn_l, dn_l, gates_l, gates_lt, gates_lt, z_l,
      dn_c, dn_c, dn_c, gates_c, gates_ct, gates_ct, z_c, gain)


def _merge_kernel(att_ref, dn_ref, ga_ref, gd_ref, wa_ref, wd_ref, o_ref):
    y = (_sigmoid(ga_ref[...].astype(F32)) * _dot(att_ref[...], wa_ref[...])
         + _sigmoid(gd_ref[...].astype(F32)) * _dot(dn_ref[...], wd_ref[...]))
    o_ref[...] = y.astype(BF16)


def _merge(att, dn, z, w_attn, w_delta, *, gate_col, tm, tn):
    m, d = att.shape
    ga0, gd0 = gate_col // tn, (gate_col + d) // tn
    blocks = (2 * _nbytes((tm, d), BF16) + 2 * _nbytes((tm, tn), BF16) + 2 * _nbytes((d, tn), BF16)
              + _nbytes((tm, tn), BF16))
    return pl.pallas_call(
        _merge_kernel,
        grid=(m // tm, d // tn),
        in_specs=[pl.BlockSpec((tm, d), lambda i, j: (i, 0)),
                  pl.BlockSpec((tm, d), lambda i, j: (i, 0)),
                  pl.BlockSpec((tm, tn), lambda i, j: (i, ga0 + j)),
                  pl.BlockSpec((tm, tn), lambda i, j: (i, gd0 + j)),
                  pl.BlockSpec((d, tn), lambda i, j: (0, j)),
                  pl.BlockSpec((d, tn), lambda i, j: (0, j))],
        out_specs=pl.BlockSpec((tm, tn), lambda i, j: (i, j)),
        out_shape=jax.ShapeDtypeStruct((m, d), BF16),
        compiler_params=_params(("parallel", "arbitrary"), blocks, 4 * _nbytes((tm, tn), F32)),
        name="merge",
    )(att, dn, z, z, w_attn, w_delta)


def _resid_kernel(a_ref, w_ref, x_ref, mod_ref, o_ref, *, gate_row):
    o_ref[...] = x_ref[...] + mod_ref[0, gate_row:gate_row + 1, :] * _dot(a_ref[...], w_ref[...])


def _resid_matmul(a, w, x, mods, mod_row, gate_row, *, tm, tn, name):
    m, k = a.shape
    d = w.shape[1]
    blocks = (_nbytes((tm, k), BF16) + _nbytes((k, tn), BF16) + 2 * _nbytes((tm, tn), F32)
              + _nbytes((6, tn), F32))
    return pl.pallas_call(
        functools.partial(_resid_kernel, gate_row=gate_row),
        grid=(m // tm, d // tn),
        in_specs=[pl.BlockSpec((tm, k), lambda i, j: (i, 0)),
                  pl.BlockSpec((k, tn), lambda i, j: (0, j)),
                  pl.BlockSpec((tm, tn), lambda i, j: (i, j)),
                  pl.BlockSpec((1, 6, tn), lambda i, j: (mod_row(i, tm), 0, j))],
        out_specs=pl.BlockSpec((tm, tn), lambda i, j: (i, j)),
        out_shape=jax.ShapeDtypeStruct((m, d), F32),
        compiler_params=_params(("parallel", "arbitrary"), blocks, 2 * _nbytes((tm, tn), F32)),
        name=name,
    )(a, w, x, mods)


def _ffn_up_kernel(x_ref, mod_ref, g_ref, wg_ref, wu_ref, o_ref, h_ref):
    @pl.when(pl.program_id(1) == 0)
    def _():
        h_ref[...] = _norm_mod(x_ref[...], g_ref[...], mod_ref[0, 3:4, :], mod_ref[0, 4:5, :]).astype(BF16)

    h = h_ref[...]
    o_ref[...] = (_silu(_dot(h, wg_ref[...])) * _dot(h, wu_ref[...])).astype(BF16)


def _ffn_up(x, mods, mod_row, gain, w_gate_up, *, tm, tn):
    m, d = x.shape
    f = w_gate_up.shape[1] // 2
    nj = f // tn
    blocks = (_nbytes((tm, d), F32) + _nbytes((6, d), F32) + 2 * _nbytes((d, tn), BF16)
              + _nbytes((tm, tn), BF16))
    return pl.pallas_call(
        _ffn_up_kernel,
        grid=(m // tm, nj),
        in_specs=[pl.BlockSpec((tm, d), lambda i, j: (i, 0)),
                  pl.BlockSpec((1, 6, d), lambda i, j: (mod_row(i, tm), 0, 0)),
                  pl.BlockSpec((1, d), lambda i, j: (0, 0)),
                  pl.BlockSpec((d, tn), lambda i, j: (0, j)),
                  pl.BlockSpec((d, tn), lambda i, j: (0, nj + j))],
        out_specs=pl.BlockSpec((tm, tn), lambda i, j: (i, j)),
        out_shape=jax.ShapeDtypeStruct((m, f), BF16),
        scratch_shapes=[pltpu.VMEM((tm, d), BF16)],
        compiler_params=_params(("parallel", "arbitrary"), blocks,
                                _nbytes((tm, d), BF16) + 4 * _nbytes((tm, tn), F32)),
        name="ffn_up",
    )(x, mods, gain, w_gate_up, w_gate_up)


def _final_norm_kernel(x_ref, g_ref, o_ref):
    x = x_ref[...]
    o_ref[...] = x * lax.rsqrt(jnp.mean(x * x, axis=-1, keepdims=True) + EPS) * g_ref[...]


def _final_norm(x, gain, *, tm):
    m, d = x.shape
    return pl.pallas_call(
        _final_norm_kernel,
        grid=(m // tm,),
        in_specs=[pl.BlockSpec((tm, d), lambda i: (i, 0)), pl.BlockSpec((1, d), lambda i: (0, 0))],
        out_specs=pl.BlockSpec((tm, d), lambda i: (i, 0)),
        out_shape=jax.ShapeDtypeStruct((m, d), F32),
        compiler_params=_params(("parallel",), 2 * _nbytes((tm, d), F32), 2 * _nbytes((tm, d), F32)),
        name="final_norm",
    )(x, gain)


def _rope_tables(seq):
    quarter = HEAD_DIM // 4
    pos = jnp.arange(seq)
    row = (pos // GRID_W).astype(F32)
    col = (pos % GRID_W).astype(F32)
    inv_freq = ROPE_BASE ** (-jnp.arange(quarter, dtype=F32) / quarter)
    ang_r, ang_c = row[:, None] * inv_freq[None], col[:, None] * inv_freq[None]
    cos = jnp.concatenate([jnp.cos(ang_r)] * 2 + [jnp.cos(ang_c)] * 2, axis=1)
    sin = jnp.concatenate([-jnp.sin(ang_r), jnp.sin(ang_r), -jnp.sin(ang_c), jnp.sin(ang_c)], axis=1)
    return cos, sin


def kernel(x, c, ctx, c_ctx, w_mod, b_mod, norm1, norm2, w_in, conv_w, attn_sink, a_log, dt_bias, dn_norm,
           w_attn_proj, w_delta_proj, w_out, w_gate_up, w_down, final_norm):
    batch, seq, d = x.shape
    ctx_len = ctx.shape[1]
    depth = w_mod.shape[0]
    n_heads = d // HEAD_DIM
    kv_width = (n_heads // 4) * HEAD_DIM
    qkv_width = d + 2 * kv_width
    assert seq % CHUNK == 0 and ctx_len % CHUNK == 0 and seq % ATTN_BLOCK == 0
    assert 4 * n_heads <= LANES and (batch * seq) % ctx_len == 0

    off_ba = qkv_width + 4 * d
    w_main = jnp.concatenate([w_in[:, :, :off_ba], w_in[:, :, off_ba + 4 * n_heads:]], axis=2).astype(BF16)
    w_ba = jnp.pad(w_in[:, :, off_ba:off_ba + 4 * n_heads], ((0, 0), (0, 0), (0, LANES - 4 * n_heads))).astype(BF16)
    w_attn16, w_delta16, w_out16 = w_attn_proj.astype(BF16), w_delta_proj.astype(BF16), w_out.astype(BF16)
    w_gu16, w_down16 = w_gate_up.astype(BF16), w_down.astype(BF16)

    lane_pad = ((0, 0), (2 * n_heads, LANES - 4 * n_heads))
    alog_vec = jnp.pad(a_log.reshape(depth, 2 * n_heads), lane_pad).reshape(depth, 1, LANES)
    dtb_vec = jnp.pad(dt_bias.reshape(depth, 2 * n_heads), lane_pad).reshape(depth, 1, LANES)

    cos_l, sin_l = _rope_tables(seq)
    cos_l, sin_l = jnp.tile(cos_l, (batch, 1)), jnp.tile(sin_l, (batch, 1))
    cos_c = jnp.ones((batch * ctx_len, LANES), F32)
    sin_c = jnp.zeros((batch * ctx_len, LANES), F32)

    mod_rows = ((batch + 1 + SUBLANES - 1) // SUBLANES) * SUBLANES
    cc = jnp.concatenate([c, c_ctx[None], jnp.zeros((mod_rows - batch - 1, d), F32)], axis=0)
    mods_all = _modulation(cc, w_mod, b_mod).reshape(depth, mod_rows, 6, d)

    tm_l = _pick(seq, (1024, 512, 256, 128))
    tm_c = _pick(batch * ctx_len, (1024, 512, 256, 128))
    tn = _pick(kv_width, (512, 256, 128))
    tn_in = _pick(w_main.shape[2], (1024, 512, 256, 128))

    def row_l(i, tm):
        return (i * tm) // seq

    def row_c(i, tm):
        return batch

    xl = x.reshape(batch * seq, d)
    xc = ctx.reshape(batch * ctx_len, d)
    dims = dict(batch=batch, seq=seq, ctx_len=ctx_len, d_model=d)

    for l in range(depth):
        mods = mods_all[l]
        need_ctx = l < depth - 1
        g1, g2 = norm1[l][None], norm2[l][None]
        proj = functools.partial(_inproj, gain=g1, w_main=w_main[l], w_ba=w_ba[l], tn=tn_in,
                                 qk_width=d + kv_width)
        z_l, ba_l = proj(xl, mods, row_l, cos=cos_l, sin=sin_l, tm=tm_l)
        z_c, ba_c = proj(xc, mods, row_c, cos=cos_c, sin=sin_c, tm=tm_c)

        att_l = _attention(z_l, z_c, attn_sink[l], kv_width=kv_width, **dims)

        dn_l, dn_c = _dn_prep(z_l, z_c, conv_w[l], col0=qkv_width // LANES, **dims)
        gates_l = _gates(ba_l, alog_vec[l], dtb_vec[l], n_heads=n_heads)
        gates_c = _gates(ba_c, alog_vec[l], dtb_vec[l], n_heads=n_heads)
        gates_lt = gates_l.T.reshape(LANES, 1, batch * seq)
        gates_ct = gates_c.T.reshape(LANES, 1, batch * ctx_len)
        do_l, do_c = _delta_scan(dn_l, dn_c, gates_l, gates_c, gates_lt, gates_ct, z_l, z_c,
                                 dn_norm[l][None], zcol=(qkv_width + 3 * d) // LANES, **dims)

        streams = [(xl, att_l, do_l, z_l, row_l, tm_l)]
        if need_ctx:
            att_c = _attention_ctx(z_c, attn_sink[l], batch=batch, ctx_len=ctx_len, d_model=d, kv_width=kv_width)
            streams.append((xc, att_c, do_c, z_c, row_c, tm_c))
        outs = []
        for xs, att, dn, z, row, tm in streams:
            y = _merge(att, dn, z, w_attn16[l], w_delta16[l], gate_col=qkv_width + 4 * d, tm=tm, tn=tn)
            xs = _resid_matmul(y, w_out16[l], xs, mods, row, 2, tm=tm, tn=tn, name="out_proj")
            act = _ffn_up(xs, mods, row, g2, w_gu16[l], tm=tm, tn=tn)
            xs = _resid_matmul(act, w_down16[l], xs, mods, row, 5, tm=tm, tn=tn, name="ffn_down")
            outs.append(xs)
        xl = outs[0]
        if need_ctx:
            xc = outs[1]

    return _final_norm(xl, final_norm[None], tm=tm_l).reshape(batch, seq, d)
```

```python
import functools

import jax
import jax.numpy as jnp
from jax import lax
from jax.experimental import pallas as pl
from jax.experimental.pallas import tpu as pltpu

F32 = jnp.float32
BF16 = jnp.bfloat16

HEAD_DIM = 128
WINDOW = 128
ATTN_BLOCK = 128
GRID_W = 64
CONV_K = 5
ROPE_BASE = 10000.0
EPS = 1e-6
NEG_INF = -1e30
CHUNK = 128
LANES = 128
SUBLANES = 8
VMEM_CAP_BYTES = 60 * 1024 * 1024


def _vmem_limit(block_bytes, scratch_bytes=0):
    est = 2 * block_bytes + scratch_bytes
    return int(min(VMEM_CAP_BYTES, max(32 * 1024 * 1024, est * 3 // 2)))


def _params(sem, block_bytes, scratch_bytes=0):
    return pltpu.CompilerParams(dimension_semantics=sem,
                                vmem_limit_bytes=_vmem_limit(block_bytes, scratch_bytes))


def _pick(n, candidates):
    for c in candidates:
        if n % c == 0:
            return c
    raise ValueError(f"no tile in {candidates} divides {n}")


def _nbytes(shape, dtype):
    n = 1
    for s in shape:
        n *= s
    return n * jnp.dtype(dtype).itemsize


def _sigmoid(x):
    return 1.0 / (1.0 + jnp.exp(-x))


def _silu(x):
    return x * _sigmoid(x)


def _dot(a, b):
    return jnp.dot(a, b, preferred_element_type=F32)


def _dot_nt(a, b):
    return lax.dot_general(a, b, (((1,), (1,)), ((), ())), preferred_element_type=F32)


def _split2(x):
    hi = x.astype(BF16)
    lo = (x - hi.astype(F32)).astype(BF16)
    return hi, lo


def _split3(x):
    hi = x.astype(BF16)
    r = x - hi.astype(F32)
    mid = r.astype(BF16)
    lo = (r - mid.astype(F32)).astype(BF16)
    return hi, mid, lo


def _mod_kernel(c_ref, w_ref, b_ref, o_ref):
    s = _silu(c_ref[...])
    s_hi, s_lo = _split2(s)
    w = w_ref[0]
    w_hi, w_lo = _split2(w)
    acc = _dot(s_hi, w_hi) + _dot(s_hi, w_lo) + _dot(s_lo, w_hi)
    o_ref[0] = acc + b_ref[0]


def _modulation(cc, w_mod, b_mod):
    depth, d, n = w_mod.shape
    rows = cc.shape[0]
    tn = _pick(n, (512, 256, 128))
    blocks = _nbytes((rows, d), F32) + _nbytes((d, tn), F32) + _nbytes((1, tn), F32) + _nbytes((rows, tn), F32)
    return pl.pallas_call(
        _mod_kernel,
        grid=(depth, n // tn),
        in_specs=[pl.BlockSpec((rows, d), lambda l, j: (0, 0)),
                  pl.BlockSpec((1, d, tn), lambda l, j: (l, 0, j)),
                  pl.BlockSpec((1, 1, tn), lambda l, j: (l, 0, j))],
        out_specs=pl.BlockSpec((1, rows, tn), lambda l, j: (l, 0, j)),
        out_shape=jax.ShapeDtypeStruct((depth, rows, n), F32),
        compiler_params=_params(("parallel", "parallel"), blocks, 3 * _nbytes((d, tn), F32)),
        name="modulation",
    )(cc, w_mod, b_mod.reshape(depth, 1, n))


def _norm_mod(x, gain, shift, scale):
    y = x * lax.rsqrt(jnp.mean(x * x, axis=-1, keepdims=True) + EPS) * gain
    return y * (1.0 + scale) + shift


def _inproj_kernel(x_ref, mod_ref, g_ref, w_ref, wba_ref, cos_ref, sin_ref, z_ref, ba_ref, h_ref, *, qk_width):
    j = pl.program_id(1)

    @pl.when(j == 0)
    def _():
        h = _norm_mod(x_ref[...], g_ref[...], mod_ref[0, 0:1, :], mod_ref[0, 1:2, :]).astype(BF16)
        h_ref[...] = h
        ba_ref[...] = _dot(h, wba_ref[...])

    acc = _dot(h_ref[...], w_ref[...])
    tn = acc.shape[1]
    full, rem = divmod(qk_width, tn)

    def rope(t):
        width = t.shape[1]
        cos = jnp.concatenate([cos_ref[...]] * (width // HEAD_DIM), axis=1)
        sin = jnp.concatenate([sin_ref[...]] * (width // HEAD_DIM), axis=1)
        lane = lax.broadcasted_iota(jnp.int32, t.shape, 1)
        quarter = HEAD_DIM // 4
        partner = jnp.where((lane & quarter) == 0, pltpu.roll(t, width - quarter, 1), pltpu.roll(t, quarter, 1))
        return t * cos + partner * sin

    @pl.when(j < full)
    def _():
        z_ref[...] = rope(acc).astype(BF16)

    if rem:
        @pl.when(j == full)
        def _():
            z_ref[...] = jnp.concatenate([rope(acc[:, :rem]), acc[:, rem:]], axis=1).astype(BF16)

    @pl.when(j >= full + (1 if rem else 0))
    def _():
        z_ref[...] = acc.astype(BF16)


def _inproj(x, mods, mod_row, gain, w_main, w_ba, cos, sin, *, tm, tn, qk_width):
    m, d = x.shape
    n = w_main.shape[1]
    blocks = (_nbytes((tm, d), F32) + _nbytes((6, d), F32) + _nbytes((d, tn), BF16) + _nbytes((d, LANES), BF16)
              + 2 * _nbytes((tm, LANES), F32) + _nbytes((tm, tn), BF16) + _nbytes((tm, LANES), F32))
    return pl.pallas_call(
        functools.partial(_inproj_kernel, qk_width=qk_width),
        grid=(m // tm, n // tn),
        in_specs=[pl.BlockSpec((tm, d), lambda i, j: (i, 0)),
                  pl.BlockSpec((1, 6, d), lambda i, j: (mod_row(i, tm), 0, 0)),
                  pl.BlockSpec((1, d), lambda i, j: (0, 0)),
                  pl.BlockSpec((d, tn), lambda i, j: (0, j)),
                  pl.BlockSpec((d, LANES), lambda i, j: (0, 0)),
                  pl.BlockSpec((tm, LANES), lambda i, j: (i, 0)),
                  pl.BlockSpec((tm, LANES), lambda i, j: (i, 0))],
        out_specs=[pl.BlockSpec((tm, tn), lambda i, j: (i, j)),
                   pl.BlockSpec((tm, LANES), lambda i, j: (i, 0))],
        out_shape=[jax.ShapeDtypeStruct((m, n), BF16),
                   jax.ShapeDtypeStruct((m, LANES), F32)],
        scratch_shapes=[pltpu.VMEM((tm, d), BF16)],
        compiler_params=_params(("parallel", "arbitrary"), blocks,
                                _nbytes((tm, d), BF16) + 3 * _nbytes((tm, tn), F32)),
        name="inproj",
    )(x, mods, gain, w_main, w_ba, cos, sin)


def _softmax_pv(s, sink_col, v):
    m = jnp.maximum(jnp.max(s, axis=-1, keepdims=True), sink_col)
    p = jnp.exp(s - m)
    denom = jnp.sum(p, axis=-1, keepdims=True) + jnp.exp(sink_col - m)
    return _dot((p * (1.0 / denom)).astype(BF16), v)


def _stack_heads(q, groups):
    return jnp.concatenate([q[:, g * HEAD_DIM:(g + 1) * HEAD_DIM] for g in range(groups)], axis=0)


def _unstack_heads(o, groups, rows):
    return jnp.concatenate([o[g * rows:(g + 1) * rows] for g in range(groups)], axis=1)


def _sink_col(sink_ref, kvh, groups, rows):
    return jnp.concatenate([jnp.full((rows, 1), sink_ref[kvh * groups + g], F32) for g in range(groups)], axis=0)


def _attn_kernel(sink_ref, q_ref, k0, k1, k2, k3, v0, v1, v2, v3, kx_ref, vx_ref, o_ref, *, npairs, groups):
    kvh, n = pl.program_id(1), pl.program_id(2)
    bl = ATTN_BLOCK
    k = jnp.concatenate([k0[...], k1[...], k2[...], k3[...], kx_ref[...]], axis=0)
    v = jnp.concatenate([v0[...], v1[...], v2[...], v3[...], vx_ref[...]], axis=0)
    row = lax.broadcasted_iota(jnp.int32, (2 * bl, 3 * bl), 0)
    col = lax.broadcasted_iota(jnp.int32, (2 * bl, 3 * bl), 1)
    sub = row // bl
    valid = ((jnp.abs(col - bl - (row % bl)) <= WINDOW)
             & ((col >= bl) | (sub == 1) | (n > 0)) & ((col < 2 * bl) | (sub == 0) | (n < npairs - 1)))

    def scores(g):
        s = _dot_nt(q_ref[:, g * HEAD_DIM:(g + 1) * HEAD_DIM], k) * (HEAD_DIM ** -0.5)
        local = jnp.concatenate([s[:bl, :3 * bl], s[bl:, bl:4 * bl]], axis=0)
        return jnp.concatenate([jnp.where(valid, local, NEG_INF), s[:, 4 * bl:]], axis=1)

    def probs(s, sink):
        m = jnp.maximum(jnp.max(s, axis=-1, keepdims=True), sink)
        p = jnp.exp(s - m)
        denom = jnp.sum(p, axis=-1, keepdims=True) + jnp.exp(sink - m)
        p = (p * (1.0 / denom)).astype(BF16)
        zeros = jnp.zeros((bl, bl), BF16)
        return jnp.concatenate([jnp.concatenate([p[:bl, :3 * bl], zeros, p[:bl, 3 * bl:]], axis=1),
                                jnp.concatenate([zeros, p[bl:, :3 * bl], p[bl:, 3 * bl:]], axis=1)], axis=0)

    pending = scores(0)
    for g in range(groups):
        s = pending
        if g + 1 < groups:
            pending = scores(g + 1)
        o_ref[:, g * HEAD_DIM:(g + 1) * HEAD_DIM] = _dot(probs(s, sink_ref[kvh * groups + g]), v).astype(BF16)


def _attention(z_l, z_c, sink, *, batch, seq, ctx_len, d_model, kv_width):
    bl = ATTN_BLOCK
    nb = seq // bl
    assert nb % 2 == 0
    npairs = nb // 2
    n_kv = kv_width // HEAD_DIM
    groups = d_model // kv_width
    gw = groups * HEAD_DIM
    kcol, vcol = d_model // HEAD_DIM, (d_model + kv_width) // HEAD_DIM

    def kv_spec(col0, shift):
        def imap(b, h, n):
            return (b * nb + jnp.clip(2 * n + shift, 0, nb - 1), col0 + h)
        return pl.BlockSpec((bl, HEAD_DIM), imap)

    ctx_k = pl.BlockSpec((ctx_len, HEAD_DIM), lambda b, h, n: (b, kcol + h))
    ctx_v = pl.BlockSpec((ctx_len, HEAD_DIM), lambda b, h, n: (b, vcol + h))
    blocks = (2 * _nbytes((2 * bl, gw), BF16) + 8 * _nbytes((bl, HEAD_DIM), BF16)
              + 2 * _nbytes((ctx_len, HEAD_DIM), BF16))
    temps = 8 * _nbytes((2 * bl, 4 * bl + ctx_len), F32)
    return pl.pallas_call(
        functools.partial(_attn_kernel, npairs=npairs, groups=groups),
        grid=(batch, n_kv, npairs),
        in_specs=[pl.BlockSpec(memory_space=pltpu.SMEM),
                  pl.BlockSpec((2 * bl, gw), lambda b, h, n: (b * npairs + n, h))]
                 + [kv_spec(kcol, shift) for shift in (-1, 0, 1, 2)]
                 + [kv_spec(vcol, shift) for shift in (-1, 0, 1, 2)]
                 + [ctx_k, ctx_v],
        out_specs=pl.BlockSpec((2 * bl, gw), lambda b, h, n: (b * npairs + n, h)),
        out_shape=jax.ShapeDtypeStruct((batch * seq, d_model), BF16),
        compiler_params=_params(("parallel", "parallel", "arbitrary"), blocks, temps),
        name="window_attention",
    )(sink, *([z_l] * 9), z_c, z_c)


def _attn_ctx_kernel(sink_ref, q_ref, kx_ref, vx_ref, o_ref, *, groups):
    kvh = pl.program_id(1)
    rows = q_ref.shape[0]
    q = _stack_heads(q_ref[...], groups)
    s = _dot_nt(q, kx_ref[...]) * (HEAD_DIM ** -0.5)
    o = _softmax_pv(s, _sink_col(sink_ref, kvh, groups, rows), vx_ref[...])
    o_ref[...] = _unstack_heads(o, groups, rows).astype(BF16)


def _attention_ctx(qkv_c, sink, *, batch, ctx_len, d_model, kv_width):
    n_kv = kv_width // HEAD_DIM
    groups = d_model // kv_width
    gw = groups * HEAD_DIM
    kcol, vcol = d_model // HEAD_DIM, (d_model + kv_width) // HEAD_DIM
    blocks = 2 * _nbytes((ctx_len, gw), BF16) + 2 * _nbytes((ctx_len, HEAD_DIM), BF16)
    temps = 6 * _nbytes((groups * ctx_len, ctx_len), F32)
    return pl.pallas_call(
        functools.partial(_attn_ctx_kernel, groups=groups),
        grid=(batch, n_kv),
        in_specs=[pl.BlockSpec(memory_space=pltpu.SMEM),
                  pl.BlockSpec((ctx_len, gw), lambda b, h: (b, h)),
                  pl.BlockSpec((ctx_len, HEAD_DIM), lambda b, h: (b, kcol + h)),
                  pl.BlockSpec((ctx_len, HEAD_DIM), lambda b, h: (b, vcol + h))],
        out_specs=pl.BlockSpec((ctx_len, gw), lambda b, h: (b, h)),
        out_shape=jax.ShapeDtypeStruct((batch * ctx_len, d_model), BF16),
        compiler_params=_params(("parallel", "parallel"), blocks, temps),
        name="context_attention",
    )(sink, qkv_c, qkv_c, qkv_c)


CONV_PAD = SUBLANES
CONV_ROWS = 256


def _conv_seq(x_ref, w, o_ref, pad_ref, l2_flag, l2_scale):
    s = x_ref.shape[0]
    zeros = jnp.zeros((CONV_PAD, LANES), F32)
    pad_ref[pl.ds(0, CONV_PAD), :] = zeros
    pad_ref[pl.ds(CONV_PAD, s), :] = x_ref[...].astype(F32)
    pad_ref[pl.ds(CONV_PAD + s, CONV_PAD), :] = zeros
    rows = min(CONV_ROWS, s)

    def conv_silu(r0):
        y = jnp.zeros((rows, LANES), F32)
        for k in range(CONV_K):
            off = CONV_PAD - CONV_K // 2 + k
            y = y + pad_ref[pl.ds(r0 + off, rows), :] * w[k:k + 1, :]
        return _silu(y)

    def body_l2(it, carry):
        r0 = pl.multiple_of(it * rows, rows)
        y = conv_silu(r0)
        inv = lax.rsqrt(jnp.sum(y * y, axis=-1, keepdims=True) + EPS) * l2_scale
        o_ref[pl.ds(r0, rows), :] = (y * inv).astype(BF16)
        return carry

    def body_plain(it, carry):
        r0 = pl.multiple_of(it * rows, rows)
        o_ref[pl.ds(r0, rows), :] = conv_silu(r0).astype(BF16)
        return carry

    trips = s // rows
    unroll = 2 if trips % 2 == 0 else 1

    @pl.when(l2_flag)
    def _():
        lax.fori_loop(0, trips, body_l2, 0, unroll=unroll)

    @pl.when(jnp.logical_not(l2_flag))
    def _():
        lax.fori_loop(0, trips, body_plain, 0, unroll=unroll)


def _dnprep_kernel(xl_ref, xc_ref, w_ref, ol_ref, oc_ref, padl_ref, padc_ref, *, n_q):
    col = pl.program_id(1)
    l2_flag = col < 2 * n_q
    l2_scale = jnp.where(col < n_q, HEAD_DIM ** -0.5, 1.0)
    w = w_ref[...]
    _conv_seq(xl_ref, w, ol_ref, padl_ref, l2_flag, l2_scale)
    _conv_seq(xc_ref, w, oc_ref, padc_ref, l2_flag, l2_scale)


def _dn_prep(z_l, z_c, conv_w, *, col0, batch, seq, ctx_len, d_model):
    ncol = 3 * d_model // LANES
    blocks = 2 * (_nbytes((seq, LANES), BF16) + _nbytes((ctx_len, LANES), BF16)) + _nbytes((8, LANES), F32)
    scratch = _nbytes((seq + 2 * CONV_PAD, LANES), F32) + _nbytes((ctx_len + 2 * CONV_PAD, LANES), F32)
    return pl.pallas_call(
        functools.partial(_dnprep_kernel, n_q=d_model // LANES),
        grid=(batch, ncol),
        in_specs=[pl.BlockSpec((seq, LANES), lambda b, c: (b, col0 + c)),
                  pl.BlockSpec((ctx_len, LANES), lambda b, c: (b, col0 + c)),
                  pl.BlockSpec((CONV_K, LANES), lambda b, c: (0, c))],
        out_specs=[pl.BlockSpec((seq, LANES), lambda b, c: (b, c)),
                   pl.BlockSpec((ctx_len, LANES), lambda b, c: (b, c))],
        out_shape=[jax.ShapeDtypeStruct((batch * seq, 3 * d_model), BF16),
                   jax.ShapeDtypeStruct((batch * ctx_len, 3 * d_model), BF16)],
        scratch_shapes=[pltpu.VMEM((seq + 2 * CONV_PAD, LANES), F32),
                        pltpu.VMEM((ctx_len + 2 * CONV_PAD, LANES), F32)],
        compiler_params=_params(("parallel", "parallel"), blocks, scratch + 16 * _nbytes((CONV_ROWS, LANES), F32)),
        name="deltanet_prep",
    )(z_l, z_c, conv_w)


def _gates_kernel(ba_ref, alog_ref, dtb_ref, o_ref, *, n_heads):
    tm = ba_ref.shape[0]
    x = ba_ref[...]
    lane = lax.broadcasted_iota(jnp.int32, (CHUNK, LANES), 1)
    z = x + dtb_ref[...]
    softplus = jnp.maximum(z, 0.0) + jnp.log(1.0 + jnp.exp(-jnp.abs(z)))
    g = -jnp.exp(alog_ref[...]) * softplus
    beta = _sigmoid(x)
    r = lax.broadcasted_iota(jnp.int32, (CHUNK, CHUNK), 0)
    c = lax.broadcasted_iota(jnp.int32, (CHUNK, CHUNK), 1)
    tri_prefix = (c <= r).astype(BF16)
    tri_suffix = (c >= r).astype(BF16)
    for i in range(tm // CHUNK):
        rows = slice(i * CHUNK, (i + 1) * CHUNK)
        parts = _split3(g[rows])
        pre = sum(_dot(tri_prefix, p) for p in parts)
        suf = sum(_dot(tri_suffix, p) for p in parts)
        csum = jnp.where(lane < 3 * n_heads, pre, suf)
        o_ref[rows, :] = jnp.where(lane < 2 * n_heads, beta[rows], csum)


def _gates(ba, alog_vec, dtb_vec, *, n_heads):
    m = ba.shape[0]
    tm = _pick(m, (1024, 512, 256, 128))
    blocks = 2 * _nbytes((tm, LANES), F32)
    return pl.pallas_call(
        functools.partial(_gates_kernel, n_heads=n_heads),
        grid=(m // tm,),
        in_specs=[pl.BlockSpec((tm, LANES), lambda i: (i, 0)),
                  pl.BlockSpec((1, LANES), lambda i: (0, 0)),
                  pl.BlockSpec((1, LANES), lambda i: (0, 0))],
        out_specs=pl.BlockSpec((tm, LANES), lambda i: (i, 0)),
        out_shape=jax.ShapeDtypeStruct((m, LANES), F32),
        compiler_params=_params(("parallel",), blocks, 8 * _nbytes((tm, LANES), F32)),
        name="deltanet_gates",
    )(ba, alog_vec, dtb_vec)


def _dot_split(a, b):
    n = b[0].shape[1]
    lhs = jnp.concatenate(a, axis=1)
    half = jnp.concatenate(b, axis=1)
    out = _dot(lhs, jnp.concatenate([half, half], axis=0))
    return out[:, :n] + out[:, n:]


def _unit_tri_inverse_many(mats, hooks=()):
    n = mats[0].shape[0]
    row = lax.broadcasted_iota(jnp.int32, (n, n), 0)
    col = lax.broadcasted_iota(jnp.int32, (n, n), 1)
    splits = [_split2(a) for a in mats]
    eye = jnp.where(row == col, 1.0, 0.0)
    pair = (row // 2) == (col // 2)
    xs = [eye - jnp.where(pair, a, 0.0) for a in mats]
    s, level = 2, 0
    while s < n:
        couple = ((row // (2 * s)) == (col // (2 * s))) & ((row // s) != (col // s))
        zero16 = jnp.zeros((n, n), BF16)
        es = [(jnp.where(couple, hi, zero16), jnp.where(couple, lo, zero16)) for hi, lo in splits]
        xsp = [_split2(x) for x in xs]
        exs = [_split2(_dot_split(e, xp)) for e, xp in zip(es, xsp)]
        xs = [x - _dot_split(xp, ex) for x, xp, ex in zip(xs, xsp, exs)]
        if level < len(hooks):
            hooks[level]()
        s *= 2
        level += 1
    return xs


def _scan_prep_group(refs, chunks, dsts, sel, scr, hooks=()):
    q_ref, k_ref, v_ref, p_ref, crow_refs = refs
    u_s, w_s, qk_s, qd_s, kdt_s, gl_s = scr
    c = CHUNK
    group = len(chunks)
    hooks = list(hooks)
    if hooks:
        hooks.pop(0)()
    row = lax.broadcasted_iota(jnp.int32, (c, c), 0)
    col = lax.broadcasted_iota(jnp.int32, (c, c), 1)
    eye16 = (row == col).astype(BF16)
    r0s = [pl.multiple_of(ci * c, c) for ci in chunks]
    q16s = [q_ref[pl.ds(r0, c), :] for r0 in r0s]
    k16s = [k_ref[pl.ds(r0, c), :] for r0 in r0s]
    qs = [q.astype(F32) for q in q16s]
    ks = [k.astype(F32) for k in k16s]
    vs = [v_ref[pl.ds(r0, c), :].astype(F32) for r0 in r0s]
    sel2 = jnp.concatenate([sel, sel], axis=0)
    parts = [_split3(p_ref[pl.ds(r0, c), :]) for r0 in r0s]
    gates = [_dot(jnp.concatenate([hi, mid], axis=1), sel2) + _dot(lo, sel) for hi, mid, lo in parts]
    kks = [_dot_nt(k16, k16) for k16 in k16s]
    qks = [_dot_nt(q16, k16) for q16, k16 in zip(q16s, k16s)]
    probs = [(g, d) for g in range(group) for d in range(2)]
    betas = [gates[g][:, d * LANES:(d + 1) * LANES] for g, d in probs]
    ccols = [gates[g][:, (2 + d) * LANES:(3 + d) * LANES] for g, d in probs]
    decays, mats = [], []
    for (g, d), beta, ccol in zip(probs, betas, ccols):
        crow = crow_refs[d][0, :, pl.ds(r0s[g], c)]
        incl = (row >= col) if d == 0 else (row <= col)
        strict = (row > col) if d == 0 else (row < col)
        decay = jnp.exp(jnp.where(incl, ccol - crow, NEG_INF))
        decays.append(decay)
        mats.append(jnp.where(strict, beta * kks[g] * decay, 0.0))
    if hooks:
        hooks.pop(0)()
    xsp = [_split2(x) for x in _unit_tri_inverse_many(mats, hooks)]
    expcs = [jnp.exp(ccol) for ccol in ccols]
    rus = [_split2(vs[g] * beta) for (g, d), beta in zip(probs, betas)]
    us = [_dot_split(xp, ru) for xp, ru in zip(xsp, rus)]
    rws = [(ks[g] * beta * expc).astype(BF16) for (g, d), beta, expc in zip(probs, betas, expcs)]
    ws = [_dot(jnp.concatenate(xp, axis=1), jnp.concatenate([rw, rw], axis=0)) for xp, rw in zip(xsp, rws)]
    clasts = [ccol[c - 1:c, :] if d == 0 else ccol[0:1, :] for (g, d), ccol in zip(probs, ccols)]
    kdts = [_dot_nt(eye16, (ks[g] * jnp.exp(clast - ccol)).astype(BF16))
            for (g, d), clast, ccol in zip(probs, clasts, ccols)]
    for i, (g, d) in enumerate(probs):
        rows = pl.ds(pl.multiple_of(dsts[g] * c, c), c)
        u_s[d, rows, :] = us[i]
        w_s[d, rows, :] = ws[i].astype(BF16)
        qk_s[d, rows, :] = (qks[g] * decays[i]).astype(BF16)
        qd_s[d, rows, :] = (qs[g] * expcs[i]).astype(BF16)
        kdt_s[d, rows, :] = kdts[i].astype(BF16)
        gl_s[d, pl.ds(pl.multiple_of(dsts[g] * SUBLANES, SUBLANES), SUBLANES), :] = jnp.broadcast_to(
            jnp.exp(clasts[i]), (SUBLANES, LANES))


def _step_issue(chunks, states, scr):
    u_s, w_s, qk_s, qd_s, kdt_s, gl_s = scr
    c = CHUNK
    rows = [pl.ds(pl.multiple_of(ci * c, c), c) for ci in chunks]
    dirs = range(2)
    ws = [_dot(jnp.concatenate([w_s[d, rows[d], :], qd_s[d, rows[d], :]], axis=0), states[d].astype(BF16))
          for d in dirs]
    return dict(chunks=chunks, rows=rows, states=states, ws=ws)


def _step_finish(pending, scr):
    u_s, w_s, qk_s, qd_s, kdt_s, gl_s = scr
    c = CHUNK
    chunks, rows, states, ws = pending["chunks"], pending["rows"], pending["states"], pending["ws"]
    dirs = range(2)
    v_new = [(u_s[d, rows[d], :] - ws[d][:c]).astype(BF16) for d in dirs]
    outs = [ws[d][c:] + _dot(qk_s[d, rows[d], :], v_new[d]) for d in dirs]
    gls = [gl_s[d, pl.ds(pl.multiple_of(chunks[d] * SUBLANES, SUBLANES), SUBLANES), :][0:1, :] for d in dirs]
    new_states = [states[d] * gls[d] + _dot(kdt_s[d, rows[d], :], v_new[d]) for d in dirs]
    return tuple(new_states), outs


def _scan_step_pair(chunks, states, scr):
    return _step_finish(_step_issue(chunks, states, scr), scr)


PREP_GROUP = 4


def _scan_kernel(ql, kl, vl, pl_l, crfl, crbl, zl_prev, qx, kx, vx, px, crfx, crbx, zx, gain_ref,
                 outl_prev_ref, outx_ref, u_s, w_s, qk_s, qd_s, kdt_s, gl_s, o_s, st_s, *, n_heads):
    b, j = pl.program_id(0), pl.program_id(1)
    c = CHUNK
    ncx, ncl = qx.shape[0] // c, ql.shape[0] // c
    half = PREP_GROUP // 2
    n_groups = ncl // PREP_GROUP
    assert PREP_GROUP == 4 and ncx % 2 == 0 and ncl % PREP_GROUP == 0 and ncx <= PREP_GROUP
    cur = j % 2
    factor_refs = (u_s, w_s, qk_s, qd_s, kdt_s, gl_s)
    scr_cur, o_cur = tuple(r.at[cur] for r in factor_refs), o_s.at[cur]
    scr_prev, o_prev = tuple(r.at[1 - cur] for r in factor_refs), o_s.at[1 - cur]
    gain = gain_ref[...]

    def chunk_rows(i):
        return pl.ds(pl.multiple_of(i * c, c), c)

    @pl.when((b == 0) & (j == 0))
    def _():
        def body(i, carry):
            for r in factor_refs[:5] + (o_s,):
                for d in range(2):
                    r[1, d, chunk_rows(i), :] = jnp.zeros((c, LANES), r.dtype)
            return carry
        lax.fori_loop(0, ncx + ncl, body, 0)
        gl_s[1] = jnp.zeros(gl_s.shape[1:], F32)
        st_s[...] = jnp.zeros(st_s.shape, F32)

    def finish(z_ref, out_ref, i, o):
        on = o * lax.rsqrt(jnp.mean(o * o, axis=-1, keepdims=True) + EPS) * gain
        out_ref[chunk_rows(i), :] = (on * _silu(z_ref[chunk_rows(i), :].astype(F32))).astype(BF16)

    def step_stages(scr, o_ref, base, n, it, box, key, z_ref=None, out_ref=None):
        pending = {}
        jf, jb = it, n - 1 - it

        def issue():
            pending.update(_step_issue((base + jf, base + jb), box[key], scr))

        def complete():
            box[key], outs = _step_finish(pending, scr)
            if z_ref is None:
                o_ref[0, chunk_rows(base + jf), :] = outs[0]
                o_ref[1, chunk_rows(base + jb), :] = outs[1]
            else:
                finish(z_ref, out_ref, jf, outs[0] + o_ref[1, chunk_rows(base + jf), :])
                finish(z_ref, out_ref, jb, outs[1] + o_ref[0, chunk_rows(base + jb), :])
        return [issue, complete]

    def run(stages):
        for stage in stages:
            stage()

    def front(i):
        return [half * i + k for k in range(half)]

    def drain_stages(p, box):
        if isinstance(p, int) and p < half:
            return step_stages(scr_prev, o_prev, ncx, ncl, front(n_groups - 1)[p], box, "prev")
        return step_stages(scr_prev, o_prev, ncx, ncl, ncl // 2 + p - half, box, "prev", zl_prev, outl_prev_ref)

    n_drain = half + ncl // 2

    @pl.when(j < n_heads)
    def _():
        r = lax.broadcasted_iota(jnp.int32, (LANES, 4 * LANES), 0)
        t = lax.broadcasted_iota(jnp.int32, (LANES, 4 * LANES), 1) // LANES
        sel = (r == t * n_heads + j).astype(BF16)
        zero = jnp.zeros((HEAD_DIM, HEAD_DIM), F32)
        box = {"cur": (zero, zero), "prev": (st_s[0], st_s[1])}

        _scan_prep_group((qx, kx, vx, px, (crfx, crbx)), list(range(ncx)), list(range(ncx)), sel, scr_cur)
        for it in range(ncx):
            ends = (zx, outx_ref) if it >= ncx // 2 else ()
            run(step_stages(scr_cur, o_cur, 0, ncx, it, box, "cur", *ends))

        def prep_latent(i, hooks):
            chunks = front(i) + [ncl - 1 - ci for ci in reversed(front(i))]
            _scan_prep_group((ql, kl, vl, pl_l, (crfl, crbl)), chunks, [ncx + ci for ci in chunks], sel,
                             scr_cur, hooks)

        prep_latent(0, [stage for p in range(2 * half) for stage in drain_stages(p, box)])

        def overlapped(i, carry):
            box = {"cur": carry[:2], "prev": carry[2:]}
            hooks = []
            for k, it in enumerate(front(i - 1)):
                own = step_stages(scr_cur, o_cur, ncx, ncl, it, box, "cur")
                other = drain_stages(2 * half + half * (i - 1) + k, box)
                hooks += [own[0], other[0], own[1], other[1]]
            prep_latent(i, hooks)
            return tuple(box["cur"]) + tuple(box["prev"])

        carry = lax.fori_loop(1, n_groups, overlapped, tuple(box["cur"]) + tuple(box["prev"]))
        st_s[0] = carry[0]
        st_s[1] = carry[1]

    @pl.when(j == n_heads)
    def _():
        box = {"prev": (st_s[0], st_s[1])}
        for p in range(half):
            run(drain_stages(p, box))

        def body(p, states):
            inner = {"prev": states}
            run(drain_stages(p, inner))
            return tuple(inner["prev"])
        lax.fori_loop(half, n_drain, body, tuple(box["prev"]))


def _delta_scan(dn_l, dn_c, gates_l, gates_c, gates_lt, gates_ct, z_l, z_c, gain,
                *, zcol, batch, seq, ctx_len, d_model):
    n_heads = d_model // HEAD_DIM
    t = seq + ctx_len
    last = n_heads - 1

    def col_spec(rows, col0, lag=0):
        return pl.BlockSpec((rows, LANES), lambda b, j: (b, col0 + jnp.clip(j - lag, 0, last)))

    def gate_spec(rows):
        return pl.BlockSpec((rows, LANES), lambda b, j: (b, 0))

    def crow_spec(rows, d):
        return pl.BlockSpec((1, 1, rows), lambda b, j: ((2 + d) * n_heads + jnp.minimum(j, last), 0, b))

    def side(rows, z_lag):
        return [col_spec(rows, 0), col_spec(rows, n_heads), col_spec(rows, 2 * n_heads), gate_spec(rows),
                crow_spec(rows, 0), crow_spec(rows, 1), col_spec(rows, zcol, z_lag)]

    blocks = (_nbytes((t, LANES), F32) + 2 * _nbytes((SUBLANES, t), F32) + 5 * _nbytes((t, LANES), BF16))
    slots = 2
    scratch_shapes = [pltpu.VMEM((slots, 2, t, LANES), F32),
                      pltpu.VMEM((slots, 2, t, LANES), BF16),
                      pltpu.VMEM((slots, 2, t, LANES), BF16),
                      pltpu.VMEM((slots, 2, t, LANES), BF16),
                      pltpu.VMEM((slots, 2, t, LANES), BF16),
                      pltpu.VMEM((slots, 2, (t // CHUNK) * SUBLANES, LANES), F32),
                      pltpu.VMEM((slots, 2, t, LANES), F32),
                      pltpu.VMEM((2, HEAD_DIM, HEAD_DIM), F32)]
    scratch = slots * (2 * 2 * _nbytes((t, LANES), F32) + 4 * 2 * _nbytes((t, LANES), BF16))
    return pl.pallas_call(
        functools.partial(_scan_kernel, n_heads=n_heads),
        grid=(batch, n_heads + 1),
        in_specs=side(seq, 1) + side(ctx_len, 0) + [pl.BlockSpec((1, LANES), lambda b, j: (0, 0))],
        out_specs=[pl.BlockSpec((seq, LANES), lambda b, j: (b, jnp.clip(j - 1, 0, last))),
                   pl.BlockSpec((ctx_len, LANES), lambda b, j: (b, jnp.minimum(j, last)))],
        out_shape=[jax.ShapeDtypeStruct((batch * seq, d_model), BF16),
                   jax.ShapeDtypeStruct((batch * ctx_len, d_model), BF16)],
        scratch_shapes=scratch_shapes,
        compiler_params=_params(("arbitrary", "arbitrary"), blocks, scratch + 4 * 1024 * 1024),
        name="delta_scan",
    )(dn_l, dn_l, dn_l, gates_l, gates_lt, gates_lt, z_l,
      dn_c, dn_c, dn_c, gates_c, gates_ct, gates_ct, z_c, gain)


def _merge_kernel(att_ref, dn_ref, ga_ref, gd_ref, wa_ref, wd_ref, o_ref):
    y = (_sigmoid(ga_ref[...].astype(F32)) * _dot(att_ref[...], wa_ref[...])
         + _sigmoid(gd_ref[...].astype(F32)) * _dot(dn_ref[...], wd_ref[...]))
    o_ref[...] = y.astype(BF16)


def _merge(att, dn, z, w_attn, w_delta, *, gate_col, tm, tn):
    m, d = att.shape
    ga0, gd0 = gate_col // tn, (gate_col + d) // tn
    blocks = (2 * _nbytes((tm, d), BF16) + 2 * _nbytes((tm, tn), BF16) + 2 * _nbytes((d, tn), BF16)
              + _nbytes((tm, tn), BF16))
    return pl.pallas_call(
        _merge_kernel,
        grid=(m // tm, d // tn),
        in_specs=[pl.BlockSpec((tm, d), lambda i, j: (i, 0)),
                  pl.BlockSpec((tm, d), lambda i, j: (i, 0)),
                  pl.BlockSpec((tm, tn), lambda i, j: (i, ga0 + j)),
                  pl.BlockSpec((tm, tn), lambda i, j: (i, gd0 + j)),
                  pl.BlockSpec((d, tn), lambda i, j: (0, j)),
                  pl.BlockSpec((d, tn), lambda i, j: (0, j))],
        out_specs=pl.BlockSpec((tm, tn), lambda i, j: (i, j)),
        out_shape=jax.ShapeDtypeStruct((m, d), BF16),
        compiler_params=_params(("parallel", "arbitrary"), blocks, 4 * _nbytes((tm, tn), F32)),
        name="merge",
    )(att, dn, z, z, w_attn, w_delta)


def _resid_kernel(a_ref, w_ref, x_ref, mod_ref, o_ref, *, gate_row):
    o_ref[...] = x_ref[...] + mod_ref[0, gate_row:gate_row + 1, :] * _dot(a_ref[...], w_ref[...])


def _resid_matmul(a, w, x, mods, mod_row, gate_row, *, tm, tn, name):
    m, k = a.shape
    d = w.shape[1]
    blocks = (_nbytes((tm, k), BF16) + _nbytes((k, tn), BF16) + 2 * _nbytes((tm, tn), F32)
              + _nbytes((6, tn), F32))
    return pl.pallas_call(
        functools.partial(_resid_kernel, gate_row=gate_row),
        grid=(m // tm, d // tn),
        in_specs=[pl.BlockSpec((tm, k), lambda i, j: (i, 0)),
                  pl.BlockSpec((k, tn), lambda i, j: (0, j)),
                  pl.BlockSpec((tm, tn), lambda i, j: (i, j)),
                  pl.BlockSpec((1, 6, tn), lambda i, j: (mod_row(i, tm), 0, j))],
        out_specs=pl.BlockSpec((tm, tn), lambda i, j: (i, j)),
        out_shape=jax.ShapeDtypeStruct((m, d), F32),
        compiler_params=_params(("parallel", "arbitrary"), blocks, 2 * _nbytes((tm, tn), F32)),
        name=name,
    )(a, w, x, mods)


def _ffn_up_kernel(x_ref, mod_ref, g_ref, wg_ref, wu_ref, o_ref, h_ref):
    @pl.when(pl.program_id(1) == 0)
    def _():
        h_ref[...] = _norm_mod(x_ref[...], g_ref[...], mod_ref[0, 3:4, :], mod_ref[0, 4:5, :]).astype(BF16)

    h = h_ref[...]
    o_ref[...] = (_silu(_dot(h, wg_ref[...])) * _dot(h, wu_ref[...])).astype(BF16)


def _ffn_up(x, mods, mod_row, gain, w_gate_up, *, tm, tn):
    m, d = x.shape
    f = w_gate_up.shape[1] // 2
    nj = f // tn
    blocks = (_nbytes((tm, d), F32) + _nbytes((6, d), F32) + 2 * _nbytes((d, tn), BF16)
              + _nbytes((tm, tn), BF16))
    return pl.pallas_call(
        _ffn_up_kernel,
        grid=(m // tm, nj),
        in_specs=[pl.BlockSpec((tm, d), lambda i, j: (i, 0)),
                  pl.BlockSpec((1, 6, d), lambda i, j: (mod_row(i, tm), 0, 0)),
                  pl.BlockSpec((1, d), lambda i, j: (0, 0)),
                  pl.BlockSpec((d, tn), lambda i, j: (0, j)),
                  pl.BlockSpec((d, tn), lambda i, j: (0, nj + j))],
        out_specs=pl.BlockSpec((tm, tn), lambda i, j: (i, j)),
        out_shape=jax.ShapeDtypeStruct((m, f), BF16),
        scratch_shapes=[pltpu.VMEM((tm, d), BF16)],
        compiler_params=_params(("parallel", "arbitrary"), blocks,
                                _nbytes((tm, d), BF16) + 4 * _nbytes((tm, tn), F32)),
        name="ffn_up",
    )(x, mods, gain, w_gate_up, w_gate_up)


def _final_norm_kernel(x_ref, g_ref, o_ref):
    x = x_ref[...]
    o_ref[...] = x * lax.rsqrt(jnp.mean(x * x, axis=-1, keepdims=True) + EPS) * g_ref[...]


def _final_norm(x, gain, *, tm):
    m, d = x.shape
    return pl.pallas_call(
        _final_norm_kernel,
        grid=(m // tm,),
        in_specs=[pl.BlockSpec((tm, d), lambda i: (i, 0)), pl.BlockSpec((1, d), lambda i: (0, 0))],
        out_specs=pl.BlockSpec((tm, d), lambda i: (i, 0)),
        out_shape=jax.ShapeDtypeStruct((m, d), F32),
        compiler_params=_params(("parallel",), 2 * _nbytes((tm, d), F32), 2 * _nbytes((tm, d), F32)),
        name="final_norm",
    )(x, gain)


def _rope_tables(seq):
    quarter = HEAD_DIM // 4
    pos = jnp.arange(seq)
    row = (pos // GRID_W).astype(F32)
    col = (pos % GRID_W).astype(F32)
    inv_freq = ROPE_BASE ** (-jnp.arange(quarter, dtype=F32) / quarter)
    ang_r, ang_c = row[:, None] * inv_freq[None], col[:, None] * inv_freq[None]
    cos = jnp.concatenate([jnp.cos(ang_r)] * 2 + [jnp.cos(ang_c)] * 2, axis=1)
    sin = jnp.concatenate([-jnp.sin(ang_r), jnp.sin(ang_r), -jnp.sin(ang_c), jnp.sin(ang_c)], axis=1)
    return cos, sin


def kernel(x, c, ctx, c_ctx, w_mod, b_mod, norm1, norm2, w_in, conv_w, attn_sink, a_log, dt_bias, dn_norm,
           w_attn_proj, w_delta_proj, w_out, w_gate_up, w_down, final_norm):
    batch, seq, d = x.shape
    ctx_len = ctx.shape[1]
    depth = w_mod.shape[0]
    n_heads = d // HEAD_DIM
    kv_width = (n_heads // 4) * HEAD_DIM
    qkv_width = d + 2 * kv_width
    assert seq % CHUNK == 0 and ctx_len % CHUNK == 0 and seq % ATTN_BLOCK == 0
    assert 4 * n_heads <= LANES and (batch * seq) % ctx_len == 0

    off_ba = qkv_width + 4 * d
    w_main = jnp.concatenate([w_in[:, :, :off_ba], w_in[:, :, off_ba + 4 * n_heads:]], axis=2).astype(BF16)
    w_ba = jnp.pad(w_in[:, :, off_ba:off_ba + 4 * n_heads], ((0, 0), (0, 0), (0, LANES - 4 * n_heads))).astype(BF16)
    w_attn16, w_delta16, w_out16 = w_attn_proj.astype(BF16), w_delta_proj.astype(BF16), w_out.astype(BF16)
    w_gu16, w_down16 = w_gate_up.astype(BF16), w_down.astype(BF16)

    lane_pad = ((0, 0), (2 * n_heads, LANES - 4 * n_heads))
    alog_vec = jnp.pad(a_log.reshape(depth, 2 * n_heads), lane_pad).reshape(depth, 1, LANES)
    dtb_vec = jnp.pad(dt_bias.reshape(depth, 2 * n_heads), lane_pad).reshape(depth, 1, LANES)

    cos_l, sin_l = _rope_tables(seq)
    cos_l, sin_l = jnp.tile(cos_l, (batch, 1)), jnp.tile(sin_l, (batch, 1))
    cos_c = jnp.ones((batch * ctx_len, LANES), F32)
    sin_c = jnp.zeros((batch * ctx_len, LANES), F32)

    mod_rows = ((batch + 1 + SUBLANES - 1) // SUBLANES) * SUBLANES
    cc = jnp.concatenate([c, c_ctx[None], jnp.zeros((mod_rows - batch - 1, d), F32)], axis=0)
    mods_all = _modulation(cc, w_mod, b_mod).reshape(depth, mod_rows, 6, d)

    tm_l = _pick(seq, (1024, 512, 256, 128))
    tm_c = _pick(batch * ctx_len, (1024, 512, 256, 128))
    tn = _pick(kv_width, (512, 256, 128))
    tn_in = _pick(w_main.shape[2], (1024, 512, 256, 128))

    def row_l(i, tm):
        return (i * tm) // seq

    def row_c(i, tm):
        return batch

    xl = x.reshape(batch * seq, d)
    xc = ctx.reshape(batch * ctx_len, d)
    dims = dict(batch=batch, seq=seq, ctx_len=ctx_len, d_model=d)

    for l in range(depth):
        mods = mods_all[l]
        need_ctx = l < depth - 1
        g1, g2 = norm1[l][None], norm2[l][None]
        proj = functools.partial(_inproj, gain=g1, w_main=w_main[l], w_ba=w_ba[l], tn=tn_in,
                                 qk_width=d + kv_width)
        z_l, ba_l = proj(xl, mods, row_l, cos=cos_l, sin=sin_l, tm=tm_l)
        z_c, ba_c = proj(xc, mods, row_c, cos=cos_c, sin=sin_c, tm=tm_c)

        att_l = _attention(z_l, z_c, attn_sink[l], kv_width=kv_width, **dims)

        dn_l, dn_c = _dn_prep(z_l, z_c, conv_w[l], col0=qkv_width // LANES, **dims)
        gates_l = _gates(ba_l, alog_vec[l], dtb_vec[l], n_heads=n_heads)
        gates_c = _gates(ba_c, alog_vec[l], dtb_vec[l], n_heads=n_heads)
        gates_lt = gates_l.T.reshape(LANES, 1, batch * seq)
        gates_ct = gates_c.T.reshape(LANES, 1, batch * ctx_len)
        do_l, do_c = _delta_scan(dn_l, dn_c, gates_l, gates_c, gates_lt, gates_ct, z_l, z_c,
                                 dn_norm[l][None], zcol=(qkv_width + 3 * d) // LANES, **dims)

        streams = [(xl, att_l, do_l, z_l, row_l, tm_l)]
        if need_ctx:
            att_c = _attention_ctx(z_c, attn_sink[l], batch=batch, ctx_len=ctx_len, d_model=d, kv_width=kv_width)
            streams.append((xc, att_c, do_c, z_c, row_c, tm_c))
        outs = []
        for xs, att, dn, z, row, tm in streams:
            y = _merge(att, dn, z, w_attn16[l], w_delta16[l], gate_col=qkv_width + 4 * d, tm=tm, tn=tn)
            xs = _resid_matmul(y, w_out16[l], xs, mods, row, 2, tm=tm, tn=tn, name="out_proj")
            act = _ffn_up(xs, mods, row, g2, w_gu16[l], tm=tm, tn=tn)
            xs = _resid_matmul(act, w_down16[l], xs, mods, row, 5, tm=tm, tn=tn, name="ffn_down")
            outs.append(xs)
        xl = outs[0]
        if need_ctx:
            xc = outs[1]

    return _final_norm(xl, final_norm[None], tm=tm_l).reshape(batch, seq, d)
```

```python
import functools

import jax
import jax.numpy as jnp
from jax import lax
from jax.experimental import pallas as pl
from jax.experimental.pallas import tpu as pltpu

F32 = jnp.float32
BF16 = jnp.bfloat16

HEAD_DIM = 128
WINDOW = 128
ATTN_BLOCK = 128
GRID_W = 64
CONV_K = 5
ROPE_BASE = 10000.0
EPS = 1e-6
NEG_INF = -1e30
LOG2_E = 1.4426950408889634
CHUNK = 128
LANES = 128
SUBLANES = 8
VMEM_CAP_BYTES = 60 * 1024 * 1024


def _vmem_limit(block_bytes, scratch_bytes=0):
    est = 2 * block_bytes + scratch_bytes
    return int(min(VMEM_CAP_BYTES, max(32 * 1024 * 1024, est * 3 // 2)))


def _params(sem, block_bytes, scratch_bytes=0):
    return pltpu.CompilerParams(dimension_semantics=sem,
                                vmem_limit_bytes=_vmem_limit(block_bytes, scratch_bytes))


def _pick(n, candidates):
    for c in candidates:
        if n % c == 0:
            return c
    raise ValueError(f"no tile in {candidates} divides {n}")


def _nbytes(shape, dtype):
    n = 1
    for s in shape:
        n *= s
    return n * jnp.dtype(dtype).itemsize


def _sigmoid(x):
    return 1.0 / (1.0 + jnp.exp(-x))


def _silu(x):
    return x * _sigmoid(x)


def _dot(a, b):
    return jnp.dot(a, b, preferred_element_type=F32)


def _dot_nt(a, b):
    return lax.dot_general(a, b, (((1,), (1,)), ((), ())), preferred_element_type=F32)


def _split2(x):
    hi = x.astype(BF16)
    lo = (x - hi.astype(F32)).astype(BF16)
    return hi, lo


def _split3(x):
    hi = x.astype(BF16)
    r = x - hi.astype(F32)
    mid = r.astype(BF16)
    lo = (r - mid.astype(F32)).astype(BF16)
    return hi, mid, lo


def _mod_kernel(c_ref, w_ref, b_ref, o_ref):
    s = _silu(c_ref[...])
    s_hi, s_lo = _split2(s)
    w = w_ref[0]
    w_hi, w_lo = _split2(w)
    acc = _dot(s_hi, w_hi) + _dot(s_hi, w_lo) + _dot(s_lo, w_hi)
    o_ref[0] = acc + b_ref[0]


def _modulation(cc, w_mod, b_mod):
    depth, d, n = w_mod.shape
    rows = cc.shape[0]
    tn = _pick(n, (512, 256, 128))
    blocks = _nbytes((rows, d), F32) + _nbytes((d, tn), F32) + _nbytes((1, tn), F32) + _nbytes((rows, tn), F32)
    return pl.pallas_call(
        _mod_kernel,
        grid=(depth, n // tn),
        in_specs=[pl.BlockSpec((rows, d), lambda l, j: (0, 0)),
                  pl.BlockSpec((1, d, tn), lambda l, j: (l, 0, j)),
                  pl.BlockSpec((1, 1, tn), lambda l, j: (l, 0, j))],
        out_specs=pl.BlockSpec((1, rows, tn), lambda l, j: (l, 0, j)),
        out_shape=jax.ShapeDtypeStruct((depth, rows, n), F32),
        compiler_params=_params(("parallel", "parallel"), blocks, 3 * _nbytes((d, tn), F32)),
        name="modulation",
    )(cc, w_mod, b_mod.reshape(depth, 1, n))


def _norm_mod(x, gain, shift, scale):
    y = x * lax.rsqrt(jnp.mean(x * x, axis=-1, keepdims=True) + EPS) * gain
    return y * (1.0 + scale) + shift


def _inproj_kernel(x_ref, mod_ref, g_ref, w_ref, wba_ref, cos_ref, sin_ref, z_ref, ba_ref, h_ref, *, qk_width):
    j = pl.program_id(1)

    @pl.when(j == 0)
    def _():
        h = _norm_mod(x_ref[...], g_ref[...], mod_ref[0, 0:1, :], mod_ref[0, 1:2, :]).astype(BF16)
        h_ref[...] = h
        ba_ref[...] = _dot(h, wba_ref[...])

    acc = _dot(h_ref[...], w_ref[...])
    tn = acc.shape[1]
    full, rem = divmod(qk_width, tn)

    def rope(t):
        width = t.shape[1]
        cos = jnp.concatenate([cos_ref[...]] * (width // HEAD_DIM), axis=1)
        sin = jnp.concatenate([sin_ref[...]] * (width // HEAD_DIM), axis=1)
        partner = jnp.concatenate([pltpu.roll(t[:, g * HEAD_DIM:(g + 1) * HEAD_DIM], HEAD_DIM // 2, 1)
                                   for g in range(width // HEAD_DIM)], axis=1)
        return t * cos + partner * sin

    @pl.when(j < full)
    def _():
        z_ref[...] = rope(acc).astype(BF16)

    if rem:
        @pl.when(j == full)
        def _():
            z_ref[...] = jnp.concatenate([rope(acc[:, :rem]), acc[:, rem:]], axis=1).astype(BF16)

    @pl.when(j >= full + (1 if rem else 0))
    def _():
        z_ref[...] = acc.astype(BF16)


def _inproj(x, mods, mod_row, gain, w_main, w_ba, layer, cos, sin, *, tm, tn, qk_width):
    m, d = x.shape
    n = w_main.shape[2]
    blocks = (_nbytes((tm, d), F32) + _nbytes((6, d), F32) + _nbytes((d, tn), BF16) + _nbytes((d, LANES), BF16)
              + 2 * _nbytes((tm, LANES), F32) + _nbytes((tm, tn), BF16) + _nbytes((tm, LANES), F32))
    return pl.pallas_call(
        functools.partial(_inproj_kernel, qk_width=qk_width),
        grid=(m // tm, n // tn),
        in_specs=[pl.BlockSpec((tm, d), lambda i, j: (i, 0)),
                  pl.BlockSpec((1, 6, d), lambda i, j: (mod_row(i, tm), 0, 0)),
                  pl.BlockSpec((1, d), lambda i, j: (0, 0)),
                  pl.BlockSpec((None, d, tn), lambda i, j: (layer, 0, j)),
                  pl.BlockSpec((None, d, LANES), lambda i, j: (layer, 0, 0)),
                  pl.BlockSpec((tm, LANES), lambda i, j: (i, 0)),
                  pl.BlockSpec((tm, LANES), lambda i, j: (i, 0))],
        out_specs=[pl.BlockSpec((tm, tn), lambda i, j: (i, j)),
                   pl.BlockSpec((tm, LANES), lambda i, j: (i, 0))],
        out_shape=[jax.ShapeDtypeStruct((m, n), BF16),
                   jax.ShapeDtypeStruct((m, LANES), F32)],
        scratch_shapes=[pltpu.VMEM((tm, d), BF16)],
        compiler_params=_params(("parallel", "arbitrary"), blocks,
                                _nbytes((tm, d), BF16) + 3 * _nbytes((tm, tn), F32)),
        name="inproj",
    )(x, mods, gain, w_main, w_ba, cos, sin)


def _softmax_pv(s, sink_col, v):
    m = jnp.maximum(jnp.max(s, axis=-1, keepdims=True), sink_col)
    p = jnp.exp(s - m)
    denom = jnp.sum(p, axis=-1, keepdims=True) + jnp.exp(sink_col - m)
    return _dot((p * (1.0 / denom)).astype(BF16), v)


def _stack_heads(q, groups):
    return jnp.concatenate([q[:, g * HEAD_DIM:(g + 1) * HEAD_DIM] for g in range(groups)], axis=0)


def _unstack_heads(o, groups, rows):
    return jnp.concatenate([o[g * rows:(g + 1) * rows] for g in range(groups)], axis=1)


def _sink_col(sink_ref, kvh, groups, rows):
    return jnp.concatenate([jnp.full((rows, 1), sink_ref[kvh * groups + g], F32) for g in range(groups)], axis=0)


def _attn_kernel(sink_ref, q_ref, k0, k1, k2, k3, v0, v1, v2, v3, kx_ref, vx_ref, o_ref, *, npairs, groups):
    kvh, n = pl.program_id(1), pl.program_id(2)
    bl = ATTN_BLOCK
    k = jnp.concatenate([k0[...], k1[...], k2[...], k3[...], kx_ref[...]], axis=0)
    v = jnp.concatenate([v0[...], v1[...], v2[...], v3[...], vx_ref[...]], axis=0)
    row = lax.broadcasted_iota(jnp.int32, (2 * bl, 3 * bl), 0)
    col = lax.broadcasted_iota(jnp.int32, (2 * bl, 3 * bl), 1)
    sub = row // bl
    valid = ((jnp.abs(col - bl - (row % bl)) <= WINDOW)
             & ((col >= bl) | (sub == 1) | (n > 0)) & ((col < 2 * bl) | (sub == 0) | (n < npairs - 1)))

    v_ones = jnp.concatenate([v, jnp.ones(v.shape, BF16)], axis=1)

    def scores(g):
        s = _dot_nt(q_ref[:, g * HEAD_DIM:(g + 1) * HEAD_DIM], k) * (HEAD_DIM ** -0.5 * LOG2_E)
        local = jnp.concatenate([s[:bl, :3 * bl], s[bl:, bl:4 * bl]], axis=0)
        return jnp.concatenate([jnp.where(valid, local, NEG_INF), s[:, 4 * bl:]], axis=1)

    def attend(s, sink):
        sink = sink * LOG2_E
        m = jnp.maximum(jnp.max(s, axis=-1, keepdims=True), sink)
        p = jnp.exp2(s - m).astype(BF16)
        zeros = jnp.zeros((bl, bl), BF16)
        p = jnp.concatenate([jnp.concatenate([p[:bl, :3 * bl], zeros, p[:bl, 3 * bl:]], axis=1),
                             jnp.concatenate([zeros, p[bl:, :3 * bl], p[bl:, 3 * bl:]], axis=1)], axis=0)
        pv = _dot(p, v_ones)
        denom = pv[:, HEAD_DIM:] + jnp.exp2(sink - m)
        return pv[:, :HEAD_DIM] * (1.0 / denom)

    pending = scores(0)
    for g in range(groups):
        s = pending
        if g + 1 < groups:
            pending = scores(g + 1)
        o_ref[:, g * HEAD_DIM:(g + 1) * HEAD_DIM] = attend(s, sink_ref[kvh * groups + g]).astype(BF16)


def _attention(z_l, z_c, sink, *, batch, seq, ctx_len, d_model, kv_width):
    bl = ATTN_BLOCK
    nb = seq // bl
    assert nb % 2 == 0
    npairs = nb // 2
    n_kv = kv_width // HEAD_DIM
    groups = d_model // kv_width
    gw = groups * HEAD_DIM
    kcol, vcol = d_model // HEAD_DIM, (d_model + kv_width) // HEAD_DIM

    def kv_spec(col0, shift):
        def imap(b, h, n):
            return (b * nb + jnp.clip(2 * n + shift, 0, nb - 1), col0 + h)
        return pl.BlockSpec((bl, HEAD_DIM), imap)

    ctx_k = pl.BlockSpec((ctx_len, HEAD_DIM), lambda b, h, n: (b, kcol + h))
    ctx_v = pl.BlockSpec((ctx_len, HEAD_DIM), lambda b, h, n: (b, vcol + h))
    blocks = (2 * _nbytes((2 * bl, gw), BF16) + 8 * _nbytes((bl, HEAD_DIM), BF16)
              + 2 * _nbytes((ctx_len, HEAD_DIM), BF16))
    temps = 8 * _nbytes((2 * bl, 4 * bl + ctx_len), F32)
    return pl.pallas_call(
        functools.partial(_attn_kernel, npairs=npairs, groups=groups),
        grid=(batch, n_kv, npairs),
        in_specs=[pl.BlockSpec(memory_space=pltpu.SMEM),
                  pl.BlockSpec((2 * bl, gw), lambda b, h, n: (b * npairs + n, h))]
                 + [kv_spec(kcol, shift) for shift in (-1, 0, 1, 2)]
                 + [kv_spec(vcol, shift) for shift in (-1, 0, 1, 2)]
                 + [ctx_k, ctx_v],
        out_specs=pl.BlockSpec((2 * bl, gw), lambda b, h, n: (b * npairs + n, h)),
        out_shape=jax.ShapeDtypeStruct((batch * seq, d_model), BF16),
        compiler_params=_params(("parallel", "parallel", "arbitrary"), blocks, temps),
        name="window_attention",
    )(sink, *([z_l] * 9), z_c, z_c)


def _attn_ctx_kernel(sink_ref, q_ref, kx_ref, vx_ref, o_ref, *, groups):
    kvh = pl.program_id(1)
    rows = q_ref.shape[0]
    q = _stack_heads(q_ref[...], groups)
    s = _dot_nt(q, kx_ref[...]) * (HEAD_DIM ** -0.5)
    o = _softmax_pv(s, _sink_col(sink_ref, kvh, groups, rows), vx_ref[...])
    o_ref[...] = _unstack_heads(o, groups, rows).astype(BF16)


def _attention_ctx(qkv_c, sink, *, batch, ctx_len, d_model, kv_width):
    n_kv = kv_width // HEAD_DIM
    groups = d_model // kv_width
    gw = groups * HEAD_DIM
    kcol, vcol = d_model // HEAD_DIM, (d_model + kv_width) // HEAD_DIM
    blocks = 2 * _nbytes((ctx_len, gw), BF16) + 2 * _nbytes((ctx_len, HEAD_DIM), BF16)
    temps = 6 * _nbytes((groups * ctx_len, ctx_len), F32)
    return pl.pallas_call(
        functools.partial(_attn_ctx_kernel, groups=groups),
        grid=(batch, n_kv),
        in_specs=[pl.BlockSpec(memory_space=pltpu.SMEM),
                  pl.BlockSpec((ctx_len, gw), lambda b, h: (b, h)),
                  pl.BlockSpec((ctx_len, HEAD_DIM), lambda b, h: (b, kcol + h)),
                  pl.BlockSpec((ctx_len, HEAD_DIM), lambda b, h: (b, vcol + h))],
        out_specs=pl.BlockSpec((ctx_len, gw), lambda b, h: (b, h)),
        out_shape=jax.ShapeDtypeStruct((batch * ctx_len, d_model), BF16),
        compiler_params=_params(("parallel", "parallel"), blocks, temps),
        name="context_attention",
    )(sink, qkv_c, qkv_c, qkv_c)


CONV_PAD = SUBLANES
CONV_ROWS = 256


def _conv_seq(x_ref, w, o_ref, pad_ref, l2_flag, l2_scale):
    s = x_ref.shape[0]
    zeros = jnp.zeros((CONV_PAD, LANES), F32)
    pad_ref[pl.ds(0, CONV_PAD), :] = zeros
    pad_ref[pl.ds(CONV_PAD, s), :] = x_ref[...].astype(F32)
    pad_ref[pl.ds(CONV_PAD + s, CONV_PAD), :] = zeros
    rows = min(CONV_ROWS, s)

    def conv_silu(r0):
        y = jnp.zeros((rows, LANES), F32)
        for k in range(CONV_K):
            off = CONV_PAD - CONV_K // 2 + k
            y = y + pad_ref[pl.ds(r0 + off, rows), :] * w[k:k + 1, :]
        return _silu(y)

    def body_l2(it, carry):
        r0 = pl.multiple_of(it * rows, rows)
        y = conv_silu(r0)
        inv = lax.rsqrt(jnp.sum(y * y, axis=-1, keepdims=True) + EPS) * l2_scale
        o_ref[pl.ds(r0, rows), :] = (y * inv).astype(BF16)
        return carry

    def body_plain(it, carry):
        r0 = pl.multiple_of(it * rows, rows)
        o_ref[pl.ds(r0, rows), :] = conv_silu(r0).astype(BF16)
        return carry

    trips = s // rows
    unroll = 2 if trips % 2 == 0 else 1

    @pl.when(l2_flag)
    def _():
        lax.fori_loop(0, trips, body_l2, 0, unroll=unroll)

    @pl.when(jnp.logical_not(l2_flag))
    def _():
        lax.fori_loop(0, trips, body_plain, 0, unroll=unroll)


def _dnprep_kernel(xl_ref, xc_ref, w_ref, ol_ref, oc_ref, padl_ref, padc_ref, *, n_q):
    col = pl.program_id(1)
    l2_flag = col < 2 * n_q
    l2_scale = jnp.where(col < n_q, HEAD_DIM ** -0.5, 1.0)
    w = w_ref[...]
    _conv_seq(xl_ref, w, ol_ref, padl_ref, l2_flag, l2_scale)
    _conv_seq(xc_ref, w, oc_ref, padc_ref, l2_flag, l2_scale)


def _dn_prep(z_l, z_c, conv_w, *, col0, batch, seq, ctx_len, d_model):
    ncol = 3 * d_model // LANES
    blocks = 2 * (_nbytes((seq, LANES), BF16) + _nbytes((ctx_len, LANES), BF16)) + _nbytes((8, LANES), F32)
    scratch = _nbytes((seq + 2 * CONV_PAD, LANES), F32) + _nbytes((ctx_len + 2 * CONV_PAD, LANES), F32)
    return pl.pallas_call(
        functools.partial(_dnprep_kernel, n_q=d_model // LANES),
        grid=(batch, ncol),
        in_specs=[pl.BlockSpec((seq, LANES), lambda b, c: (b, col0 + c)),
                  pl.BlockSpec((ctx_len, LANES), lambda b, c: (b, col0 + c)),
                  pl.BlockSpec((CONV_K, LANES), lambda b, c: (0, c))],
        out_specs=[pl.BlockSpec((seq, LANES), lambda b, c: (b, c)),
                   pl.BlockSpec((ctx_len, LANES), lambda b, c: (b, c))],
        out_shape=[jax.ShapeDtypeStruct((batch * seq, 3 * d_model), BF16),
                   jax.ShapeDtypeStruct((batch * ctx_len, 3 * d_model), BF16)],
        scratch_shapes=[pltpu.VMEM((seq + 2 * CONV_PAD, LANES), F32),
                        pltpu.VMEM((ctx_len + 2 * CONV_PAD, LANES), F32)],
        compiler_params=_params(("parallel", "parallel"), blocks, scratch + 16 * _nbytes((CONV_ROWS, LANES), F32)),
        name="deltanet_prep",
    )(z_l, z_c, conv_w)


def _gates_kernel(ba_ref, alog_ref, dtb_ref, o_ref, *, n_heads):
    tm = ba_ref.shape[0]
    x = ba_ref[...]
    lane = lax.broadcasted_iota(jnp.int32, (CHUNK, LANES), 1)
    z = x + dtb_ref[...]
    softplus = jnp.maximum(z, 0.0) + jnp.log(1.0 + jnp.exp(-jnp.abs(z)))
    g = -jnp.exp(alog_ref[...]) * softplus
    beta = _sigmoid(x)
    r = lax.broadcasted_iota(jnp.int32, (CHUNK, CHUNK), 0)
    c = lax.broadcasted_iota(jnp.int32, (CHUNK, CHUNK), 1)
    tri_prefix = (c <= r).astype(BF16)
    tri_suffix = (c >= r).astype(BF16)
    for i in range(tm // CHUNK):
        rows = slice(i * CHUNK, (i + 1) * CHUNK)
        parts = _split3(g[rows])
        pre = sum(_dot(tri_prefix, p) for p in parts)
        suf = sum(_dot(tri_suffix, p) for p in parts)
        csum = jnp.where(lane < 3 * n_heads, pre, suf)
        o_ref[rows, :] = jnp.where(lane < 2 * n_heads, beta[rows], csum)


def _gates(ba, alog_vec, dtb_vec, *, n_heads):
    m = ba.shape[0]
    tm = _pick(m, (1024, 512, 256, 128))
    blocks = 2 * _nbytes((tm, LANES), F32)
    return pl.pallas_call(
        functools.partial(_gates_kernel, n_heads=n_heads),
        grid=(m // tm,),
        in_specs=[pl.BlockSpec((tm, LANES), lambda i: (i, 0)),
                  pl.BlockSpec((1, LANES), lambda i: (0, 0)),
                  pl.BlockSpec((1, LANES), lambda i: (0, 0))],
        out_specs=pl.BlockSpec((tm, LANES), lambda i: (i, 0)),
        out_shape=jax.ShapeDtypeStruct((m, LANES), F32),
        compiler_params=_params(("parallel",), blocks, 8 * _nbytes((tm, LANES), F32)),
        name="deltanet_gates",
    )(ba, alog_vec, dtb_vec)


def _dot_split(a, b):
    n = b[0].shape[1]
    lhs = jnp.concatenate(a, axis=1)
    half = jnp.concatenate(b, axis=1)
    out = _dot(lhs, jnp.concatenate([half, half], axis=0))
    return out[:, :n] + out[:, n:]


def _unit_tri_inverse_many(mats, hooks=()):
    n = mats[0].shape[0]
    row = lax.broadcasted_iota(jnp.int32, (n, n), 0)
    col = lax.broadcasted_iota(jnp.int32, (n, n), 1)
    splits = [_split2(a) for a in mats]
    eye = jnp.where(row == col, 1.0, 0.0)
    pair = (row // 2) == (col // 2)
    xs = [eye - jnp.where(pair, a, 0.0) for a in mats]
    s, level = 2, 0
    while s < n:
        couple = ((row // (2 * s)) == (col // (2 * s))) & ((row // s) != (col // s))
        zero16 = jnp.zeros((n, n), BF16)
        es = [(jnp.where(couple, hi, zero16), jnp.where(couple, lo, zero16)) for hi, lo in splits]
        xsp = [_split2(x) for x in xs]
        exs = [_split2(_dot_split(e, xp)) for e, xp in zip(es, xsp)]
        xs = [x - _dot_split(xp, ex) for x, xp, ex in zip(xs, xsp, exs)]
        if level < len(hooks):
            hooks[level]()
        s *= 2
        level += 1
    return xs


def _scan_prep_group(refs, chunks, dsts, sel, scr, hooks=()):
    q_ref, k_ref, v_ref, p_ref, crow_refs = refs
    u_s, w_s, qk_s, qd_s, kdt_s, gl_s = scr
    c = CHUNK
    group = len(chunks)
    hooks = list(hooks)
    if hooks:
        hooks.pop(0)()
    row = lax.broadcasted_iota(jnp.int32, (c, c), 0)
    col = lax.broadcasted_iota(jnp.int32, (c, c), 1)
    eye16 = (row == col).astype(BF16)
    r0s = [pl.multiple_of(ci * c, c) for ci in chunks]
    q16s = [q_ref[pl.ds(r0, c), :] for r0 in r0s]
    k16s = [k_ref[pl.ds(r0, c), :] for r0 in r0s]
    qs = [q.astype(F32) for q in q16s]
    ks = [k.astype(F32) for k in k16s]
    vs = [v_ref[pl.ds(r0, c), :].astype(F32) for r0 in r0s]
    sel2 = jnp.concatenate([sel, sel], axis=0)
    parts = [_split3(p_ref[pl.ds(r0, c), :]) for r0 in r0s]
    gates = [_dot(jnp.concatenate([hi, mid], axis=1), sel2) + _dot(lo, sel) for hi, mid, lo in parts]
    kks = [_dot_nt(k16, k16) for k16 in k16s]
    qks = [_dot_nt(q16, k16) for q16, k16 in zip(q16s, k16s)]
    probs = [(g, d) for g in range(group) for d in range(2)]
    betas = [gates[g][:, d * LANES:(d + 1) * LANES] for g, d in probs]
    ccols = [gates[g][:, (2 + d) * LANES:(3 + d) * LANES] for g, d in probs]
    decays, mats = [], []
    for (g, d), beta, ccol in zip(probs, betas, ccols):
        crow = crow_refs[d][0, :, pl.ds(r0s[g], c)]
        incl = (row >= col) if d == 0 else (row <= col)
        strict = (row > col) if d == 0 else (row < col)
        decay = jnp.exp(jnp.where(incl, ccol - crow, NEG_INF))
        decays.append(decay)
        mats.append(jnp.where(strict, beta * kks[g] * decay, 0.0))
    if hooks:
        hooks.pop(0)()
    xsp = [_split2(x) for x in _unit_tri_inverse_many(mats, hooks)]
    expcs = [jnp.exp(ccol) for ccol in ccols]
    rus = [_split2(vs[g] * beta) for (g, d), beta in zip(probs, betas)]
    us = [_dot_split(xp, ru) for xp, ru in zip(xsp, rus)]
    rws = [(ks[g] * beta * expc).astype(BF16) for (g, d), beta, expc in zip(probs, betas, expcs)]
    ws = [_dot(jnp.concatenate(xp, axis=1), jnp.concatenate([rw, rw], axis=0)) for xp, rw in zip(xsp, rws)]
    clasts = [ccol[c - 1:c, :] if d == 0 else ccol[0:1, :] for (g, d), ccol in zip(probs, ccols)]
    kdts = [_dot_nt(eye16, (ks[g] * jnp.exp(clast - ccol)).astype(BF16))
            for (g, d), clast, ccol in zip(probs, clasts, ccols)]
    for i, (g, d) in enumerate(probs):
        rows = pl.ds(pl.multiple_of(dsts[g] * c, c), c)
        u_s[d, rows, :] = us[i]
        w_s[d, rows, :] = ws[i].astype(BF16)
        qk_s[d, rows, :] = (qks[g] * decays[i]).astype(BF16)
        qd_s[d, rows, :] = (qs[g] * expcs[i]).astype(BF16)
        kdt_s[d, rows, :] = kdts[i].astype(BF16)
        gl_s[d, pl.ds(pl.multiple_of(dsts[g] * SUBLANES, SUBLANES), SUBLANES), :] = jnp.broadcast_to(
            jnp.exp(clasts[i]), (SUBLANES, LANES))


def _step_issue(chunks, states, scr):
    u_s, w_s, qk_s, qd_s, kdt_s, gl_s = scr
    c = CHUNK
    rows = [pl.ds(pl.multiple_of(ci * c, c), c) for ci in chunks]
    dirs = range(2)
    ws = [_dot(jnp.concatenate([w_s[d, rows[d], :], qd_s[d, rows[d], :]], axis=0), states[d].astype(BF16))
          for d in dirs]
    return dict(chunks=chunks, rows=rows, states=states, ws=ws)


def _step_finish(pending, scr):
    u_s, w_s, qk_s, qd_s, kdt_s, gl_s = scr
    c = CHUNK
    chunks, rows, states, ws = pending["chunks"], pending["rows"], pending["states"], pending["ws"]
    dirs = range(2)
    v_new = [(u_s[d, rows[d], :] - ws[d][:c]).astype(BF16) for d in dirs]
    outs = [ws[d][c:] + _dot(qk_s[d, rows[d], :], v_new[d]) for d in dirs]
    gls = [gl_s[d, pl.ds(pl.multiple_of(chunks[d] * SUBLANES, SUBLANES), SUBLANES), :][0:1, :] for d in dirs]
    new_states = [states[d] * gls[d] + _dot(kdt_s[d, rows[d], :], v_new[d]) for d in dirs]
    return tuple(new_states), outs


def _scan_step_pair(chunks, states, scr):
    return _step_finish(_step_issue(chunks, states, scr), scr)


PREP_GROUP = 4


def _scan_kernel(ql, kl, vl, pl_l, crfl, crbl, zl_prev, qx, kx, vx, px, crfx, crbx, zx, gain_ref,
                 outl_prev_ref, outx_ref, u_s, w_s, qk_s, qd_s, kdt_s, gl_s, o_s, st_s, *, n_heads):
    b, j = pl.program_id(0), pl.program_id(1)
    c = CHUNK
    ncx, ncl = qx.shape[0] // c, ql.shape[0] // c
    half = PREP_GROUP // 2
    n_groups = ncl // PREP_GROUP
    assert PREP_GROUP == 4 and ncx % 2 == 0 and ncl % PREP_GROUP == 0 and ncx <= PREP_GROUP
    cur = j % 2
    factor_refs = (u_s, w_s, qk_s, qd_s, kdt_s, gl_s)
    scr_cur, o_cur = tuple(r.at[cur] for r in factor_refs), o_s.at[cur]
    scr_prev, o_prev = tuple(r.at[1 - cur] for r in factor_refs), o_s.at[1 - cur]
    gain = gain_ref[...]

    def chunk_rows(i):
        return pl.ds(pl.multiple_of(i * c, c), c)

    @pl.when((b == 0) & (j == 0))
    def _():
        def body(i, carry):
            for r in factor_refs[:5] + (o_s,):
                for d in range(2):
                    r[1, d, chunk_rows(i), :] = jnp.zeros((c, LANES), r.dtype)
            return carry
        lax.fori_loop(0, ncx + ncl, body, 0)
        gl_s[1] = jnp.zeros(gl_s.shape[1:], F32)
        st_s[...] = jnp.zeros(st_s.shape, F32)

    def finish(z_ref, out_ref, i, o):
        on = o * lax.rsqrt(jnp.mean(o * o, axis=-1, keepdims=True) + EPS) * gain
        out_ref[chunk_rows(i), :] = (on * _silu(z_ref[chunk_rows(i), :].astype(F32))).astype(BF16)

    def step_stages(scr, o_ref, base, n, it, box, key, z_ref=None, out_ref=None):
        pending = {}
        jf, jb = it, n - 1 - it

        def issue():
            pending.update(_step_issue((base + jf, base + jb), box[key], scr))

        def complete():
            box[key], outs = _step_finish(pending, scr)
            if z_ref is None:
                o_ref[0, chunk_rows(base + jf), :] = outs[0]
                o_ref[1, chunk_rows(base + jb), :] = outs[1]
            else:
                finish(z_ref, out_ref, jf, outs[0] + o_ref[1, chunk_rows(base + jf), :])
                finish(z_ref, out_ref, jb, outs[1] + o_ref[0, chunk_rows(base + jb), :])
        return [issue, complete]

    def run(stages):
        for stage in stages:
            stage()

    def front(i):
        return [half * i + k for k in range(half)]

    def drain_stages(p, box):
        if isinstance(p, int) and p < half:
            return step_stages(scr_prev, o_prev, ncx, ncl, front(n_groups - 1)[p], box, "prev")
        return step_stages(scr_prev, o_prev, ncx, ncl, ncl // 2 + p - half, box, "prev", zl_prev, outl_prev_ref)

    n_drain = half + ncl // 2

    @pl.when(j < n_heads)
    def _():
        r = lax.broadcasted_iota(jnp.int32, (LANES, 4 * LANES), 0)
        t = lax.broadcasted_iota(jnp.int32, (LANES, 4 * LANES), 1) // LANES
        sel = (r == t * n_heads + j).astype(BF16)
        zero = jnp.zeros((HEAD_DIM, HEAD_DIM), F32)
        box = {"cur": (zero, zero), "prev": (st_s[0], st_s[1])}

        _scan_prep_group((qx, kx, vx, px, (crfx, crbx)), list(range(ncx)), list(range(ncx)), sel, scr_cur)
        for it in range(ncx):
            ends = (zx, outx_ref) if it >= ncx // 2 else ()
            run(step_stages(scr_cur, o_cur, 0, ncx, it, box, "cur", *ends))

        def prep_latent(i, hooks):
            chunks = front(i) + [ncl - 1 - ci for ci in reversed(front(i))]
            _scan_prep_group((ql, kl, vl, pl_l, (crfl, crbl)), chunks, [ncx + ci for ci in chunks], sel,
                             scr_cur, hooks)

        prep_latent(0, [stage for p in range(2 * half) for stage in drain_stages(p, box)])

        def overlapped(i, carry):
            box = {"cur": carry[:2], "prev": carry[2:]}
            hooks = []
            for k, it in enumerate(front(i - 1)):
                own = step_stages(scr_cur, o_cur, ncx, ncl, it, box, "cur")
                other = drain_stages(2 * half + half * (i - 1) + k, box)
                hooks += [own[0], other[0], own[1], other[1]]
            prep_latent(i, hooks)
            return tuple(box["cur"]) + tuple(box["prev"])

        carry = lax.fori_loop(1, n_groups, overlapped, tuple(box["cur"]) + tuple(box["prev"]))
        st_s[0] = carry[0]
        st_s[1] = carry[1]

    @pl.when(j == n_heads)
    def _():
        box = {"prev": (st_s[0], st_s[1])}
        for p in range(half):
            run(drain_stages(p, box))

        def body(p, states):
            inner = {"prev": states}
            run(drain_stages(p, inner))
            return tuple(inner["prev"])
        lax.fori_loop(half, n_drain, body, tuple(box["prev"]))


def _delta_scan(dn_l, dn_c, gates_l, gates_c, gates_lt, gates_ct, z_l, z_c, gain,
                *, zcol, batch, seq, ctx_len, d_model):
    n_heads = d_model // HEAD_DIM
    t = seq + ctx_len
    last = n_heads - 1

    def col_spec(rows, col0, lag=0):
        return pl.BlockSpec((rows, LANES), lambda b, j: (b, col0 + jnp.clip(j - lag, 0, last)))

    def gate_spec(rows):
        return pl.BlockSpec((rows, LANES), lambda b, j: (b, 0))

    def crow_spec(rows, d):
        return pl.BlockSpec((1, 1, rows), lambda b, j: ((2 + d) * n_heads + jnp.minimum(j, last), 0, b))

    def side(rows, z_lag):
        return [col_spec(rows, 0), col_spec(rows, n_heads), col_spec(rows, 2 * n_heads), gate_spec(rows),
                crow_spec(rows, 0), crow_spec(rows, 1), col_spec(rows, zcol, z_lag)]

    blocks = (_nbytes((t, LANES), F32) + 2 * _nbytes((SUBLANES, t), F32) + 5 * _nbytes((t, LANES), BF16))
    slots = 2
    scratch_shapes = [pltpu.VMEM((slots, 2, t, LANES), F32),
                      pltpu.VMEM((slots, 2, t, LANES), BF16),
                      pltpu.VMEM((slots, 2, t, LANES), BF16),
                      pltpu.VMEM((slots, 2, t, LANES), BF16),
                      pltpu.VMEM((slots, 2, t, LANES), BF16),
                      pltpu.VMEM((slots, 2, (t // CHUNK) * SUBLANES, LANES), F32),
                      pltpu.VMEM((slots, 2, t, LANES), F32),
                      pltpu.VMEM((2, HEAD_DIM, HEAD_DIM), F32)]
    scratch = slots * (2 * 2 * _nbytes((t, LANES), F32) + 4 * 2 * _nbytes((t, LANES), BF16))
    return pl.pallas_call(
        functools.partial(_scan_kernel, n_heads=n_heads),
        grid=(batch, n_heads + 1),
        in_specs=side(seq, 1) + side(ctx_len, 0) + [pl.BlockSpec((1, LANES), lambda b, j: (0, 0))],
        out_specs=[pl.BlockSpec((seq, LANES), lambda b, j: (b, jnp.clip(j - 1, 0, last))),
                   pl.BlockSpec((ctx_len, LANES), lambda b, j: (b, jnp.minimum(j, last)))],
        out_shape=[jax.ShapeDtypeStruct((batch * seq, d_model), BF16),
                   jax.ShapeDtypeStruct((batch * ctx_len, d_model), BF16)],
        scratch_shapes=scratch_shapes,
        compiler_params=_params(("arbitrary", "arbitrary"), blocks, scratch + 4 * 1024 * 1024),
        name="delta_scan",
    )(dn_l, dn_l, dn_l, gates_l, gates_lt, gates_lt, z_l,
      dn_c, dn_c, dn_c, gates_c, gates_ct, gates_ct, z_c, gain)


def _merge_kernel(att_ref, dn_ref, ga_ref, gd_ref, wa_ref, wd_ref, o_ref):
    y = (_sigmoid(ga_ref[...].astype(F32)) * _dot(att_ref[...], wa_ref[...])
         + _sigmoid(gd_ref[...].astype(F32)) * _dot(dn_ref[...], wd_ref[...]))
    o_ref[...] = y.astype(BF16)


def _merge(att, dn, z, w_attn, w_delta, layer, *, gate_col, tm, tn):
    m, d = att.shape
    ga0, gd0 = gate_col // tn, (gate_col + d) // tn
    blocks = (2 * _nbytes((tm, d), BF16) + 2 * _nbytes((tm, tn), BF16) + 2 * _nbytes((d, tn), BF16)
              + _nbytes((tm, tn), BF16))
    return pl.pallas_call(
        _merge_kernel,
        grid=(m // tm, d // tn),
        in_specs=[pl.BlockSpec((tm, d), lambda i, j: (i, 0)),
                  pl.BlockSpec((tm, d), lambda i, j: (i, 0)),
                  pl.BlockSpec((tm, tn), lambda i, j: (i, ga0 + j)),
                  pl.BlockSpec((tm, tn), lambda i, j: (i, gd0 + j)),
                  pl.BlockSpec((None, d, tn), lambda i, j: (layer, 0, j)),
                  pl.BlockSpec((None, d, tn), lambda i, j: (layer, 0, j))],
        out_specs=pl.BlockSpec((tm, tn), lambda i, j: (i, j)),
        out_shape=jax.ShapeDtypeStruct((m, d), BF16),
        compiler_params=_params(("parallel", "arbitrary"), blocks, 4 * _nbytes((tm, tn), F32)),
        name="merge",
    )(att, dn, z, z, w_attn, w_delta)


def _resid_kernel(a_ref, w_ref, x_ref, mod_ref, o_ref, *, gate_row):
    o_ref[...] = x_ref[...] + mod_ref[0, gate_row:gate_row + 1, :] * _dot(a_ref[...], w_ref[...])


def _resid_matmul(a, w, layer, x, mods, mod_row, gate_row, *, tm, tn, name):
    m, k = a.shape
    d = w.shape[2]
    blocks = (_nbytes((tm, k), BF16) + _nbytes((k, tn), BF16) + 2 * _nbytes((tm, tn), F32)
              + _nbytes((6, tn), F32))
    return pl.pallas_call(
        functools.partial(_resid_kernel, gate_row=gate_row),
        grid=(m // tm, d // tn),
        in_specs=[pl.BlockSpec((tm, k), lambda i, j: (i, 0)),
                  pl.BlockSpec((None, k, tn), lambda i, j: (layer, 0, j)),
                  pl.BlockSpec((tm, tn), lambda i, j: (i, j)),
                  pl.BlockSpec((1, 6, tn), lambda i, j: (mod_row(i, tm), 0, j))],
        out_specs=pl.BlockSpec((tm, tn), lambda i, j: (i, j)),
        out_shape=jax.ShapeDtypeStruct((m, d), F32),
        compiler_params=_params(("parallel", "arbitrary"), blocks, 2 * _nbytes((tm, tn), F32)),
        name=name,
    )(a, w, x, mods)


def _ffn_up_kernel(x_ref, mod_ref, g_ref, wg_ref, wu_ref, o_ref, h_ref):
    @pl.when(pl.program_id(1) == 0)
    def _():
        h_ref[...] = _norm_mod(x_ref[...], g_ref[...], mod_ref[0, 3:4, :], mod_ref[0, 4:5, :]).astype(BF16)

    h = h_ref[...]
    o_ref[...] = (_silu(_dot(h, wg_ref[...])) * _dot(h, wu_ref[...])).astype(BF16)


def _ffn_up(x, mods, mod_row, gain, w_gate_up, layer, *, tm, tn):
    m, d = x.shape
    f = w_gate_up.shape[2] // 2
    nj = f // tn
    blocks = (_nbytes((tm, d), F32) + _nbytes((6, d), F32) + 2 * _nbytes((d, tn), BF16)
              + _nbytes((tm, tn), BF16))
    return pl.pallas_call(
        _ffn_up_kernel,
        grid=(m // tm, nj),
        in_specs=[pl.BlockSpec((tm, d), lambda i, j: (i, 0)),
                  pl.BlockSpec((1, 6, d), lambda i, j: (mod_row(i, tm), 0, 0)),
                  pl.BlockSpec((1, d), lambda i, j: (0, 0)),
                  pl.BlockSpec((None, d, tn), lambda i, j: (layer, 0, j)),
                  pl.BlockSpec((None, d, tn), lambda i, j: (layer, 0, nj + j))],
        out_specs=pl.BlockSpec((tm, tn), lambda i, j: (i, j)),
        out_shape=jax.ShapeDtypeStruct((m, f), BF16),
        scratch_shapes=[pltpu.VMEM((tm, d), BF16)],
        compiler_params=_params(("parallel", "arbitrary"), blocks,
                                _nbytes((tm, d), BF16) + 4 * _nbytes((tm, tn), F32)),
        name="ffn_up",
    )(x, mods, gain, w_gate_up, w_gate_up)


def _final_norm_kernel(x_ref, g_ref, o_ref):
    x = x_ref[...]
    o_ref[...] = x * lax.rsqrt(jnp.mean(x * x, axis=-1, keepdims=True) + EPS) * g_ref[...]


def _final_norm(x, gain, *, tm):
    m, d = x.shape
    return pl.pallas_call(
        _final_norm_kernel,
        grid=(m // tm,),
        in_specs=[pl.BlockSpec((tm, d), lambda i: (i, 0)), pl.BlockSpec((1, d), lambda i: (0, 0))],
        out_specs=pl.BlockSpec((tm, d), lambda i: (i, 0)),
        out_shape=jax.ShapeDtypeStruct((m, d), F32),
        compiler_params=_params(("parallel",), 2 * _nbytes((tm, d), F32), 2 * _nbytes((tm, d), F32)),
        name="final_norm",
    )(x, gain)


def _rope_tables(seq):
    quarter = HEAD_DIM // 4
    pos = jnp.arange(seq)
    row = (pos // GRID_W).astype(F32)
    col = (pos % GRID_W).astype(F32)
    inv_freq = ROPE_BASE ** (-jnp.arange(quarter, dtype=F32) / quarter)
    ang_r, ang_c = row[:, None] * inv_freq[None], col[:, None] * inv_freq[None]
    cos = jnp.concatenate([jnp.cos(ang_r), jnp.cos(ang_c)] * 2, axis=1)
    sin = jnp.concatenate([-jnp.sin(ang_r), -jnp.sin(ang_c), jnp.sin(ang_r), jnp.sin(ang_c)], axis=1)
    return cos, sin


def _rope_perm():
    quarter = HEAD_DIM // 4
    order = (0, 2, 1, 3)
    return jnp.concatenate([jnp.arange(quarter) + quarter * o for o in order])


def kernel(x, c, ctx, c_ctx, w_mod, b_mod, norm1, norm2, w_in, conv_w, attn_sink, a_log, dt_bias, dn_norm,
           w_attn_proj, w_delta_proj, w_out, w_gate_up, w_down, final_norm):
    batch, seq, d = x.shape
    ctx_len = ctx.shape[1]
    depth = w_mod.shape[0]
    n_heads = d // HEAD_DIM
    kv_width = (n_heads // 4) * HEAD_DIM
    qkv_width = d + 2 * kv_width
    assert seq % CHUNK == 0 and ctx_len % CHUNK == 0 and seq % ATTN_BLOCK == 0
    assert 4 * n_heads <= LANES and (batch * seq) % ctx_len == 0

    off_ba = qkv_width + 4 * d
    qk_width = d + kv_width
    w_qk = w_in[:, :, :qk_width].reshape(depth, d, qk_width // HEAD_DIM, HEAD_DIM)[..., _rope_perm()]
    w_main = jnp.concatenate([w_qk.reshape(depth, d, qk_width), w_in[:, :, qk_width:off_ba],
                              w_in[:, :, off_ba + 4 * n_heads:]], axis=2).astype(BF16)
    w_ba = jnp.pad(w_in[:, :, off_ba:off_ba + 4 * n_heads], ((0, 0), (0, 0), (0, LANES - 4 * n_heads))).astype(BF16)
    w_attn16, w_delta16, w_out16 = w_attn_proj.astype(BF16), w_delta_proj.astype(BF16), w_out.astype(BF16)
    w_gu16, w_down16 = w_gate_up.astype(BF16), w_down.astype(BF16)

    lane_pad = ((0, 0), (2 * n_heads, LANES - 4 * n_heads))
    alog_vec = jnp.pad(a_log.reshape(depth, 2 * n_heads), lane_pad).reshape(depth, 1, LANES)
    dtb_vec = jnp.pad(dt_bias.reshape(depth, 2 * n_heads), lane_pad).reshape(depth, 1, LANES)

    cos_l, sin_l = _rope_tables(seq)
    cos_l, sin_l = jnp.tile(cos_l, (batch, 1)), jnp.tile(sin_l, (batch, 1))
    cos_c = jnp.ones((batch * ctx_len, LANES), F32)
    sin_c = jnp.zeros((batch * ctx_len, LANES), F32)

    mod_rows = ((batch + 1 + SUBLANES - 1) // SUBLANES) * SUBLANES
    cc = jnp.concatenate([c, c_ctx[None], jnp.zeros((mod_rows - batch - 1, d), F32)], axis=0)
    mods_all = _modulation(cc, w_mod, b_mod).reshape(depth, mod_rows, 6, d)

    tm_l = _pick(seq, (1024, 512, 256, 128))
    tm_c = _pick(batch * ctx_len, (1024, 512, 256, 128))
    tn = _pick(kv_width, (512, 256, 128))
    tn_in = _pick(w_main.shape[2], (1024, 512, 256, 128))

    def row_l(i, tm):
        return (i * tm) // seq

    def row_c(i, tm):
        return batch

    xl = x.reshape(batch * seq, d)
    xc = ctx.reshape(batch * ctx_len, d)
    dims = dict(batch=batch, seq=seq, ctx_len=ctx_len, d_model=d)

    for l in range(depth):
        mods = mods_all[l]
        need_ctx = l < depth - 1
        g1, g2 = norm1[l][None], norm2[l][None]
        proj = functools.partial(_inproj, gain=g1, w_main=w_main, w_ba=w_ba, layer=l, tn=tn_in,
                                 qk_width=d + kv_width)
        z_l, ba_l = proj(xl, mods, row_l, cos=cos_l, sin=sin_l, tm=tm_l)
        z_c, ba_c = proj(xc, mods, row_c, cos=cos_c, sin=sin_c, tm=tm_c)

        att_l = _attention(z_l, z_c, attn_sink[l], kv_width=kv_width, **dims)

        dn_l, dn_c = _dn_prep(z_l, z_c, conv_w[l], col0=qkv_width // LANES, **dims)
        gates_l = _gates(ba_l, alog_vec[l], dtb_vec[l], n_heads=n_heads)
        gates_c = _gates(ba_c, alog_vec[l], dtb_vec[l], n_heads=n_heads)
        gates_lt = gates_l.T.reshape(LANES, 1, batch * seq)
        gates_ct = gates_c.T.reshape(LANES, 1, batch * ctx_len)
        do_l, do_c = _delta_scan(dn_l, dn_c, gates_l, gates_c, gates_lt, gates_ct, z_l, z_c,
                                 dn_norm[l][None], zcol=(qkv_width + 3 * d) // LANES, **dims)

        streams = [(xl, att_l, do_l, z_l, row_l, tm_l)]
        if need_ctx:
            att_c = _attention_ctx(z_c, attn_sink[l], batch=batch, ctx_len=ctx_len, d_model=d, kv_width=kv_width)
            streams.append((xc, att_c, do_c, z_c, row_c, tm_c))
        outs = []
        for xs, att, dn, z, row, tm in streams:
            y = _merge(att, dn, z, w_attn16, w_delta16, l, gate_col=qkv_width + 4 * d, tm=tm, tn=tn)
            xs = _resid_matmul(y, w_out16, l, xs, mods, row, 2, tm=tm, tn=tn, name="out_proj")
            act = _ffn_up(xs, mods, row, g2, w_gu16, l, tm=tm, tn=tn)
            xs = _resid_matmul(act, w_down16, l, xs, mods, row, 5, tm=tm, tn=tn, name="ffn_down")
            outs.append(xs)
        xl = outs[0]
        if need_ctx:
            xc = outs[1]

    return _final_norm(xl, final_norm[None], tm=tm_l).reshape(batch, seq, d)
```

```python
import functools

import jax
import jax.numpy as jnp
from jax import lax
from jax.experimental import pallas as pl
from jax.experimental.pallas import tpu as pltpu

F32 = jnp.float32
BF16 = jnp.bfloat16

HEAD_DIM = 128
WINDOW = 128
ATTN_BLOCK = 128
GRID_W = 64
CONV_K = 5
ROPE_BASE = 10000.0
EPS = 1e-6
NEG_INF = -1e30
LOG2_E = 1.4426950408889634
CHUNK = 128
LANES = 128
SUBLANES = 8
VMEM_CAP_BYTES = 60 * 1024 * 1024


def _vmem_limit(block_bytes, scratch_bytes=0):
    est = 2 * block_bytes + scratch_bytes
    return int(min(VMEM_CAP_BYTES, max(32 * 1024 * 1024, est * 3 // 2)))


def _params(sem, block_bytes, scratch_bytes=0):
    return pltpu.CompilerParams(dimension_semantics=sem,
                                vmem_limit_bytes=_vmem_limit(block_bytes, scratch_bytes))


def _pick(n, candidates):
    for c in candidates:
        if n % c == 0:
            return c
    raise ValueError(f"no tile in {candidates} divides {n}")


def _nbytes(shape, dtype):
    n = 1
    for s in shape:
        n *= s
    return n * jnp.dtype(dtype).itemsize


def _sigmoid(x):
    return 1.0 / (1.0 + jnp.exp(-x))


def _silu(x):
    return x * _sigmoid(x)


def _dot(a, b):
    return jnp.dot(a, b, preferred_element_type=F32)


def _dot_nt(a, b):
    return lax.dot_general(a, b, (((1,), (1,)), ((), ())), preferred_element_type=F32)


def _split2(x):
    hi = x.astype(BF16)
    lo = (x - hi.astype(F32)).astype(BF16)
    return hi, lo


def _split3(x):
    hi = x.astype(BF16)
    r = x - hi.astype(F32)
    mid = r.astype(BF16)
    lo = (r - mid.astype(F32)).astype(BF16)
    return hi, mid, lo


def _mod_kernel(c_ref, w_ref, b_ref, o_ref):
    s = _silu(c_ref[...])
    s_hi, s_lo = _split2(s)
    w = w_ref[0]
    w_hi, w_lo = _split2(w)
    acc = _dot(s_hi, w_hi) + _dot(s_hi, w_lo) + _dot(s_lo, w_hi)
    o_ref[0] = acc + b_ref[0]


def _modulation(cc, w_mod, b_mod):
    depth, d, n = w_mod.shape
    rows = cc.shape[0]
    tn = _pick(n, (512, 256, 128))
    blocks = _nbytes((rows, d), F32) + _nbytes((d, tn), F32) + _nbytes((1, tn), F32) + _nbytes((rows, tn), F32)
    return pl.pallas_call(
        _mod_kernel,
        grid=(depth, n // tn),
        in_specs=[pl.BlockSpec((rows, d), lambda l, j: (0, 0)),
                  pl.BlockSpec((1, d, tn), lambda l, j: (l, 0, j)),
                  pl.BlockSpec((1, 1, tn), lambda l, j: (l, 0, j))],
        out_specs=pl.BlockSpec((1, rows, tn), lambda l, j: (l, 0, j)),
        out_shape=jax.ShapeDtypeStruct((depth, rows, n), F32),
        compiler_params=_params(("parallel", "parallel"), blocks, 3 * _nbytes((d, tn), F32)),
        name="modulation",
    )(cc, w_mod, b_mod.reshape(depth, 1, n))


INPROJ_SUB_ROWS = 256


def _norm_mod(x, gain, shift, scale):
    y = x * lax.rsqrt(jnp.mean(x * x, axis=-1, keepdims=True) + EPS) * gain
    return y * (1.0 + scale) + shift


def _inproj_kernel(x_ref, mod_ref, g_ref, w_ref, wba_ref, cos_ref, sin_ref, z_ref, ba_ref, h_ref, *, qk_width):
    j = pl.program_id(1)
    tm, tn = z_ref.shape
    full, rem = divmod(qk_width, tn)
    sub = min(tm, INPROJ_SUB_ROWS)

    def rope(t, rows):
        width = t.shape[1]
        cos = jnp.concatenate([cos_ref[rows, :]] * (width // HEAD_DIM), axis=1)
        sin = jnp.concatenate([sin_ref[rows, :]] * (width // HEAD_DIM), axis=1)
        partner = jnp.concatenate([pltpu.roll(t[:, g * HEAD_DIM:(g + 1) * HEAD_DIM], HEAD_DIM // 2, 1)
                                   for g in range(width // HEAD_DIM)], axis=1)
        return t * cos + partner * sin

    def tile(epilogue, fresh=False):
        for r in range(tm // sub):
            rows = pl.ds(r * sub, sub)
            if fresh:
                h = _norm_mod(x_ref[rows, :], g_ref[...], mod_ref[0, 0:1, :], mod_ref[0, 1:2, :]).astype(BF16)
                h_ref[rows, :] = h
                ba_ref[rows, :] = _dot(h, wba_ref[...])
            else:
                h = h_ref[rows, :]
            z_ref[rows, :] = epilogue(_dot(h, w_ref[...]), rows).astype(BF16)

    assert full >= 1

    @pl.when(j == 0)
    def _():
        tile(rope, fresh=True)

    @pl.when((j > 0) & (j < full))
    def _():
        tile(rope)

    if rem:
        @pl.when(j == full)
        def _():
            tile(lambda acc, rows: jnp.concatenate([rope(acc[:, :rem], rows), acc[:, rem:]], axis=1))

    @pl.when(j >= full + (1 if rem else 0))
    def _():
        tile(lambda acc, rows: acc)


def _inproj(x, mods, mod_row, gain, w_main, w_ba, layer, cos, sin, *, tm, tn, qk_width):
    m, d = x.shape
    n = w_main.shape[2]
    blocks = (_nbytes((tm, d), F32) + _nbytes((6, d), F32) + _nbytes((d, tn), BF16) + _nbytes((d, LANES), BF16)
              + 2 * _nbytes((tm, LANES), F32) + _nbytes((tm, tn), BF16) + _nbytes((tm, LANES), F32))
    return pl.pallas_call(
        functools.partial(_inproj_kernel, qk_width=qk_width),
        grid=(m // tm, n // tn),
        in_specs=[pl.BlockSpec((tm, d), lambda i, j: (i, 0)),
                  pl.BlockSpec((1, 6, d), lambda i, j: (mod_row(i, tm), 0, 0)),
                  pl.BlockSpec((1, d), lambda i, j: (0, 0)),
                  pl.BlockSpec((None, d, tn), lambda i, j: (layer, 0, j)),
                  pl.BlockSpec((None, d, LANES), lambda i, j: (layer, 0, 0)),
                  pl.BlockSpec((tm, LANES), lambda i, j: (i, 0)),
                  pl.BlockSpec((tm, LANES), lambda i, j: (i, 0))],
        out_specs=[pl.BlockSpec((tm, tn), lambda i, j: (i, j)),
                   pl.BlockSpec((tm, LANES), lambda i, j: (i, 0))],
        out_shape=[jax.ShapeDtypeStruct((m, n), BF16),
                   jax.ShapeDtypeStruct((m, LANES), F32)],
        scratch_shapes=[pltpu.VMEM((tm, d), BF16)],
        compiler_params=_params(("parallel", "arbitrary"), blocks,
                                _nbytes((tm, d), BF16) + 3 * _nbytes((tm, tn), F32)),
        name="inproj",
    )(x, mods, gain, w_main, w_ba, cos, sin)


def _softmax_pv(s, sink_col, v):
    m = jnp.maximum(jnp.max(s, axis=-1, keepdims=True), sink_col)
    p = jnp.exp(s - m)
    denom = jnp.sum(p, axis=-1, keepdims=True) + jnp.exp(sink_col - m)
    return _dot((p * (1.0 / denom)).astype(BF16), v)


def _stack_heads(q, groups):
    return jnp.concatenate([q[:, g * HEAD_DIM:(g + 1) * HEAD_DIM] for g in range(groups)], axis=0)


def _unstack_heads(o, groups, rows):
    return jnp.concatenate([o[g * rows:(g + 1) * rows] for g in range(groups)], axis=1)


def _sink_col(sink_ref, kvh, groups, rows):
    return jnp.concatenate([jnp.full((rows, 1), sink_ref[kvh * groups + g], F32) for g in range(groups)], axis=0)


def _attn_kernel(sink_ref, q_ref, k0, k1, k2, k3, v0, v1, v2, v3, kx_ref, vx_ref, o_ref, *, npairs, groups):
    kvh, n = pl.program_id(1), pl.program_id(2)
    bl = ATTN_BLOCK
    k = jnp.concatenate([k0[...], k1[...], k2[...], k3[...], kx_ref[...]], axis=0)
    v = jnp.concatenate([v0[...], v1[...], v2[...], v3[...], vx_ref[...]], axis=0)
    row = lax.broadcasted_iota(jnp.int32, (2 * bl, 3 * bl), 0)
    col = lax.broadcasted_iota(jnp.int32, (2 * bl, 3 * bl), 1)
    sub = row // bl
    valid = ((jnp.abs(col - bl - (row % bl)) <= WINDOW)
             & ((col >= bl) | (sub == 1) | (n > 0)) & ((col < 2 * bl) | (sub == 0) | (n < npairs - 1)))

    v_ones = jnp.concatenate([v, jnp.ones(v.shape, BF16)], axis=1)

    def scores(g):
        s = _dot_nt(q_ref[:, g * HEAD_DIM:(g + 1) * HEAD_DIM], k) * (HEAD_DIM ** -0.5 * LOG2_E)
        local = jnp.concatenate([s[:bl, :3 * bl], s[bl:, bl:4 * bl]], axis=0)
        return jnp.concatenate([jnp.where(valid, local, NEG_INF), s[:, 4 * bl:]], axis=1)

    def attend(s, sink):
        sink = sink * LOG2_E
        m = jnp.maximum(jnp.max(s, axis=-1, keepdims=True), sink)
        p = jnp.exp2(s - m).astype(BF16)
        zeros = jnp.zeros((bl, bl), BF16)
        p = jnp.concatenate([jnp.concatenate([p[:bl, :3 * bl], zeros, p[:bl, 3 * bl:]], axis=1),
                             jnp.concatenate([zeros, p[bl:, :3 * bl], p[bl:, 3 * bl:]], axis=1)], axis=0)
        pv = _dot(p, v_ones)
        denom = pv[:, HEAD_DIM:] + jnp.exp2(sink - m)
        return pv[:, :HEAD_DIM] * (1.0 / denom)

    pending = scores(0)
    for g in range(groups):
        s = pending
        if g + 1 < groups:
            pending = scores(g + 1)
        o_ref[:, g * HEAD_DIM:(g + 1) * HEAD_DIM] = attend(s, sink_ref[kvh * groups + g]).astype(BF16)


def _attention(z_l, z_c, sink, *, batch, seq, ctx_len, d_model, kv_width):
    bl = ATTN_BLOCK
    nb = seq // bl
    assert nb % 2 == 0
    npairs = nb // 2
    n_kv = kv_width // HEAD_DIM
    groups = d_model // kv_width
    gw = groups * HEAD_DIM
    kcol, vcol = d_model // HEAD_DIM, (d_model + kv_width) // HEAD_DIM

    def kv_spec(col0, shift):
        def imap(b, h, n):
            return (b * nb + jnp.clip(2 * n + shift, 0, nb - 1), col0 + h)
        return pl.BlockSpec((bl, HEAD_DIM), imap)

    ctx_k = pl.BlockSpec((ctx_len, HEAD_DIM), lambda b, h, n: (b, kcol + h))
    ctx_v = pl.BlockSpec((ctx_len, HEAD_DIM), lambda b, h, n: (b, vcol + h))
    blocks = (2 * _nbytes((2 * bl, gw), BF16) + 8 * _nbytes((bl, HEAD_DIM), BF16)
              + 2 * _nbytes((ctx_len, HEAD_DIM), BF16))
    temps = 8 * _nbytes((2 * bl, 4 * bl + ctx_len), F32)
    return pl.pallas_call(
        functools.partial(_attn_kernel, npairs=npairs, groups=groups),
        grid=(batch, n_kv, npairs),
        in_specs=[pl.BlockSpec(memory_space=pltpu.SMEM),
                  pl.BlockSpec((2 * bl, gw), lambda b, h, n: (b * npairs + n, h))]
                 + [kv_spec(kcol, shift) for shift in (-1, 0, 1, 2)]
                 + [kv_spec(vcol, shift) for shift in (-1, 0, 1, 2)]
                 + [ctx_k, ctx_v],
        out_specs=pl.BlockSpec((2 * bl, gw), lambda b, h, n: (b * npairs + n, h)),
        out_shape=jax.ShapeDtypeStruct((batch * seq, d_model), BF16),
        compiler_params=_params(("parallel", "parallel", "arbitrary"), blocks, temps),
        name="window_attention",
    )(sink, *([z_l] * 9), z_c, z_c)


def _attn_ctx_kernel(sink_ref, q_ref, kx_ref, vx_ref, o_ref, *, groups):
    kvh = pl.program_id(1)
    rows = q_ref.shape[0]
    q = _stack_heads(q_ref[...], groups)
    s = _dot_nt(q, kx_ref[...]) * (HEAD_DIM ** -0.5)
    o = _softmax_pv(s, _sink_col(sink_ref, kvh, groups, rows), vx_ref[...])
    o_ref[...] = _unstack_heads(o, groups, rows).astype(BF16)


def _attention_ctx(qkv_c, sink, *, batch, ctx_len, d_model, kv_width):
    n_kv = kv_width // HEAD_DIM
    groups = d_model // kv_width
    gw = groups * HEAD_DIM
    kcol, vcol = d_model // HEAD_DIM, (d_model + kv_width) // HEAD_DIM
    blocks = 2 * _nbytes((ctx_len, gw), BF16) + 2 * _nbytes((ctx_len, HEAD_DIM), BF16)
    temps = 6 * _nbytes((groups * ctx_len, ctx_len), F32)
    return pl.pallas_call(
        functools.partial(_attn_ctx_kernel, groups=groups),
        grid=(batch, n_kv),
        in_specs=[pl.BlockSpec(memory_space=pltpu.SMEM),
                  pl.BlockSpec((ctx_len, gw), lambda b, h: (b, h)),
                  pl.BlockSpec((ctx_len, HEAD_DIM), lambda b, h: (b, kcol + h)),
                  pl.BlockSpec((ctx_len, HEAD_DIM), lambda b, h: (b, vcol + h))],
        out_specs=pl.BlockSpec((ctx_len, gw), lambda b, h: (b, h)),
        out_shape=jax.ShapeDtypeStruct((batch * ctx_len, d_model), BF16),
        compiler_params=_params(("parallel", "parallel"), blocks, temps),
        name="context_attention",
    )(sink, qkv_c, qkv_c, qkv_c)


CONV_PAD = SUBLANES
CONV_ROWS = 256


def _conv_seq(x_ref, w, o_ref, pad_ref, l2_flag, l2_scale):
    s = x_ref.shape[0]
    zeros = jnp.zeros((CONV_PAD, LANES), F32)
    pad_ref[pl.ds(0, CONV_PAD), :] = zeros
    pad_ref[pl.ds(CONV_PAD, s), :] = x_ref[...].astype(F32)
    pad_ref[pl.ds(CONV_PAD + s, CONV_PAD), :] = zeros
    rows = min(CONV_ROWS, s)

    def conv_silu(r0):
        y = jnp.zeros((rows, LANES), F32)
        for k in range(CONV_K):
            off = CONV_PAD - CONV_K // 2 + k
            y = y + pad_ref[pl.ds(r0 + off, rows), :] * w[k:k + 1, :]
        return _silu(y)

    def body_l2(it, carry):
        r0 = pl.multiple_of(it * rows, rows)
        y = conv_silu(r0)
        inv = lax.rsqrt(jnp.sum(y * y, axis=-1, keepdims=True) + EPS) * l2_scale
        o_ref[pl.ds(r0, rows), :] = (y * inv).astype(BF16)
        return carry

    def body_plain(it, carry):
        r0 = pl.multiple_of(it * rows, rows)
        o_ref[pl.ds(r0, rows), :] = conv_silu(r0).astype(BF16)
        return carry

    trips = s // rows
    unroll = 2 if trips % 2 == 0 else 1

    @pl.when(l2_flag)
    def _():
        lax.fori_loop(0, trips, body_l2, 0, unroll=unroll)

    @pl.when(jnp.logical_not(l2_flag))
    def _():
        lax.fori_loop(0, trips, body_plain, 0, unroll=unroll)


def _dnprep_kernel(xl_ref, xc_ref, w_ref, ol_ref, oc_ref, padl_ref, padc_ref, *, n_q):
    col = pl.program_id(1)
    l2_flag = col < 2 * n_q
    l2_scale = jnp.where(col < n_q, HEAD_DIM ** -0.5, 1.0)
    w = w_ref[...]
    _conv_seq(xl_ref, w, ol_ref, padl_ref, l2_flag, l2_scale)
    _conv_seq(xc_ref, w, oc_ref, padc_ref, l2_flag, l2_scale)


def _dn_prep(z_l, z_c, conv_w, *, col0, batch, seq, ctx_len, d_model):
    ncol = 3 * d_model // LANES
    blocks = 2 * (_nbytes((seq, LANES), BF16) + _nbytes((ctx_len, LANES), BF16)) + _nbytes((8, LANES), F32)
    scratch = _nbytes((seq + 2 * CONV_PAD, LANES), F32) + _nbytes((ctx_len + 2 * CONV_PAD, LANES), F32)
    return pl.pallas_call(
        functools.partial(_dnprep_kernel, n_q=d_model // LANES),
        grid=(batch, ncol),
        in_specs=[pl.BlockSpec((seq, LANES), lambda b, c: (b, col0 + c)),
                  pl.BlockSpec((ctx_len, LANES), lambda b, c: (b, col0 + c)),
                  pl.BlockSpec((CONV_K, LANES), lambda b, c: (0, c))],
        out_specs=[pl.BlockSpec((seq, LANES), lambda b, c: (b, c)),
                   pl.BlockSpec((ctx_len, LANES), lambda b, c: (b, c))],
        out_shape=[jax.ShapeDtypeStruct((batch * seq, 3 * d_model), BF16),
                   jax.ShapeDtypeStruct((batch * ctx_len, 3 * d_model), BF16)],
        scratch_shapes=[pltpu.VMEM((seq + 2 * CONV_PAD, LANES), F32),
                        pltpu.VMEM((ctx_len + 2 * CONV_PAD, LANES), F32)],
        compiler_params=_params(("parallel", "parallel"), blocks, scratch + 16 * _nbytes((CONV_ROWS, LANES), F32)),
        name="deltanet_prep",
    )(z_l, z_c, conv_w)


def _gates_kernel(ba_ref, alog_ref, dtb_ref, o_ref, *, n_heads):
    tm = ba_ref.shape[0]
    x = ba_ref[...]
    lane = lax.broadcasted_iota(jnp.int32, (CHUNK, LANES), 1)
    z = x + dtb_ref[...]
    softplus = jnp.maximum(z, 0.0) + jnp.log(1.0 + jnp.exp(-jnp.abs(z)))
    g = -jnp.exp(alog_ref[...]) * softplus
    beta = _sigmoid(x)
    r = lax.broadcasted_iota(jnp.int32, (CHUNK, CHUNK), 0)
    c = lax.broadcasted_iota(jnp.int32, (CHUNK, CHUNK), 1)
    tri_prefix = (c <= r).astype(BF16)
    tri_suffix = (c >= r).astype(BF16)
    for i in range(tm // CHUNK):
        rows = slice(i * CHUNK, (i + 1) * CHUNK)
        parts = _split3(g[rows])
        pre = sum(_dot(tri_prefix, p) for p in parts)
        suf = sum(_dot(tri_suffix, p) for p in parts)
        csum = jnp.where(lane < 3 * n_heads, pre, suf)
        o_ref[rows, :] = jnp.where(lane < 2 * n_heads, beta[rows], csum)


def _gates(ba, alog_vec, dtb_vec, *, n_heads):
    m = ba.shape[0]
    tm = _pick(m, (1024, 512, 256, 128))
    blocks = 2 * _nbytes((tm, LANES), F32)
    return pl.pallas_call(
        functools.partial(_gates_kernel, n_heads=n_heads),
        grid=(m // tm,),
        in_specs=[pl.BlockSpec((tm, LANES), lambda i: (i, 0)),
                  pl.BlockSpec((1, LANES), lambda i: (0, 0)),
                  pl.BlockSpec((1, LANES), lambda i: (0, 0))],
        out_specs=pl.BlockSpec((tm, LANES), lambda i: (i, 0)),
        out_shape=jax.ShapeDtypeStruct((m, LANES), F32),
        compiler_params=_params(("parallel",), blocks, 8 * _nbytes((tm, LANES), F32)),
        name="deltanet_gates",
    )(ba, alog_vec, dtb_vec)


def _dot_split(a, b):
    n = b[0].shape[1]
    lhs = jnp.concatenate(a, axis=1)
    half = jnp.concatenate(b, axis=1)
    out = _dot(lhs, jnp.concatenate([half, half], axis=0))
    return out[:, :n] + out[:, n:]


def _unit_tri_inverse_many(mats, hooks=()):
    n = mats[0].shape[0]
    row = lax.broadcasted_iota(jnp.int32, (n, n), 0)
    col = lax.broadcasted_iota(jnp.int32, (n, n), 1)
    splits = [_split2(a) for a in mats]
    eye = jnp.where(row == col, 1.0, 0.0)
    pair = (row // 2) == (col // 2)
    xs = [eye - jnp.where(pair, a, 0.0) for a in mats]
    s, level = 2, 0
    while s < n:
        couple = ((row // (2 * s)) == (col // (2 * s))) & ((row // s) != (col // s))
        zero16 = jnp.zeros((n, n), BF16)
        es = [(jnp.where(couple, hi, zero16), jnp.where(couple, lo, zero16)) for hi, lo in splits]
        xsp = [_split2(x) for x in xs]
        exs = [_split2(_dot_split(e, xp)) for e, xp in zip(es, xsp)]
        xs = [x - _dot_split(xp, ex) for x, xp, ex in zip(xs, xsp, exs)]
        if level < len(hooks):
            hooks[level]()
        s *= 2
        level += 1
    return xs


def _scan_prep_group(refs, chunks, dsts, sel, scr, hooks=()):
    q_ref, k_ref, v_ref, p_ref, crow_refs = refs
    u_s, w_s, qk_s, qd_s, kdt_s, gl_s = scr
    c = CHUNK
    group = len(chunks)
    hooks = list(hooks)
    if hooks:
        hooks.pop(0)()
    row = lax.broadcasted_iota(jnp.int32, (c, c), 0)
    col = lax.broadcasted_iota(jnp.int32, (c, c), 1)
    eye16 = (row == col).astype(BF16)
    r0s = [pl.multiple_of(ci * c, c) for ci in chunks]
    q16s = [q_ref[pl.ds(r0, c), :] for r0 in r0s]
    k16s = [k_ref[pl.ds(r0, c), :] for r0 in r0s]
    qs = [q.astype(F32) for q in q16s]
    ks = [k.astype(F32) for k in k16s]
    vs = [v_ref[pl.ds(r0, c), :].astype(F32) for r0 in r0s]
    sel2 = jnp.concatenate([sel, sel], axis=0)
    parts = [_split3(p_ref[pl.ds(r0, c), :]) for r0 in r0s]
    gates = [_dot(jnp.concatenate([hi, mid], axis=1), sel2) + _dot(lo, sel) for hi, mid, lo in parts]
    kks = [_dot_nt(k16, k16) for k16 in k16s]
    qks = [_dot_nt(q16, k16) for q16, k16 in zip(q16s, k16s)]
    probs = [(g, d) for g in range(group) for d in range(2)]
    betas = [gates[g][:, d * LANES:(d + 1) * LANES] for g, d in probs]
    ccols = [gates[g][:, (2 + d) * LANES:(3 + d) * LANES] for g, d in probs]
    decays, mats = [], []
    for (g, d), beta, ccol in zip(probs, betas, ccols):
        crow = crow_refs[d][0, :, pl.ds(r0s[g], c)]
        incl = (row >= col) if d == 0 else (row <= col)
        strict = (row > col) if d == 0 else (row < col)
        decay = jnp.exp(jnp.where(incl, ccol - crow, NEG_INF))
        decays.append(decay)
        mats.append(jnp.where(strict, beta * kks[g] * decay, 0.0))
    if hooks:
        hooks.pop(0)()
    xsp = [_split2(x) for x in _unit_tri_inverse_many(mats, hooks)]
    expcs = [jnp.exp(ccol) for ccol in ccols]
    rus = [_split2(vs[g] * beta) for (g, d), beta in zip(probs, betas)]
    us = [_dot_split(xp, ru) for xp, ru in zip(xsp, rus)]
    rws = [(ks[g] * beta * expc).astype(BF16) for (g, d), beta, expc in zip(probs, betas, expcs)]
    ws = [_dot(jnp.concatenate(xp, axis=1), jnp.concatenate([rw, rw], axis=0)) for xp, rw in zip(xsp, rws)]
    clasts = [ccol[c - 1:c, :] if d == 0 else ccol[0:1, :] for (g, d), ccol in zip(probs, ccols)]
    kdts = [_dot_nt(eye16, (ks[g] * jnp.exp(clast - ccol)).astype(BF16))
            for (g, d), clast, ccol in zip(probs, clasts, ccols)]
    for i, (g, d) in enumerate(probs):
        rows = pl.ds(pl.multiple_of(dsts[g] * c, c), c)
        u_s[d, rows, :] = us[i]
        w_s[d, rows, :] = ws[i].astype(BF16)
        qk_s[d, rows, :] = (qks[g] * decays[i]).astype(BF16)
        qd_s[d, rows, :] = (qs[g] * expcs[i]).astype(BF16)
        kdt_s[d, rows, :] = kdts[i].astype(BF16)
        gl_s[d, pl.ds(pl.multiple_of(dsts[g] * SUBLANES, SUBLANES), SUBLANES), :] = jnp.broadcast_to(
            jnp.exp(clasts[i]), (SUBLANES, LANES))


def _step_issue(chunks, states, scr):
    u_s, w_s, qk_s, qd_s, kdt_s, gl_s = scr
    c = CHUNK
    rows = [pl.ds(pl.multiple_of(ci * c, c), c) for ci in chunks]
    dirs = range(2)
    ws = [_dot(jnp.concatenate([w_s[d, rows[d], :], qd_s[d, rows[d], :]], axis=0), states[d].astype(BF16))
          for d in dirs]
    return dict(chunks=chunks, rows=rows, states=states, ws=ws)


def _step_finish(pending, scr):
    u_s, w_s, qk_s, qd_s, kdt_s, gl_s = scr
    c = CHUNK
    chunks, rows, states, ws = pending["chunks"], pending["rows"], pending["states"], pending["ws"]
    dirs = range(2)
    v_new = [(u_s[d, rows[d], :] - ws[d][:c]).astype(BF16) for d in dirs]
    outs = [ws[d][c:] + _dot(qk_s[d, rows[d], :], v_new[d]) for d in dirs]
    gls = [gl_s[d, pl.ds(pl.multiple_of(chunks[d] * SUBLANES, SUBLANES), SUBLANES), :][0:1, :] for d in dirs]
    new_states = [states[d] * gls[d] + _dot(kdt_s[d, rows[d], :], v_new[d]) for d in dirs]
    return tuple(new_states), outs


def _scan_step_pair(chunks, states, scr):
    return _step_finish(_step_issue(chunks, states, scr), scr)


PREP_GROUP = 4


def _scan_kernel(ql, kl, vl, pl_l, crfl, crbl, zl_prev, qx, kx, vx, px, crfx, crbx, zx, gain_ref,
                 outl_prev_ref, outx_ref, u_s, w_s, qk_s, qd_s, kdt_s, gl_s, o_s, st_s, *, n_heads):
    b, j = pl.program_id(0), pl.program_id(1)
    c = CHUNK
    ncx, ncl = qx.shape[0] // c, ql.shape[0] // c
    half = PREP_GROUP // 2
    n_groups = ncl // PREP_GROUP
    assert PREP_GROUP == 4 and ncx % 2 == 0 and ncl % PREP_GROUP == 0 and ncx <= PREP_GROUP
    cur = j % 2
    factor_refs = (u_s, w_s, qk_s, qd_s, kdt_s, gl_s)
    scr_cur, o_cur = tuple(r.at[cur] for r in factor_refs), o_s.at[cur]
    scr_prev, o_prev = tuple(r.at[1 - cur] for r in factor_refs), o_s.at[1 - cur]
    gain = gain_ref[...]

    def chunk_rows(i):
        return pl.ds(pl.multiple_of(i * c, c), c)

    @pl.when((b == 0) & (j == 0))
    def _():
        def body(i, carry):
            for r in factor_refs[:5] + (o_s,):
                for d in range(2):
                    r[1, d, chunk_rows(i), :] = jnp.zeros((c, LANES), r.dtype)
            return carry
        lax.fori_loop(0, ncx + ncl, body, 0)
        gl_s[1] = jnp.zeros(gl_s.shape[1:], F32)
        st_s[...] = jnp.zeros(st_s.shape, F32)

    def finish(z_ref, out_ref, i, o):
        on = o * lax.rsqrt(jnp.mean(o * o, axis=-1, keepdims=True) + EPS) * gain
        out_ref[chunk_rows(i), :] = (on * _silu(z_ref[chunk_rows(i), :].astype(F32))).astype(BF16)

    def step_stages(scr, o_ref, base, n, it, box, key, z_ref=None, out_ref=None):
        pending = {}
        jf, jb = it, n - 1 - it

        def issue():
            pending.update(_step_issue((base + jf, base + jb), box[key], scr))

        def complete():
            box[key], outs = _step_finish(pending, scr)
            if z_ref is None:
                o_ref[0, chunk_rows(base + jf), :] = outs[0]
                o_ref[1, chunk_rows(base + jb), :] = outs[1]
            else:
                finish(z_ref, out_ref, jf, outs[0] + o_ref[1, chunk_rows(base + jf), :])
                finish(z_ref, out_ref, jb, outs[1] + o_ref[0, chunk_rows(base + jb), :])
        return [issue, complete]

    def run(stages):
        for stage in stages:
            stage()

    def front(i):
        return [half * i + k for k in range(half)]

    def drain_stages(p, box):
        if isinstance(p, int) and p < half:
            return step_stages(scr_prev, o_prev, ncx, ncl, front(n_groups - 1)[p], box, "prev")
        return step_stages(scr_prev, o_prev, ncx, ncl, ncl // 2 + p - half, box, "prev", zl_prev, outl_prev_ref)

    n_drain = half + ncl // 2

    @pl.when(j < n_heads)
    def _():
        r = lax.broadcasted_iota(jnp.int32, (LANES, 4 * LANES), 0)
        t = lax.broadcasted_iota(jnp.int32, (LANES, 4 * LANES), 1) // LANES
        sel = (r == t * n_heads + j).astype(BF16)
        zero = jnp.zeros((HEAD_DIM, HEAD_DIM), F32)
        box = {"cur": (zero, zero), "prev": (st_s[0], st_s[1])}

        _scan_prep_group((qx, kx, vx, px, (crfx, crbx)), list(range(ncx)), list(range(ncx)), sel, scr_cur)
        for it in range(ncx):
            ends = (zx, outx_ref) if it >= ncx // 2 else ()
            run(step_stages(scr_cur, o_cur, 0, ncx, it, box, "cur", *ends))

        def prep_latent(i, hooks):
            chunks = front(i) + [ncl - 1 - ci for ci in reversed(front(i))]
            _scan_prep_group((ql, kl, vl, pl_l, (crfl, crbl)), chunks, [ncx + ci for ci in chunks], sel,
                             scr_cur, hooks)

        prep_latent(0, [stage for p in range(2 * half) for stage in drain_stages(p, box)])

        def overlapped(i, carry):
            box = {"cur": carry[:2], "prev": carry[2:]}
            hooks = []
            for k, it in enumerate(front(i - 1)):
                own = step_stages(scr_cur, o_cur, ncx, ncl, it, box, "cur")
                other = drain_stages(2 * half + half * (i - 1) + k, box)
                hooks += [own[0], other[0], own[1], other[1]]
            prep_latent(i, hooks)
            return tuple(box["cur"]) + tuple(box["prev"])

        carry = lax.fori_loop(1, n_groups, overlapped, tuple(box["cur"]) + tuple(box["prev"]))
        st_s[0] = carry[0]
        st_s[1] = carry[1]

    @pl.when(j == n_heads)
    def _():
        box = {"prev": (st_s[0], st_s[1])}
        for p in range(half):
            run(drain_stages(p, box))

        def body(p, states):
            inner = {"prev": states}
            run(drain_stages(p, inner))
            return tuple(inner["prev"])
        lax.fori_loop(half, n_drain, body, tuple(box["prev"]))


def _delta_scan(dn_l, dn_c, gates_l, gates_c, gates_lt, gates_ct, z_l, z_c, gain,
                *, zcol, batch, seq, ctx_len, d_model):
    n_heads = d_model // HEAD_DIM
    t = seq + ctx_len
    last = n_heads - 1

    def col_spec(rows, col0, lag=0):
        return pl.BlockSpec((rows, LANES), lambda b, j: (b, col0 + jnp.clip(j - lag, 0, last)))

    def gate_spec(rows):
        return pl.BlockSpec((rows, LANES), lambda b, j: (b, 0))

    def crow_spec(rows, d):
        return pl.BlockSpec((1, 1, rows), lambda b, j: ((2 + d) * n_heads + jnp.minimum(j, last), 0, b))

    def side(rows, z_lag):
        return [col_spec(rows, 0), col_spec(rows, n_heads), col_spec(rows, 2 * n_heads), gate_spec(rows),
                crow_spec(rows, 0), crow_spec(rows, 1), col_spec(rows, zcol, z_lag)]

    blocks = (_nbytes((t, LANES), F32) + 2 * _nbytes((SUBLANES, t), F32) + 5 * _nbytes((t, LANES), BF16))
    slots = 2
    scratch_shapes = [pltpu.VMEM((slots, 2, t, LANES), F32),
                      pltpu.VMEM((slots, 2, t, LANES), BF16),
                      pltpu.VMEM((slots, 2, t, LANES), BF16),
                      pltpu.VMEM((slots, 2, t, LANES), BF16),
                      pltpu.VMEM((slots, 2, t, LANES), BF16),
                      pltpu.VMEM((slots, 2, (t // CHUNK) * SUBLANES, LANES), F32),
                      pltpu.VMEM((slots, 2, t, LANES), F32),
                      pltpu.VMEM((2, HEAD_DIM, HEAD_DIM), F32)]
    scratch = slots * (2 * 2 * _nbytes((t, LANES), F32) + 4 * 2 * _nbytes((t, LANES), BF16))
    return pl.pallas_call(
        functools.partial(_scan_kernel, n_heads=n_heads),
        grid=(batch, n_heads + 1),
        in_specs=side(seq, 1) + side(ctx_len, 0) + [pl.BlockSpec((1, LANES), lambda b, j: (0, 0))],
        out_specs=[pl.BlockSpec((seq, LANES), lambda b, j: (b, jnp.clip(j - 1, 0, last))),
                   pl.BlockSpec((ctx_len, LANES), lambda b, j: (b, jnp.minimum(j, last)))],
        out_shape=[jax.ShapeDtypeStruct((batch * seq, d_model), BF16),
                   jax.ShapeDtypeStruct((batch * ctx_len, d_model), BF16)],
        scratch_shapes=scratch_shapes,
        compiler_params=_params(("arbitrary", "arbitrary"), blocks, scratch + 4 * 1024 * 1024),
        name="delta_scan",
    )(dn_l, dn_l, dn_l, gates_l, gates_lt, gates_lt, z_l,
      dn_c, dn_c, dn_c, gates_c, gates_ct, gates_ct, z_c, gain)


def _merge_kernel(att_ref, dn_ref, ga_ref, gd_ref, wa_ref, wd_ref, o_ref):
    y = (_sigmoid(ga_ref[...].astype(F32)) * _dot(att_ref[...], wa_ref[...])
         + _sigmoid(gd_ref[...].astype(F32)) * _dot(dn_ref[...], wd_ref[...]))
    o_ref[...] = y.astype(BF16)


def _merge(att, dn, z, w_attn, w_delta, layer, *, gate_col, tm, tn):
    m, d = att.shape
    ga0, gd0 = gate_col // tn, (gate_col + d) // tn
    blocks = (2 * _nbytes((tm, d), BF16) + 2 * _nbytes((tm, tn), BF16) + 2 * _nbytes((d, tn), BF16)
              + _nbytes((tm, tn), BF16))
    return pl.pallas_call(
        _merge_kernel,
        grid=(m // tm, d // tn),
        in_specs=[pl.BlockSpec((tm, d), lambda i, j: (i, 0)),
                  pl.BlockSpec((tm, d), lambda i, j: (i, 0)),
                  pl.BlockSpec((tm, tn), lambda i, j: (i, ga0 + j)),
                  pl.BlockSpec((tm, tn), lambda i, j: (i, gd0 + j)),
                  pl.BlockSpec((None, d, tn), lambda i, j: (layer, 0, j)),
                  pl.BlockSpec((None, d, tn), lambda i, j: (layer, 0, j))],
        out_specs=pl.BlockSpec((tm, tn), lambda i, j: (i, j)),
        out_shape=jax.ShapeDtypeStruct((m, d), BF16),
        compiler_params=_params(("parallel", "arbitrary"), blocks, 4 * _nbytes((tm, tn), F32)),
        name="merge",
    )(att, dn, z, z, w_attn, w_delta)


def _resid_kernel(a_ref, w_ref, x_ref, mod_ref, o_ref, *, gate_row):
    o_ref[...] = x_ref[...] + mod_ref[0, gate_row:gate_row + 1, :] * _dot(a_ref[...], w_ref[...])


def _resid_matmul(a, w, layer, x, mods, mod_row, gate_row, *, tm, tn, name):
    m, k = a.shape
    d = w.shape[2]
    blocks = (_nbytes((tm, k), BF16) + _nbytes((k, tn), BF16) + 2 * _nbytes((tm, tn), F32)
              + _nbytes((6, tn), F32))
    return pl.pallas_call(
        functools.partial(_resid_kernel, gate_row=gate_row),
        grid=(m // tm, d // tn),
        in_specs=[pl.BlockSpec((tm, k), lambda i, j: (i, 0)),
                  pl.BlockSpec((None, k, tn), lambda i, j: (layer, 0, j)),
                  pl.BlockSpec((tm, tn), lambda i, j: (i, j)),
                  pl.BlockSpec((1, 6, tn), lambda i, j: (mod_row(i, tm), 0, j))],
        out_specs=pl.BlockSpec((tm, tn), lambda i, j: (i, j)),
        out_shape=jax.ShapeDtypeStruct((m, d), F32),
        compiler_params=_params(("parallel", "arbitrary"), blocks, 2 * _nbytes((tm, tn), F32)),
        name=name,
    )(a, w, x, mods)


def _ffn_up_kernel(x_ref, mod_ref, g_ref, wg_ref, wu_ref, o_ref, h_ref):
    j = pl.program_id(1)
    tm = o_ref.shape[0]
    sub = min(tm, INPROJ_SUB_ROWS)

    def tile(fresh):
        for r in range(tm // sub):
            rows = pl.ds(r * sub, sub)
            if fresh:
                h = _norm_mod(x_ref[rows, :], g_ref[...], mod_ref[0, 3:4, :], mod_ref[0, 4:5, :]).astype(BF16)
                h_ref[rows, :] = h
            else:
                h = h_ref[rows, :]
            o_ref[rows, :] = (_silu(_dot(h, wg_ref[...])) * _dot(h, wu_ref[...])).astype(BF16)

    @pl.when(j == 0)
    def _():
        tile(True)

    @pl.when(j > 0)
    def _():
        tile(False)


def _ffn_up(x, mods, mod_row, gain, w_gate_up, layer, *, tm, tn):
    m, d = x.shape
    f = w_gate_up.shape[2] // 2
    nj = f // tn
    blocks = (_nbytes((tm, d), F32) + _nbytes((6, d), F32) + 2 * _nbytes((d, tn), BF16)
              + _nbytes((tm, tn), BF16))
    return pl.pallas_call(
        _ffn_up_kernel,
        grid=(m // tm, nj),
        in_specs=[pl.BlockSpec((tm, d), lambda i, j: (i, 0)),
                  pl.BlockSpec((1, 6, d), lambda i, j: (mod_row(i, tm), 0, 0)),
                  pl.BlockSpec((1, d), lambda i, j: (0, 0)),
                  pl.BlockSpec((None, d, tn), lambda i, j: (layer, 0, j)),
                  pl.BlockSpec((None, d, tn), lambda i, j: (layer, 0, nj + j))],
        out_specs=pl.BlockSpec((tm, tn), lambda i, j: (i, j)),
        out_shape=jax.ShapeDtypeStruct((m, f), BF16),
        scratch_shapes=[pltpu.VMEM((tm, d), BF16)],
        compiler_params=_params(("parallel", "arbitrary"), blocks,
                                _nbytes((tm, d), BF16) + 4 * _nbytes((tm, tn), F32)),
        name="ffn_up",
    )(x, mods, gain, w_gate_up, w_gate_up)


def _final_norm_kernel(x_ref, g_ref, o_ref):
    x = x_ref[...]
    o_ref[...] = x * lax.rsqrt(jnp.mean(x * x, axis=-1, keepdims=True) + EPS) * g_ref[...]


def _final_norm(x, gain, *, tm):
    m, d = x.shape
    return pl.pallas_call(
        _final_norm_kernel,
        grid=(m // tm,),
        in_specs=[pl.BlockSpec((tm, d), lambda i: (i, 0)), pl.BlockSpec((1, d), lambda i: (0, 0))],
        out_specs=pl.BlockSpec((tm, d), lambda i: (i, 0)),
        out_shape=jax.ShapeDtypeStruct((m, d), F32),
        compiler_params=_params(("parallel",), 2 * _nbytes((tm, d), F32), 2 * _nbytes((tm, d), F32)),
        name="final_norm",
    )(x, gain)


def _rope_tables(seq):
    quarter = HEAD_DIM // 4
    pos = jnp.arange(seq)
    row = (pos // GRID_W).astype(F32)
    col = (pos % GRID_W).astype(F32)
    inv_freq = ROPE_BASE ** (-jnp.arange(quarter, dtype=F32) / quarter)
    ang_r, ang_c = row[:, None] * inv_freq[None], col[:, None] * inv_freq[None]
    cos = jnp.concatenate([jnp.cos(ang_r), jnp.cos(ang_c)] * 2, axis=1)
    sin = jnp.concatenate([-jnp.sin(ang_r), -jnp.sin(ang_c), jnp.sin(ang_r), jnp.sin(ang_c)], axis=1)
    return cos, sin


def _rope_perm():
    quarter = HEAD_DIM // 4
    order = (0, 2, 1, 3)
    return jnp.concatenate([jnp.arange(quarter) + quarter * o for o in order])


def kernel(x, c, ctx, c_ctx, w_mod, b_mod, norm1, norm2, w_in, conv_w, attn_sink, a_log, dt_bias, dn_norm,
           w_attn_proj, w_delta_proj, w_out, w_gate_up, w_down, final_norm):
    batch, seq, d = x.shape
    ctx_len = ctx.shape[1]
    depth = w_mod.shape[0]
    n_heads = d // HEAD_DIM
    kv_width = (n_heads // 4) * HEAD_DIM
    qkv_width = d + 2 * kv_width
    assert seq % CHUNK == 0 and ctx_len % CHUNK == 0 and seq % ATTN_BLOCK == 0
    assert 4 * n_heads <= LANES and (batch * seq) % ctx_len == 0

    off_ba = qkv_width + 4 * d
    qk_width = d + kv_width
    w_qk = w_in[:, :, :qk_width].reshape(depth, d, qk_width // HEAD_DIM, HEAD_DIM)[..., _rope_perm()]
    w_main = jnp.concatenate([w_qk.reshape(depth, d, qk_width), w_in[:, :, qk_width:off_ba],
                              w_in[:, :, off_ba + 4 * n_heads:]], axis=2).astype(BF16)
    w_ba = jnp.pad(w_in[:, :, off_ba:off_ba + 4 * n_heads], ((0, 0), (0, 0), (0, LANES - 4 * n_heads))).astype(BF16)
    w_attn16, w_delta16, w_out16 = w_attn_proj.astype(BF16), w_delta_proj.astype(BF16), w_out.astype(BF16)
    w_gu16, w_down16 = w_gate_up.astype(BF16), w_down.astype(BF16)

    lane_pad = ((0, 0), (2 * n_heads, LANES - 4 * n_heads))
    alog_vec = jnp.pad(a_log.reshape(depth, 2 * n_heads), lane_pad).reshape(depth, 1, LANES)
    dtb_vec = jnp.pad(dt_bias.reshape(depth, 2 * n_heads), lane_pad).reshape(depth, 1, LANES)

    cos_l, sin_l = _rope_tables(seq)
    cos_l, sin_l = jnp.tile(cos_l, (batch, 1)), jnp.tile(sin_l, (batch, 1))
    cos_c = jnp.ones((batch * ctx_len, LANES), F32)
    sin_c = jnp.zeros((batch * ctx_len, LANES), F32)

    mod_rows = ((batch + 1 + SUBLANES - 1) // SUBLANES) * SUBLANES
    cc = jnp.concatenate([c, c_ctx[None], jnp.zeros((mod_rows - batch - 1, d), F32)], axis=0)
    mods_all = _modulation(cc, w_mod, b_mod).reshape(depth, mod_rows, 6, d)

    tm_l = _pick(seq, (1024, 512, 256, 128))
    tm_c = _pick(batch * ctx_len, (1024, 512, 256, 128))
    tn = _pick(kv_width, (512, 256, 128))
    tn_in = _pick(w_main.shape[2], (1024, 512, 256, 128))

    def row_l(i, tm):
        return (i * tm) // seq

    def row_c(i, tm):
        return batch

    xl = x.reshape(batch * seq, d)
    xc = ctx.reshape(batch * ctx_len, d)
    dims = dict(batch=batch, seq=seq, ctx_len=ctx_len, d_model=d)

    for l in range(depth):
        mods = mods_all[l]
        need_ctx = l < depth - 1
        g1, g2 = norm1[l][None], norm2[l][None]
        proj = functools.partial(_inproj, gain=g1, w_main=w_main, w_ba=w_ba, layer=l, tn=tn_in,
                                 qk_width=d + kv_width)
        z_l, ba_l = proj(xl, mods, row_l, cos=cos_l, sin=sin_l, tm=tm_l)
        z_c, ba_c = proj(xc, mods, row_c, cos=cos_c, sin=sin_c, tm=tm_c)

        att_l = _attention(z_l, z_c, attn_sink[l], kv_width=kv_width, **dims)

        dn_l, dn_c = _dn_prep(z_l, z_c, conv_w[l], col0=qkv_width // LANES, **dims)
        gates_l = _gates(ba_l, alog_vec[l], dtb_vec[l], n_heads=n_heads)
        gates_c = _gates(ba_c, alog_vec[l], dtb_vec[l], n_heads=n_heads)
        gates_lt = gates_l.T.reshape(LANES, 1, batch * seq)
        gates_ct = gates_c.T.reshape(LANES, 1, batch * ctx_len)
        do_l, do_c = _delta_scan(dn_l, dn_c, gates_l, gates_c, gates_lt, gates_ct, z_l, z_c,
                                 dn_norm[l][None], zcol=(qkv_width + 3 * d) // LANES, **dims)

        streams = [(xl, att_l, do_l, z_l, row_l, tm_l)]
        if need_ctx:
            att_c = _attention_ctx(z_c, attn_sink[l], batch=batch, ctx_len=ctx_len, d_model=d, kv_width=kv_width)
            streams.append((xc, att_c, do_c, z_c, row_c, tm_c))
        outs = []
        for xs, att, dn, z, row, tm in streams:
            y = _merge(att, dn, z, w_attn16, w_delta16, l, gate_col=qkv_width + 4 * d, tm=tm, tn=tn)
            xs = _resid_matmul(y, w_out16, l, xs, mods, row, 2, tm=tm, tn=tn, name="out_proj")
            act = _ffn_up(xs, mods, row, g2, w_gu16, l, tm=tm, tn=tn)
            xs = _resid_matmul(act, w_down16, l, xs, mods, row, 5, tm=tm, tn=tn, name="ffn_down")
            outs.append(xs)
        xl = outs[0]
        if need_ctx:
            xc = outs[1]

    return _final_norm(xl, final_norm[None], tm=tm_l).reshape(batch, seq, d)
```

```python
import functools

import jax
import jax.numpy as jnp
from jax import lax
from jax.experimental import pallas as pl
from jax.experimental.pallas import tpu as pltpu

F32 = jnp.float32
BF16 = jnp.bfloat16

HEAD_DIM = 128
WINDOW = 128
ATTN_BLOCK = 128
GRID_W = 64
CONV_K = 5
ROPE_BASE = 10000.0
EPS = 1e-6
NEG_INF = -1e30
LOG2_E = 1.4426950408889634
CHUNK = 128
LANES = 128
SUBLANES = 8
VMEM_CAP_BYTES = 60 * 1024 * 1024


def _vmem_limit(block_bytes, scratch_bytes=0):
    est = 2 * block_bytes + scratch_bytes
    return int(min(VMEM_CAP_BYTES, max(32 * 1024 * 1024, est * 3 // 2)))


def _params(sem, block_bytes, scratch_bytes=0):
    return pltpu.CompilerParams(dimension_semantics=sem,
                                vmem_limit_bytes=_vmem_limit(block_bytes, scratch_bytes))


def _pick(n, candidates):
    for c in candidates:
        if n % c == 0:
            return c
    raise ValueError(f"no tile in {candidates} divides {n}")


def _nbytes(shape, dtype):
    n = 1
    for s in shape:
        n *= s
    return n * jnp.dtype(dtype).itemsize


def _sigmoid(x):
    return 1.0 / (1.0 + jnp.exp(-x))


def _silu(x):
    return x * _sigmoid(x)


def _dot(a, b):
    return jnp.dot(a, b, preferred_element_type=F32)


def _dot_nt(a, b):
    return lax.dot_general(a, b, (((1,), (1,)), ((), ())), preferred_element_type=F32)


def _split2(x):
    hi = x.astype(BF16)
    lo = (x - hi.astype(F32)).astype(BF16)
    return hi, lo


def _split3(x):
    hi = x.astype(BF16)
    r = x - hi.astype(F32)
    mid = r.astype(BF16)
    lo = (r - mid.astype(F32)).astype(BF16)
    return hi, mid, lo


def _mod_kernel(c_ref, w_ref, b_ref, o_ref):
    s = _silu(c_ref[...])
    s_hi, s_lo = _split2(s)
    w = w_ref[0]
    w_hi, w_lo = _split2(w)
    acc = _dot(s_hi, w_hi) + _dot(s_hi, w_lo) + _dot(s_lo, w_hi)
    o_ref[0] = acc + b_ref[0]


def _modulation(cc, w_mod, b_mod):
    depth, d, n = w_mod.shape
    rows = cc.shape[0]
    tn = _pick(n, (512, 256, 128))
    blocks = _nbytes((rows, d), F32) + _nbytes((d, tn), F32) + _nbytes((1, tn), F32) + _nbytes((rows, tn), F32)
    return pl.pallas_call(
        _mod_kernel,
        grid=(depth, n // tn),
        in_specs=[pl.BlockSpec((rows, d), lambda l, j: (0, 0)),
                  pl.BlockSpec((1, d, tn), lambda l, j: (l, 0, j)),
                  pl.BlockSpec((1, 1, tn), lambda l, j: (l, 0, j))],
        out_specs=pl.BlockSpec((1, rows, tn), lambda l, j: (l, 0, j)),
        out_shape=jax.ShapeDtypeStruct((depth, rows, n), F32),
        compiler_params=_params(("parallel", "parallel"), blocks, 3 * _nbytes((d, tn), F32)),
        name="modulation",
    )(cc, w_mod, b_mod.reshape(depth, 1, n))


INPROJ_SUB_ROWS = 256


def _norm_mod(x, gain, shift, scale):
    y = x * lax.rsqrt(jnp.mean(x * x, axis=-1, keepdims=True) + EPS) * gain
    return y * (1.0 + scale) + shift


def _inproj_kernel(x_ref, mod_ref, g_ref, w_ref, wba_ref, cos_ref, sin_ref, z_ref, ba_ref, h_ref, *, qk_width):
    j = pl.program_id(1)
    tm, tn = z_ref.shape
    full, rem = divmod(qk_width, tn)
    sub = min(tm, INPROJ_SUB_ROWS)

    def rope(t, rows):
        width = t.shape[1]
        cos = jnp.concatenate([cos_ref[rows, :]] * (width // HEAD_DIM), axis=1)
        sin = jnp.concatenate([sin_ref[rows, :]] * (width // HEAD_DIM), axis=1)
        partner = jnp.concatenate([pltpu.roll(t[:, g * HEAD_DIM:(g + 1) * HEAD_DIM], HEAD_DIM // 2, 1)
                                   for g in range(width // HEAD_DIM)], axis=1)
        return t * cos + partner * sin

    def tile(epilogue, fresh=False):
        for r in range(tm // sub):
            rows = pl.ds(r * sub, sub)
            if fresh:
                h = _norm_mod(x_ref[rows, :], g_ref[...], mod_ref[0, 0:1, :], mod_ref[0, 1:2, :]).astype(BF16)
                h_ref[rows, :] = h
                ba_ref[rows, :] = _dot(h, wba_ref[...])
            else:
                h = h_ref[rows, :]
            z_ref[rows, :] = epilogue(_dot(h, w_ref[...]), rows).astype(BF16)

    assert full >= 1

    @pl.when(j == 0)
    def _():
        tile(rope, fresh=True)

    @pl.when((j > 0) & (j < full))
    def _():
        tile(rope)

    if rem:
        @pl.when(j == full)
        def _():
            tile(lambda acc, rows: jnp.concatenate([rope(acc[:, :rem], rows), acc[:, rem:]], axis=1))

    @pl.when(j >= full + (1 if rem else 0))
    def _():
        tile(lambda acc, rows: acc)


def _inproj(x, mods, mod_row, gain, w_main, w_ba, layer, cos, sin, *, tm, tn, qk_width):
    m, d = x.shape
    n = w_main.shape[2]
    blocks = (_nbytes((tm, d), F32) + _nbytes((6, d), F32) + _nbytes((d, tn), BF16) + _nbytes((d, LANES), BF16)
              + 2 * _nbytes((tm, LANES), F32) + _nbytes((tm, tn), BF16) + _nbytes((tm, LANES), F32))
    return pl.pallas_call(
        functools.partial(_inproj_kernel, qk_width=qk_width),
        grid=(m // tm, n // tn),
        in_specs=[pl.BlockSpec((tm, d), lambda i, j: (i, 0)),
                  pl.BlockSpec((1, 6, d), lambda i, j: (mod_row(i, tm), 0, 0)),
                  pl.BlockSpec((1, d), lambda i, j: (0, 0)),
                  pl.BlockSpec((None, d, tn), lambda i, j: (layer, 0, j)),
                  pl.BlockSpec((None, d, LANES), lambda i, j: (layer, 0, 0)),
                  pl.BlockSpec((tm, LANES), lambda i, j: (i, 0)),
                  pl.BlockSpec((tm, LANES), lambda i, j: (i, 0))],
        out_specs=[pl.BlockSpec((tm, tn), lambda i, j: (i, j)),
                   pl.BlockSpec((tm, LANES), lambda i, j: (i, 0))],
        out_shape=[jax.ShapeDtypeStruct((m, n), BF16),
                   jax.ShapeDtypeStruct((m, LANES), F32)],
        scratch_shapes=[pltpu.VMEM((tm, d), BF16)],
        compiler_params=_params(("parallel", "arbitrary"), blocks,
                                _nbytes((tm, d), BF16) + 3 * _nbytes((tm, tn), F32)),
        name="inproj",
    )(x, mods, gain, w_main, w_ba, cos, sin)


def _softmax_pv(s, sink_col, v):
    m = jnp.maximum(jnp.max(s, axis=-1, keepdims=True), sink_col)
    p = jnp.exp(s - m)
    denom = jnp.sum(p, axis=-1, keepdims=True) + jnp.exp(sink_col - m)
    return _dot((p * (1.0 / denom)).astype(BF16), v)


def _stack_heads(q, groups):
    return jnp.concatenate([q[:, g * HEAD_DIM:(g + 1) * HEAD_DIM] for g in range(groups)], axis=0)


def _unstack_heads(o, groups, rows):
    return jnp.concatenate([o[g * rows:(g + 1) * rows] for g in range(groups)], axis=1)


def _sink_col(sink_ref, kvh, groups, rows):
    return jnp.concatenate([jnp.full((rows, 1), sink_ref[kvh * groups + g], F32) for g in range(groups)], axis=0)


def _attn_kernel(sink_ref, q_ref, k0, k1, k2, k3, v0, v1, v2, v3, kx_ref, vx_ref, o_ref, *, npairs, groups):
    kvh, n = pl.program_id(1), pl.program_id(2)
    bl = ATTN_BLOCK
    k = jnp.concatenate([k0[...], k1[...], k2[...], k3[...], kx_ref[...]], axis=0)
    v = jnp.concatenate([v0[...], v1[...], v2[...], v3[...], vx_ref[...]], axis=0)
    row = lax.broadcasted_iota(jnp.int32, (2 * bl, 3 * bl), 0)
    col = lax.broadcasted_iota(jnp.int32, (2 * bl, 3 * bl), 1)
    sub = row // bl
    valid = ((jnp.abs(col - bl - (row % bl)) <= WINDOW)
             & ((col >= bl) | (sub == 1) | (n > 0)) & ((col < 2 * bl) | (sub == 0) | (n < npairs - 1)))

    v_ones = jnp.concatenate([v, jnp.ones(v.shape, BF16)], axis=1)

    def scores(g):
        s = _dot_nt(q_ref[:, g * HEAD_DIM:(g + 1) * HEAD_DIM], k) * (HEAD_DIM ** -0.5 * LOG2_E)
        local = jnp.concatenate([s[:bl, :3 * bl], s[bl:, bl:4 * bl]], axis=0)
        return jnp.concatenate([jnp.where(valid, local, NEG_INF), s[:, 4 * bl:]], axis=1)

    def attend(s, sink):
        sink = sink * LOG2_E
        m = jnp.maximum(jnp.max(s, axis=-1, keepdims=True), sink)
        p = jnp.exp2(s - m).astype(BF16)
        zeros = jnp.zeros((bl, bl), BF16)
        p = jnp.concatenate([jnp.concatenate([p[:bl, :3 * bl], zeros, p[:bl, 3 * bl:]], axis=1),
                             jnp.concatenate([zeros, p[bl:, :3 * bl], p[bl:, 3 * bl:]], axis=1)], axis=0)
        pv = _dot(p, v_ones)
        denom = pv[:, HEAD_DIM:] + jnp.exp2(sink - m)
        return pv[:, :HEAD_DIM] * (1.0 / denom)

    pending = scores(0)
    for g in range(groups):
        s = pending
        if g + 1 < groups:
            pending = scores(g + 1)
        o_ref[:, g * HEAD_DIM:(g + 1) * HEAD_DIM] = attend(s, sink_ref[kvh * groups + g]).astype(BF16)


def _attention(z_l, z_c, sink, *, batch, seq, ctx_len, d_model, kv_width):
    bl = ATTN_BLOCK
    nb = seq // bl
    assert nb % 2 == 0
    npairs = nb // 2
    n_kv = kv_width // HEAD_DIM
    groups = d_model // kv_width
    gw = groups * HEAD_DIM
    kcol, vcol = d_model // HEAD_DIM, (d_model + kv_width) // HEAD_DIM

    def kv_spec(col0, shift):
        def imap(b, h, n):
            return (b * nb + jnp.clip(2 * n + shift, 0, nb - 1), col0 + h)
        return pl.BlockSpec((bl, HEAD_DIM), imap)

    ctx_k = pl.BlockSpec((ctx_len, HEAD_DIM), lambda b, h, n: (b, kcol + h))
    ctx_v = pl.BlockSpec((ctx_len, HEAD_DIM), lambda b, h, n: (b, vcol + h))
    blocks = (2 * _nbytes((2 * bl, gw), BF16) + 8 * _nbytes((bl, HEAD_DIM), BF16)
              + 2 * _nbytes((ctx_len, HEAD_DIM), BF16))
    temps = 8 * _nbytes((2 * bl, 4 * bl + ctx_len), F32)
    return pl.pallas_call(
        functools.partial(_attn_kernel, npairs=npairs, groups=groups),
        grid=(batch, n_kv, npairs),
        in_specs=[pl.BlockSpec(memory_space=pltpu.SMEM),
                  pl.BlockSpec((2 * bl, gw), lambda b, h, n: (b * npairs + n, h))]
                 + [kv_spec(kcol, shift) for shift in (-1, 0, 1, 2)]
                 + [kv_spec(vcol, shift) for shift in (-1, 0, 1, 2)]
                 + [ctx_k, ctx_v],
        out_specs=pl.BlockSpec((2 * bl, gw), lambda b, h, n: (b * npairs + n, h)),
        out_shape=jax.ShapeDtypeStruct((batch * seq, d_model), BF16),
        compiler_params=_params(("parallel", "parallel", "arbitrary"), blocks, temps),
        name="window_attention",
    )(sink, *([z_l] * 9), z_c, z_c)


def _attn_ctx_kernel(sink_ref, q_ref, kx_ref, vx_ref, o_ref, *, groups):
    kvh = pl.program_id(1)
    rows = q_ref.shape[0]
    q = _stack_heads(q_ref[...], groups)
    s = _dot_nt(q, kx_ref[...]) * (HEAD_DIM ** -0.5)
    o = _softmax_pv(s, _sink_col(sink_ref, kvh, groups, rows), vx_ref[...])
    o_ref[...] = _unstack_heads(o, groups, rows).astype(BF16)


def _attention_ctx(qkv_c, sink, *, batch, ctx_len, d_model, kv_width):
    n_kv = kv_width // HEAD_DIM
    groups = d_model // kv_width
    gw = groups * HEAD_DIM
    kcol, vcol = d_model // HEAD_DIM, (d_model + kv_width) // HEAD_DIM
    blocks = 2 * _nbytes((ctx_len, gw), BF16) + 2 * _nbytes((ctx_len, HEAD_DIM), BF16)
    temps = 6 * _nbytes((groups * ctx_len, ctx_len), F32)
    return pl.pallas_call(
        functools.partial(_attn_ctx_kernel, groups=groups),
        grid=(batch, n_kv),
        in_specs=[pl.BlockSpec(memory_space=pltpu.SMEM),
                  pl.BlockSpec((ctx_len, gw), lambda b, h: (b, h)),
                  pl.BlockSpec((ctx_len, HEAD_DIM), lambda b, h: (b, kcol + h)),
                  pl.BlockSpec((ctx_len, HEAD_DIM), lambda b, h: (b, vcol + h))],
        out_specs=pl.BlockSpec((ctx_len, gw), lambda b, h: (b, h)),
        out_shape=jax.ShapeDtypeStruct((batch * ctx_len, d_model), BF16),
        compiler_params=_params(("parallel", "parallel"), blocks, temps),
        name="context_attention",
    )(sink, qkv_c, qkv_c, qkv_c)


CONV_PAD = SUBLANES
CONV_ROWS = 256


def _conv_seq(x_ref, w, o_ref, pad_ref, l2_flag, l2_scale):
    s = x_ref.shape[0]
    zeros = jnp.zeros((CONV_PAD, LANES), F32)
    pad_ref[pl.ds(0, CONV_PAD), :] = zeros
    pad_ref[pl.ds(CONV_PAD, s), :] = x_ref[...].astype(F32)
    pad_ref[pl.ds(CONV_PAD + s, CONV_PAD), :] = zeros
    rows = min(CONV_ROWS, s)

    def conv_silu(r0):
        y = jnp.zeros((rows, LANES), F32)
        for k in range(CONV_K):
            off = CONV_PAD - CONV_K // 2 + k
            y = y + pad_ref[pl.ds(r0 + off, rows), :] * w[k:k + 1, :]
        return _silu(y)

    def body_l2(it, carry):
        r0 = pl.multiple_of(it * rows, rows)
        y = conv_silu(r0)
        inv = lax.rsqrt(jnp.sum(y * y, axis=-1, keepdims=True) + EPS) * l2_scale
        o_ref[pl.ds(r0, rows), :] = (y * inv).astype(BF16)
        return carry

    def body_plain(it, carry):
        r0 = pl.multiple_of(it * rows, rows)
        o_ref[pl.ds(r0, rows), :] = conv_silu(r0).astype(BF16)
        return carry

    trips = s // rows
    unroll = 2 if trips % 2 == 0 else 1

    @pl.when(l2_flag)
    def _():
        lax.fori_loop(0, trips, body_l2, 0, unroll=unroll)

    @pl.when(jnp.logical_not(l2_flag))
    def _():
        lax.fori_loop(0, trips, body_plain, 0, unroll=unroll)


def _dnprep_kernel(xl_ref, xc_ref, w_ref, ol_ref, oc_ref, padl_ref, padc_ref, *, n_q):
    col = pl.program_id(1)
    l2_flag = col < 2 * n_q
    l2_scale = jnp.where(col < n_q, HEAD_DIM ** -0.5, 1.0)
    w = w_ref[...]
    _conv_seq(xl_ref, w, ol_ref, padl_ref, l2_flag, l2_scale)
    _conv_seq(xc_ref, w, oc_ref, padc_ref, l2_flag, l2_scale)


def _dn_prep(z_l, z_c, conv_w, *, col0, batch, seq, ctx_len, d_model):
    ncol = 3 * d_model // LANES
    blocks = 2 * (_nbytes((seq, LANES), BF16) + _nbytes((ctx_len, LANES), BF16)) + _nbytes((8, LANES), F32)
    scratch = _nbytes((seq + 2 * CONV_PAD, LANES), F32) + _nbytes((ctx_len + 2 * CONV_PAD, LANES), F32)
    return pl.pallas_call(
        functools.partial(_dnprep_kernel, n_q=d_model // LANES),
        grid=(batch, ncol),
        in_specs=[pl.BlockSpec((seq, LANES), lambda b, c: (b, col0 + c)),
                  pl.BlockSpec((ctx_len, LANES), lambda b, c: (b, col0 + c)),
                  pl.BlockSpec((CONV_K, LANES), lambda b, c: (0, c))],
        out_specs=[pl.BlockSpec((seq, LANES), lambda b, c: (b, c)),
                   pl.BlockSpec((ctx_len, LANES), lambda b, c: (b, c))],
        out_shape=[jax.ShapeDtypeStruct((batch * seq, 3 * d_model), BF16),
                   jax.ShapeDtypeStruct((batch * ctx_len, 3 * d_model), BF16)],
        scratch_shapes=[pltpu.VMEM((seq + 2 * CONV_PAD, LANES), F32),
                        pltpu.VMEM((ctx_len + 2 * CONV_PAD, LANES), F32)],
        compiler_params=_params(("parallel", "parallel"), blocks, scratch + 16 * _nbytes((CONV_ROWS, LANES), F32)),
        name="deltanet_prep",
    )(z_l, z_c, conv_w)


def _gates_kernel(ba_ref, alog_ref, dtb_ref, o_ref, *, n_heads):
    tm = ba_ref.shape[0]
    x = ba_ref[...]
    lane = lax.broadcasted_iota(jnp.int32, (CHUNK, LANES), 1)
    z = x + dtb_ref[...]
    softplus = jnp.maximum(z, 0.0) + jnp.log(1.0 + jnp.exp(-jnp.abs(z)))
    g = -jnp.exp(alog_ref[...]) * softplus
    beta = _sigmoid(x)
    r = lax.broadcasted_iota(jnp.int32, (CHUNK, CHUNK), 0)
    c = lax.broadcasted_iota(jnp.int32, (CHUNK, CHUNK), 1)
    tri_prefix = (c <= r).astype(BF16)
    tri_suffix = (c >= r).astype(BF16)
    for i in range(tm // CHUNK):
        rows = slice(i * CHUNK, (i + 1) * CHUNK)
        parts = _split3(g[rows])
        pre = sum(_dot(tri_prefix, p) for p in parts)
        suf = sum(_dot(tri_suffix, p) for p in parts)
        csum = jnp.where(lane < 3 * n_heads, pre, suf)
        o_ref[rows, :] = jnp.where(lane < 2 * n_heads, beta[rows], csum)


def _gates(ba, alog_vec, dtb_vec, *, n_heads):
    m = ba.shape[0]
    tm = _pick(m, (1024, 512, 256, 128))
    blocks = 2 * _nbytes((tm, LANES), F32)
    return pl.pallas_call(
        functools.partial(_gates_kernel, n_heads=n_heads),
        grid=(m // tm,),
        in_specs=[pl.BlockSpec((tm, LANES), lambda i: (i, 0)),
                  pl.BlockSpec((1, LANES), lambda i: (0, 0)),
                  pl.BlockSpec((1, LANES), lambda i: (0, 0))],
        out_specs=pl.BlockSpec((tm, LANES), lambda i: (i, 0)),
        out_shape=jax.ShapeDtypeStruct((m, LANES), F32),
        compiler_params=_params(("parallel",), blocks, 8 * _nbytes((tm, LANES), F32)),
        name="deltanet_gates",
    )(ba, alog_vec, dtb_vec)


def _dot_split(a, b):
    n = b[0].shape[1]
    lhs = jnp.concatenate(a, axis=1)
    half = jnp.concatenate(b, axis=1)
    out = _dot(lhs, jnp.concatenate([half, half], axis=0))
    return out[:, :n] + out[:, n:]


def _unit_tri_inverse_many(mats, hooks=()):
    n = mats[0].shape[0]
    row = lax.broadcasted_iota(jnp.int32, (n, n), 0)
    col = lax.broadcasted_iota(jnp.int32, (n, n), 1)
    splits = [_split2(a) for a in mats]
    eye = jnp.where(row == col, 1.0, 0.0)
    pair = (row // 2) == (col // 2)
    xs = [eye - jnp.where(pair, a, 0.0) for a in mats]
    s, level = 2, 0
    while s < n:
        couple = ((row // (2 * s)) == (col // (2 * s))) & ((row // s) != (col // s))
        zero16 = jnp.zeros((n, n), BF16)
        es = [(jnp.where(couple, hi, zero16), jnp.where(couple, lo, zero16)) for hi, lo in splits]
        xsp = [_split2(x) for x in xs]
        exs = [_split2(_dot_split(e, xp)) for e, xp in zip(es, xsp)]
        xs = [x - _dot_split(xp, ex) for x, xp, ex in zip(xs, xsp, exs)]
        if level < len(hooks):
            hooks[level]()
        s *= 2
        level += 1
    return xs


def _scan_prep_group(refs, chunks, dsts, sel, scr, hooks=()):
    u_s, w_s, qk_s, qd_s, kdt_s, gl_s = scr
    c = CHUNK
    group = len(chunks)
    per_chunk = refs if isinstance(refs, list) else [refs] * group
    hooks = list(hooks)
    if hooks:
        hooks.pop(0)()
    row = lax.broadcasted_iota(jnp.int32, (c, c), 0)
    col = lax.broadcasted_iota(jnp.int32, (c, c), 1)
    eye16 = (row == col).astype(BF16)
    r0s = [pl.multiple_of(ci * c, c) for ci in chunks]
    q16s = [r[0][pl.ds(r0, c), :] for r, r0 in zip(per_chunk, r0s)]
    k16s = [r[1][pl.ds(r0, c), :] for r, r0 in zip(per_chunk, r0s)]
    qs = [q.astype(F32) for q in q16s]
    ks = [k.astype(F32) for k in k16s]
    vs = [r[2][pl.ds(r0, c), :].astype(F32) for r, r0 in zip(per_chunk, r0s)]
    sel2 = jnp.concatenate([sel, sel], axis=0)
    parts = [_split3(r[3][pl.ds(r0, c), :]) for r, r0 in zip(per_chunk, r0s)]
    gates = [_dot(jnp.concatenate([hi, mid], axis=1), sel2) + _dot(lo, sel) for hi, mid, lo in parts]
    kks = [_dot_nt(k16, k16) for k16 in k16s]
    qks = [_dot_nt(q16, k16) for q16, k16 in zip(q16s, k16s)]
    probs = [(g, d) for g in range(group) for d in range(2)]
    betas = [gates[g][:, d * LANES:(d + 1) * LANES] for g, d in probs]
    ccols = [gates[g][:, (2 + d) * LANES:(3 + d) * LANES] for g, d in probs]
    decays, mats = [], []
    for (g, d), beta, ccol in zip(probs, betas, ccols):
        crow = per_chunk[g][4][d][0, :, pl.ds(r0s[g], c)]
        incl = (row >= col) if d == 0 else (row <= col)
        strict = (row > col) if d == 0 else (row < col)
        decay = jnp.exp(jnp.where(incl, ccol - crow, NEG_INF))
        decays.append(decay)
        mats.append(jnp.where(strict, beta * kks[g] * decay, 0.0))
    if hooks:
        hooks.pop(0)()
    xsp = [_split2(x) for x in _unit_tri_inverse_many(mats, hooks)]
    expcs = [jnp.exp(ccol) for ccol in ccols]
    rus = [_split2(vs[g] * beta) for (g, d), beta in zip(probs, betas)]
    us = [_dot_split(xp, ru) for xp, ru in zip(xsp, rus)]
    rws = [(ks[g] * beta * expc).astype(BF16) for (g, d), beta, expc in zip(probs, betas, expcs)]
    ws = [_dot(jnp.concatenate(xp, axis=1), jnp.concatenate([rw, rw], axis=0)) for xp, rw in zip(xsp, rws)]
    clasts = [ccol[c - 1:c, :] if d == 0 else ccol[0:1, :] for (g, d), ccol in zip(probs, ccols)]
    kdts = [_dot_nt(eye16, (ks[g] * jnp.exp(clast - ccol)).astype(BF16))
            for (g, d), clast, ccol in zip(probs, clasts, ccols)]
    for i, (g, d) in enumerate(probs):
        rows = pl.ds(pl.multiple_of(dsts[g] * c, c), c)
        u_s[d, rows, :] = us[i]
        w_s[d, rows, :] = ws[i].astype(BF16)
        qk_s[d, rows, :] = (qks[g] * decays[i]).astype(BF16)
        qd_s[d, rows, :] = (qs[g] * expcs[i]).astype(BF16)
        kdt_s[d, rows, :] = kdts[i].astype(BF16)
        gl_s[d, pl.ds(pl.multiple_of(dsts[g] * SUBLANES, SUBLANES), SUBLANES), :] = jnp.broadcast_to(
            jnp.exp(clasts[i]), (SUBLANES, LANES))


def _step_issue(chunks, states, scr):
    u_s, w_s, qk_s, qd_s, kdt_s, gl_s = scr
    c = CHUNK
    rows = [pl.ds(pl.multiple_of(ci * c, c), c) for ci in chunks]
    dirs = range(2)
    ws = [_dot(jnp.concatenate([w_s[d, rows[d], :], qd_s[d, rows[d], :]], axis=0), states[d].astype(BF16))
          for d in dirs]
    return dict(chunks=chunks, rows=rows, states=states, ws=ws)


def _step_finish(pending, scr):
    u_s, w_s, qk_s, qd_s, kdt_s, gl_s = scr
    c = CHUNK
    chunks, rows, states, ws = pending["chunks"], pending["rows"], pending["states"], pending["ws"]
    dirs = range(2)
    v_new = [(u_s[d, rows[d], :] - ws[d][:c]).astype(BF16) for d in dirs]
    outs = [ws[d][c:] + _dot(qk_s[d, rows[d], :], v_new[d]) for d in dirs]
    gls = [gl_s[d, pl.ds(pl.multiple_of(chunks[d] * SUBLANES, SUBLANES), SUBLANES), :][0:1, :] for d in dirs]
    new_states = [states[d] * gls[d] + _dot(kdt_s[d, rows[d], :], v_new[d]) for d in dirs]
    return tuple(new_states), outs


def _scan_step_pair(chunks, states, scr):
    return _step_finish(_step_issue(chunks, states, scr), scr)


PREP_GROUP = 4


def _scan_kernel(ql, kl, vl, pl_l, crfl, crbl, zl_prev, qx, kx, vx, px, crfx, crbx, zx, gain_ref,
                 outl_prev_ref, outx_ref, u_s, w_s, qk_s, qd_s, kdt_s, gl_s, o_s, st_s, *, n_heads):
    b, j = pl.program_id(0), pl.program_id(1)
    c = CHUNK
    ncx, ncl = qx.shape[0] // c, ql.shape[0] // c
    half = PREP_GROUP // 2
    n_groups = ncl // PREP_GROUP
    assert PREP_GROUP == 4 and ncx % 2 == 0 and ncl % PREP_GROUP == 0 and ncx <= PREP_GROUP
    cur = j % 2
    factor_refs = (u_s, w_s, qk_s, qd_s, kdt_s, gl_s)
    scr_cur, o_cur = tuple(r.at[cur] for r in factor_refs), o_s.at[cur]
    scr_prev, o_prev = tuple(r.at[1 - cur] for r in factor_refs), o_s.at[1 - cur]
    gain = gain_ref[...]

    def chunk_rows(i):
        return pl.ds(pl.multiple_of(i * c, c), c)

    @pl.when((b == 0) & (j == 0))
    def _():
        def body(i, carry):
            for r in factor_refs[:5]:
                for d in range(2):
                    r[1, d, chunk_rows(i), :] = jnp.zeros((c, LANES), r.dtype)
            o_s[1, chunk_rows(i), :] = jnp.zeros((c, LANES), F32)
            return carry
        lax.fori_loop(0, ncx + ncl, body, 0)
        gl_s[1] = jnp.zeros(gl_s.shape[1:], F32)
        st_s[...] = jnp.zeros(st_s.shape, F32)

    def finish(z_ref, out_ref, i, o):
        on = o * lax.rsqrt(jnp.mean(o * o, axis=-1, keepdims=True) + EPS) * gain
        out_ref[chunk_rows(i), :] = (on * _silu(z_ref[chunk_rows(i), :].astype(F32))).astype(BF16)

    def step_stages(scr, o_ref, base, n, it, box, key, z_ref=None, out_ref=None):
        pending = {}
        jf, jb = it, n - 1 - it

        def issue():
            pending.update(_step_issue((base + jf, base + jb), box[key], scr))

        def complete():
            box[key], outs = _step_finish(pending, scr)
            if z_ref is None:
                o_ref[chunk_rows(base + jf), :] = outs[0]
                o_ref[chunk_rows(base + jb), :] = outs[1]
            else:
                finish(z_ref, out_ref, jf, outs[0] + o_ref[chunk_rows(base + jf), :])
                finish(z_ref, out_ref, jb, outs[1] + o_ref[chunk_rows(base + jb), :])
        return [issue, complete]

    def run(stages):
        for stage in stages:
            stage()

    def front(i):
        return [half * i + k for k in range(half)]

    def drain_stages(p, box):
        if isinstance(p, int) and p < half:
            return step_stages(scr_prev, o_prev, ncx, ncl, front(n_groups - 1)[p], box, "prev")
        return step_stages(scr_prev, o_prev, ncx, ncl, ncl // 2 + p - half, box, "prev", zl_prev, outl_prev_ref)

    n_drain = half + ncl // 2

    @pl.when(j < n_heads)
    def _():
        r = lax.broadcasted_iota(jnp.int32, (LANES, 4 * LANES), 0)
        t = lax.broadcasted_iota(jnp.int32, (LANES, 4 * LANES), 1) // LANES
        sel = (r == t * n_heads + j).astype(BF16)
        zero = jnp.zeros((HEAD_DIM, HEAD_DIM), F32)
        box = {"cur": (zero, zero), "prev": (st_s[0], st_s[1])}

        refs_x = (qx, kx, vx, px, (crfx, crbx))
        refs_l = (ql, kl, vl, pl_l, (crfl, crbl))

        def latent_chunks(i):
            return front(i) + [ncl - 1 - ci for ci in reversed(front(i))]

        def prep_latent(i, hooks):
            chunks = latent_chunks(i)
            _scan_prep_group(refs_l, chunks, [ncx + ci for ci in chunks], sel, scr_cur, hooks)

        first = latent_chunks(0)
        _scan_prep_group([refs_x] * ncx + [refs_l] * len(first), list(range(ncx)) + first,
                         list(range(ncx)) + [ncx + ci for ci in first], sel, scr_cur,
                         [stage for p in range(2 * half) for stage in drain_stages(p, box)])
        for it in range(ncx):
            ends = (zx, outx_ref) if it >= ncx // 2 else ()
            run(step_stages(scr_cur, o_cur, 0, ncx, it, box, "cur", *ends))

        def overlapped(i, carry):
            box = {"cur": carry[:2], "prev": carry[2:]}
            hooks = []
            for k, it in enumerate(front(i - 1)):
                own = step_stages(scr_cur, o_cur, ncx, ncl, it, box, "cur")
                other = drain_stages(2 * half + half * (i - 1) + k, box)
                hooks += [own[0], other[0], own[1], other[1]]
            prep_latent(i, hooks)
            return tuple(box["cur"]) + tuple(box["prev"])

        carry = lax.fori_loop(1, n_groups, overlapped, tuple(box["cur"]) + tuple(box["prev"]))
        st_s[0] = carry[0]
        st_s[1] = carry[1]

    @pl.when(j == n_heads)
    def _():
        box = {"prev": (st_s[0], st_s[1])}
        for p in range(half):
            run(drain_stages(p, box))

        def body(p, states):
            inner = {"prev": states}
            run(drain_stages(p, inner))
            return tuple(inner["prev"])
        lax.fori_loop(half, n_drain, body, tuple(box["prev"]))


def _delta_scan(dn_l, dn_c, gates_l, gates_c, gates_lt, gates_ct, z_l, z_c, gain,
                *, zcol, batch, seq, ctx_len, d_model):
    n_heads = d_model // HEAD_DIM
    t = seq + ctx_len
    last = n_heads - 1

    def col_spec(rows, col0, lag=0):
        return pl.BlockSpec((rows, LANES), lambda b, j: (b, col0 + jnp.clip(j - lag, 0, last)))

    def gate_spec(rows):
        return pl.BlockSpec((rows, LANES), lambda b, j: (b, 0))

    def crow_spec(rows, d):
        return pl.BlockSpec((1, 1, rows), lambda b, j: ((2 + d) * n_heads + jnp.minimum(j, last), 0, b))

    def side(rows, z_lag):
        return [col_spec(rows, 0), col_spec(rows, n_heads), col_spec(rows, 2 * n_heads), gate_spec(rows),
                crow_spec(rows, 0), crow_spec(rows, 1), col_spec(rows, zcol, z_lag)]

    blocks = (_nbytes((t, LANES), F32) + 2 * _nbytes((SUBLANES, t), F32) + 5 * _nbytes((t, LANES), BF16))
    slots = 2
    scratch_shapes = [pltpu.VMEM((slots, 2, t, LANES), F32),
                      pltpu.VMEM((slots, 2, t, LANES), BF16),
                      pltpu.VMEM((slots, 2, t, LANES), BF16),
                      pltpu.VMEM((slots, 2, t, LANES), BF16),
                      pltpu.VMEM((slots, 2, t, LANES), BF16),
                      pltpu.VMEM((slots, 2, (t // CHUNK) * SUBLANES, LANES), F32),
                      pltpu.VMEM((slots, t, LANES), F32),
                      pltpu.VMEM((2, HEAD_DIM, HEAD_DIM), F32)]
    scratch = slots * (3 * _nbytes((t, LANES), F32) + 4 * 2 * _nbytes((t, LANES), BF16))
    return pl.pallas_call(
        functools.partial(_scan_kernel, n_heads=n_heads),
        grid=(batch, n_heads + 1),
        in_specs=side(seq, 1) + side(ctx_len, 0) + [pl.BlockSpec((1, LANES), lambda b, j: (0, 0))],
        out_specs=[pl.BlockSpec((seq, LANES), lambda b, j: (b, jnp.clip(j - 1, 0, last))),
                   pl.BlockSpec((ctx_len, LANES), lambda b, j: (b, jnp.minimum(j, last)))],
        out_shape=[jax.ShapeDtypeStruct((batch * seq, d_model), BF16),
                   jax.ShapeDtypeStruct((batch * ctx_len, d_model), BF16)],
        scratch_shapes=scratch_shapes,
        compiler_params=_params(("arbitrary", "arbitrary"), blocks, scratch + 4 * 1024 * 1024),
        name="delta_scan",
    )(dn_l, dn_l, dn_l, gates_l, gates_lt, gates_lt, z_l,
      dn_c, dn_c, dn_c, gates_c, gates_ct, gates_ct, z_c, gain)


def _merge_kernel(att_ref, dn_ref, ga_ref, gd_ref, wa_ref, wd_ref, o_ref):
    y = (_sigmoid(ga_ref[...].astype(F32)) * _dot(att_ref[...], wa_ref[...])
         + _sigmoid(gd_ref[...].astype(F32)) * _dot(dn_ref[...], wd_ref[...]))
    o_ref[...] = y.astype(BF16)


def _merge(att, dn, z, w_attn, w_delta, layer, *, gate_col, tm, tn):
    m, d = att.shape
    ga0, gd0 = gate_col // tn, (gate_col + d) // tn
    blocks = (2 * _nbytes((tm, d), BF16) + 2 * _nbytes((tm, tn), BF16) + 2 * _nbytes((d, tn), BF16)
              + _nbytes((tm, tn), BF16))
    return pl.pallas_call(
        _merge_kernel,
        grid=(m // tm, d // tn),
        in_specs=[pl.BlockSpec((tm, d), lambda i, j: (i, 0)),
                  pl.BlockSpec((tm, d), lambda i, j: (i, 0)),
                  pl.BlockSpec((tm, tn), lambda i, j: (i, ga0 + j)),
                  pl.BlockSpec((tm, tn), lambda i, j: (i, gd0 + j)),
                  pl.BlockSpec((None, d, tn), lambda i, j: (layer, 0, j)),
                  pl.BlockSpec((None, d, tn), lambda i, j: (layer, 0, j))],
        out_specs=pl.BlockSpec((tm, tn), lambda i, j: (i, j)),
        out_shape=jax.ShapeDtypeStruct((m, d), BF16),
        compiler_params=_params(("parallel", "arbitrary"), blocks, 4 * _nbytes((tm, tn), F32)),
        name="merge",
    )(att, dn, z, z, w_attn, w_delta)


def _resid_kernel(a_ref, w_ref, x_ref, mod_ref, o_ref, *, gate_row):
    o_ref[...] = x_ref[...] + mod_ref[0, gate_row:gate_row + 1, :] * _dot(a_ref[...], w_ref[...])


def _resid_matmul(a, w, layer, x, mods, mod_row, gate_row, *, tm, tn, name):
    m, k = a.shape
    d = w.shape[2]
    blocks = (_nbytes((tm, k), BF16) + _nbytes((k, tn), BF16) + 2 * _nbytes((tm, tn), F32)
              + _nbytes((6, tn), F32))
    return pl.pallas_call(
        functools.partial(_resid_kernel, gate_row=gate_row),
        grid=(m // tm, d // tn),
        in_specs=[pl.BlockSpec((tm, k), lambda i, j: (i, 0)),
                  pl.BlockSpec((None, k, tn), lambda i, j: (layer, 0, j)),
                  pl.BlockSpec((tm, tn), lambda i, j: (i, j)),
                  pl.BlockSpec((1, 6, tn), lambda i, j: (mod_row(i, tm), 0, j))],
        out_specs=pl.BlockSpec((tm, tn), lambda i, j: (i, j)),
        out_shape=jax.ShapeDtypeStruct((m, d), F32),
        compiler_params=_params(("parallel", "arbitrary"), blocks, 2 * _nbytes((tm, tn), F32)),
        name=name,
    )(a, w, x, mods)


def _ffn_up_kernel(x_ref, mod_ref, g_ref, wg_ref, wu_ref, o_ref, h_ref):
    j = pl.program_id(1)
    tm = o_ref.shape[0]
    sub = min(tm, INPROJ_SUB_ROWS)

    def tile(fresh):
        for r in range(tm // sub):
            rows = pl.ds(r * sub, sub)
            if fresh:
                h = _norm_mod(x_ref[rows, :], g_ref[...], mod_ref[0, 3:4, :], mod_ref[0, 4:5, :]).astype(BF16)
                h_ref[rows, :] = h
            else:
                h = h_ref[rows, :]
            o_ref[rows, :] = (_silu(_dot(h, wg_ref[...])) * _dot(h, wu_ref[...])).astype(BF16)

    @pl.when(j == 0)
    def _():
        tile(True)

    @pl.when(j > 0)
    def _():
        tile(False)


def _ffn_up(x, mods, mod_row, gain, w_gate_up, layer, *, tm, tn):
    m, d = x.shape
    f = w_gate_up.shape[2] // 2
    nj = f // tn
    blocks = (_nbytes((tm, d), F32) + _nbytes((6, d), F32) + 2 * _nbytes((d, tn), BF16)
              + _nbytes((tm, tn), BF16))
    return pl.pallas_call(
        _ffn_up_kernel,
        grid=(m // tm, nj),
        in_specs=[pl.BlockSpec((tm, d), lambda i, j: (i, 0)),
                  pl.BlockSpec((1, 6, d), lambda i, j: (mod_row(i, tm), 0, 0)),
                  pl.BlockSpec((1, d), lambda i, j: (0, 0)),
                  pl.BlockSpec((None, d, tn), lambda i, j: (layer, 0, j)),
                  pl.BlockSpec((None, d, tn), lambda i, j: (layer, 0, nj + j))],
        out_specs=pl.BlockSpec((tm, tn), lambda i, j: (i, j)),
        out_shape=jax.ShapeDtypeStruct((m, f), BF16),
        scratch_shapes=[pltpu.VMEM((tm, d), BF16)],
        compiler_params=_params(("parallel", "arbitrary"), blocks,
                                _nbytes((tm, d), BF16) + 4 * _nbytes((tm, tn), F32)),
        name="ffn_up",
    )(x, mods, gain, w_gate_up, w_gate_up)


def _final_norm_kernel(x_ref, g_ref, o_ref):
    x = x_ref[...]
    o_ref[...] = x * lax.rsqrt(jnp.mean(x * x, axis=-1, keepdims=True) + EPS) * g_ref[...]


def _final_norm(x, gain, *, tm):
    m, d = x.shape
    return pl.pallas_call(
        _final_norm_kernel,
        grid=(m // tm,),
        in_specs=[pl.BlockSpec((tm, d), lambda i: (i, 0)), pl.BlockSpec((1, d), lambda i: (0, 0))],
        out_specs=pl.BlockSpec((tm, d), lambda i: (i, 0)),
        out_shape=jax.ShapeDtypeStruct((m, d), F32),
        compiler_params=_params(("parallel",), 2 * _nbytes((tm, d), F32), 2 * _nbytes((tm, d), F32)),
        name="final_norm",
    )(x, gain)


def _rope_tables(seq):
    quarter = HEAD_DIM // 4
    pos = jnp.arange(seq)
    row = (pos // GRID_W).astype(F32)
    col = (pos % GRID_W).astype(F32)
    inv_freq = ROPE_BASE ** (-jnp.arange(quarter, dtype=F32) / quarter)
    ang_r, ang_c = row[:, None] * inv_freq[None], col[:, None] * inv_freq[None]
    cos = jnp.concatenate([jnp.cos(ang_r), jnp.cos(ang_c)] * 2, axis=1)
    sin = jnp.concatenate([-jnp.sin(ang_r), -jnp.sin(ang_c), jnp.sin(ang_r), jnp.sin(ang_c)], axis=1)
    return cos, sin


def _rope_perm():
    quarter = HEAD_DIM // 4
    order = (0, 2, 1, 3)
    return jnp.concatenate([jnp.arange(quarter) + quarter * o for o in order])


def kernel(x, c, ctx, c_ctx, w_mod, b_mod, norm1, norm2, w_in, conv_w, attn_sink, a_log, dt_bias, dn_norm,
           w_attn_proj, w_delta_proj, w_out, w_gate_up, w_down, final_norm):
    batch, seq, d = x.shape
    ctx_len = ctx.shape[1]
    depth = w_mod.shape[0]
    n_heads = d // HEAD_DIM
    kv_width = (n_heads // 4) * HEAD_DIM
    qkv_width = d + 2 * kv_width
    assert seq % CHUNK == 0 and ctx_len % CHUNK == 0 and seq % ATTN_BLOCK == 0
    assert 4 * n_heads <= LANES and (batch * seq) % ctx_len == 0

    off_ba = qkv_width + 4 * d
    qk_width = d + kv_width
    w_qk = w_in[:, :, :qk_width].reshape(depth, d, qk_width // HEAD_DIM, HEAD_DIM)[..., _rope_perm()]
    w_main = jnp.concatenate([w_qk.reshape(depth, d, qk_width), w_in[:, :, qk_width:off_ba],
                              w_in[:, :, off_ba + 4 * n_heads:]], axis=2).astype(BF16)
    w_ba = jnp.pad(w_in[:, :, off_ba:off_ba + 4 * n_heads], ((0, 0), (0, 0), (0, LANES - 4 * n_heads))).astype(BF16)
    w_attn16, w_delta16, w_out16 = w_attn_proj.astype(BF16), w_delta_proj.astype(BF16), w_out.astype(BF16)
    w_gu16, w_down16 = w_gate_up.astype(BF16), w_down.astype(BF16)

    lane_pad = ((0, 0), (2 * n_heads, LANES - 4 * n_heads))
    alog_vec = jnp.pad(a_log.reshape(depth, 2 * n_heads), lane_pad).reshape(depth, 1, LANES)
    dtb_vec = jnp.pad(dt_bias.reshape(depth, 2 * n_heads), lane_pad).reshape(depth, 1, LANES)

    cos_l, sin_l = _rope_tables(seq)
    cos_l, sin_l = jnp.tile(cos_l, (batch, 1)), jnp.tile(sin_l, (batch, 1))
    cos_c = jnp.ones((batch * ctx_len, LANES), F32)
    sin_c = jnp.zeros((batch * ctx_len, LANES), F32)

    mod_rows = ((batch + 1 + SUBLANES - 1) // SUBLANES) * SUBLANES
    cc = jnp.concatenate([c, c_ctx[None], jnp.zeros((mod_rows - batch - 1, d), F32)], axis=0)
    mods_all = _modulation(cc, w_mod, b_mod).reshape(depth, mod_rows, 6, d)

    tm_l = _pick(seq, (1024, 512, 256, 128))
    tm_c = _pick(batch * ctx_len, (1024, 512, 256, 128))
    tn = _pick(kv_width, (512, 256, 128))
    tn_in = _pick(w_main.shape[2], (1024, 512, 256, 128))

    def row_l(i, tm):
        return (i * tm) // seq

    def row_c(i, tm):
        return batch

    xl = x.reshape(batch * seq, d)
    xc = ctx.reshape(batch * ctx_len, d)
    dims = dict(batch=batch, seq=seq, ctx_len=ctx_len, d_model=d)

    for l in range(depth):
        mods = mods_all[l]
        need_ctx = l < depth - 1
        g1, g2 = norm1[l][None], norm2[l][None]
        proj = functools.partial(_inproj, gain=g1, w_main=w_main, w_ba=w_ba, layer=l, tn=tn_in,
                                 qk_width=d + kv_width)
        z_l, ba_l = proj(xl, mods, row_l, cos=cos_l, sin=sin_l, tm=tm_l)
        z_c, ba_c = proj(xc, mods, row_c, cos=cos_c, sin=sin_c, tm=tm_c)

        att_l = _attention(z_l, z_c, attn_sink[l], kv_width=kv_width, **dims)

        dn_l, dn_c = _dn_prep(z_l, z_c, conv_w[l], col0=qkv_width // LANES, **dims)
        gates_l = _gates(ba_l, alog_vec[l], dtb_vec[l], n_heads=n_heads)
        gates_c = _gates(ba_c, alog_vec[l], dtb_vec[l], n_heads=n_heads)
        gates_lt = gates_l.T.reshape(LANES, 1, batch * seq)
        gates_ct = gates_c.T.reshape(LANES, 1, batch * ctx_len)
        do_l, do_c = _delta_scan(dn_l, dn_c, gates_l, gates_c, gates_lt, gates_ct, z_l, z_c,
                                 dn_norm[l][None], zcol=(qkv_width + 3 * d) // LANES, **dims)

        streams = [(xl, att_l, do_l, z_l, row_l, tm_l)]
        if need_ctx:
            att_c = _attention_ctx(z_c, attn_sink[l], batch=batch, ctx_len=ctx_len, d_model=d, kv_width=kv_width)
            streams.append((xc, att_c, do_c, z_c, row_c, tm_c))
        outs = []
        for xs, att, dn, z, row, tm in streams:
            y = _merge(att, dn, z, w_attn16, w_delta16, l, gate_col=qkv_width + 4 * d, tm=tm, tn=tn)
            xs = _resid_matmul(y, w_out16, l, xs, mods, row, 2, tm=tm, tn=tn, name="out_proj")
            act = _ffn_up(xs, mods, row, g2, w_gu16, l, tm=tm, tn=tn)
            xs = _resid_matmul(act, w_down16, l, xs, mods, row, 5, tm=tm, tn=tn, name="ffn_down")
            outs.append(xs)
        xl = outs[0]
        if need_ctx:
            xc = outs[1]

    return _final_norm(xl, final_norm[None], tm=tm_l).reshape(batch, seq, d)
```

```python
import functools

import jax
import jax.numpy as jnp
from jax import lax
from jax.experimental import pallas as pl
from jax.experimental.pallas import tpu as pltpu

F32 = jnp.float32
BF16 = jnp.bfloat16

HEAD_DIM = 128
WINDOW = 128
ATTN_BLOCK = 128
GRID_W = 64
CONV_K = 5
ROPE_BASE = 10000.0
EPS = 1e-6
NEG_INF = -1e30
LOG2_E = 1.4426950408889634
CHUNK = 128
LANES = 128
SUBLANES = 8
VMEM_CAP_BYTES = 60 * 1024 * 1024


def _vmem_limit(block_bytes, scratch_bytes=0):
    est = 2 * block_bytes + scratch_bytes
    return int(min(VMEM_CAP_BYTES, max(32 * 1024 * 1024, est * 3 // 2)))


def _params(sem, block_bytes, scratch_bytes=0):
    return pltpu.CompilerParams(dimension_semantics=sem,
                                vmem_limit_bytes=_vmem_limit(block_bytes, scratch_bytes))


def _pick(n, candidates):
    for c in candidates:
        if n % c == 0:
            return c
    raise ValueError(f"no tile in {candidates} divides {n}")


def _nbytes(shape, dtype):
    n = 1
    for s in shape:
        n *= s
    return n * jnp.dtype(dtype).itemsize


def _sigmoid(x):
    return 1.0 / (1.0 + jnp.exp(-x))


def _silu(x):
    return x * _sigmoid(x)


def _dot(a, b):
    return jnp.dot(a, b, preferred_element_type=F32)


def _dot_nt(a, b):
    return lax.dot_general(a, b, (((1,), (1,)), ((), ())), preferred_element_type=F32)


def _split2(x):
    hi = x.astype(BF16)
    lo = (x - hi.astype(F32)).astype(BF16)
    return hi, lo


def _split3(x):
    hi = x.astype(BF16)
    r = x - hi.astype(F32)
    mid = r.astype(BF16)
    lo = (r - mid.astype(F32)).astype(BF16)
    return hi, mid, lo


def _mod_kernel(c_ref, w_ref, b_ref, o_ref):
    s = _silu(c_ref[...])
    s_hi, s_lo = _split2(s)
    w = w_ref[0]
    w_hi, w_lo = _split2(w)
    acc = _dot(s_hi, w_hi) + _dot(s_hi, w_lo) + _dot(s_lo, w_hi)
    o_ref[0] = acc + b_ref[0]


def _modulation(cc, w_mod, b_mod):
    depth, d, n = w_mod.shape
    rows = cc.shape[0]
    tn = _pick(n, (512, 256, 128))
    blocks = _nbytes((rows, d), F32) + _nbytes((d, tn), F32) + _nbytes((1, tn), F32) + _nbytes((rows, tn), F32)
    return pl.pallas_call(
        _mod_kernel,
        grid=(depth, n // tn),
        in_specs=[pl.BlockSpec((rows, d), lambda l, j: (0, 0)),
                  pl.BlockSpec((1, d, tn), lambda l, j: (l, 0, j)),
                  pl.BlockSpec((1, 1, tn), lambda l, j: (l, 0, j))],
        out_specs=pl.BlockSpec((1, rows, tn), lambda l, j: (l, 0, j)),
        out_shape=jax.ShapeDtypeStruct((depth, rows, n), F32),
        compiler_params=_params(("parallel", "parallel"), blocks, 3 * _nbytes((d, tn), F32)),
        name="modulation",
    )(cc, w_mod, b_mod.reshape(depth, 1, n))


INPROJ_SUB_ROWS = 256


def _norm_mod(x, gain, shift, scale):
    y = x * lax.rsqrt(jnp.mean(x * x, axis=-1, keepdims=True) + EPS) * gain
    return y * (1.0 + scale) + shift


def _inproj_kernel(x_ref, mod_ref, g_ref, w_ref, wba_ref, cos_ref, sin_ref, z_ref, ba_ref, h_ref, *, qk_width):
    j = pl.program_id(1)
    tm, tn = z_ref.shape
    full, rem = divmod(qk_width, tn)
    sub = min(tm, INPROJ_SUB_ROWS)

    def rope(t, rows):
        width = t.shape[1]
        cos = jnp.concatenate([cos_ref[rows, :]] * (width // HEAD_DIM), axis=1)
        sin = jnp.concatenate([sin_ref[rows, :]] * (width // HEAD_DIM), axis=1)
        partner = jnp.concatenate([pltpu.roll(t[:, g * HEAD_DIM:(g + 1) * HEAD_DIM], HEAD_DIM // 2, 1)
                                   for g in range(width // HEAD_DIM)], axis=1)
        return t * cos + partner * sin

    def tile(epilogue, fresh=False):
        for r in range(tm // sub):
            rows = pl.ds(r * sub, sub)
            if fresh:
                h = _norm_mod(x_ref[rows, :], g_ref[...], mod_ref[0, 0:1, :], mod_ref[0, 1:2, :]).astype(BF16)
                h_ref[rows, :] = h
                ba_ref[rows, :] = _dot(h, wba_ref[...])
            else:
                h = h_ref[rows, :]
            z_ref[rows, :] = epilogue(_dot(h, w_ref[...]), rows).astype(BF16)

    assert full >= 1

    @pl.when(j == 0)
    def _():
        tile(rope, fresh=True)

    @pl.when((j > 0) & (j < full))
    def _():
        tile(rope)

    if rem:
        @pl.when(j == full)
        def _():
            tile(lambda acc, rows: jnp.concatenate([rope(acc[:, :rem], rows), acc[:, rem:]], axis=1))

    @pl.when(j >= full + (1 if rem else 0))
    def _():
        tile(lambda acc, rows: acc)


def _inproj(x, mods, mod_row, gain, w_main, w_ba, layer, cos, sin, *, tm, tn, qk_width):
    m, d = x.shape
    n = w_main.shape[2]
    blocks = (_nbytes((tm, d), F32) + _nbytes((6, d), F32) + _nbytes((d, tn), BF16) + _nbytes((d, LANES), BF16)
              + 2 * _nbytes((tm, LANES), F32) + _nbytes((tm, tn), BF16) + _nbytes((tm, LANES), F32))
    return pl.pallas_call(
        functools.partial(_inproj_kernel, qk_width=qk_width),
        grid=(m // tm, n // tn),
        in_specs=[pl.BlockSpec((tm, d), lambda i, j: (i, 0)),
                  pl.BlockSpec((1, 6, d), lambda i, j: (mod_row(i, tm), 0, 0)),
                  pl.BlockSpec((1, d), lambda i, j: (0, 0)),
                  pl.BlockSpec((None, d, tn), lambda i, j: (layer, 0, j)),
                  pl.BlockSpec((None, d, LANES), lambda i, j: (layer, 0, 0)),
                  pl.BlockSpec((tm, LANES), lambda i, j: (i, 0)),
                  pl.BlockSpec((tm, LANES), lambda i, j: (i, 0))],
        out_specs=[pl.BlockSpec((tm, tn), lambda i, j: (i, j)),
                   pl.BlockSpec((tm, LANES), lambda i, j: (i, 0))],
        out_shape=[jax.ShapeDtypeStruct((m, n), BF16),
                   jax.ShapeDtypeStruct((m, LANES), F32)],
        scratch_shapes=[pltpu.VMEM((tm, d), BF16)],
        compiler_params=_params(("parallel", "arbitrary"), blocks,
                                _nbytes((tm, d), BF16) + 3 * _nbytes((tm, tn), F32)),
        name="inproj",
    )(x, mods, gain, w_main, w_ba, cos, sin)


def _softmax_pv(s, sink_col, v):
    m = jnp.maximum(jnp.max(s, axis=-1, keepdims=True), sink_col)
    p = jnp.exp(s - m)
    denom = jnp.sum(p, axis=-1, keepdims=True) + jnp.exp(sink_col - m)
    return _dot((p * (1.0 / denom)).astype(BF16), v)


def _stack_heads(q, groups):
    return jnp.concatenate([q[:, g * HEAD_DIM:(g + 1) * HEAD_DIM] for g in range(groups)], axis=0)


def _unstack_heads(o, groups, rows):
    return jnp.concatenate([o[g * rows:(g + 1) * rows] for g in range(groups)], axis=1)


def _sink_col(sink_ref, kvh, groups, rows):
    return jnp.concatenate([jnp.full((rows, 1), sink_ref[kvh * groups + g], F32) for g in range(groups)], axis=0)


ATTN_PAIRS = 4


def _attn_kernel(sink_ref, q_ref, *refs, nsteps, groups, pairs):
    nwin = 2 * pairs + 2
    k_blocks, v_blocks = refs[:nwin], refs[nwin:2 * nwin]
    kx_ref, vx_ref, o_ref = refs[2 * nwin:]
    kvh, n = pl.program_id(1), pl.program_id(2)
    bl = ATTN_BLOCK
    row = lax.broadcasted_iota(jnp.int32, (2 * bl, 3 * bl), 0)
    col = lax.broadcasted_iota(jnp.int32, (2 * bl, 3 * bl), 1)
    sub = row // bl
    band = jnp.abs(col - bl - (row % bl)) <= WINDOW
    kx, vx = kx_ref[...], vx_ref[...]

    def operands(p):
        k = jnp.concatenate([r[...] for r in k_blocks[2 * p:2 * p + 4]] + [kx], axis=0)
        v = jnp.concatenate([r[...] for r in v_blocks[2 * p:2 * p + 4]] + [vx], axis=0)
        v_ones = jnp.concatenate([v, jnp.ones(v.shape, BF16)], axis=1)
        valid = band
        if p == 0:
            valid = valid & ((col >= bl) | (sub == 1) | (n > 0))
        if p == pairs - 1:
            valid = valid & ((col < 2 * bl) | (sub == 0) | (n < nsteps - 1))
        return k, v_ones, valid

    ops = [operands(p) for p in range(pairs)]

    def scores(p, g):
        k, _, valid = ops[p]
        q = q_ref[pl.ds(p * 2 * bl, 2 * bl), g * HEAD_DIM:(g + 1) * HEAD_DIM]
        s = _dot_nt(q, k) * (HEAD_DIM ** -0.5 * LOG2_E)
        local = jnp.concatenate([s[:bl, :3 * bl], s[bl:, bl:4 * bl]], axis=0)
        return jnp.concatenate([jnp.where(valid, local, NEG_INF), s[:, 4 * bl:]], axis=1)

    def attend(s, sink, v_ones):
        sink = sink * LOG2_E
        m = jnp.maximum(jnp.max(s, axis=-1, keepdims=True), sink)
        p = jnp.exp2(s - m).astype(BF16)
        zeros = jnp.zeros((bl, bl), BF16)
        p = jnp.concatenate([jnp.concatenate([p[:bl, :3 * bl], zeros, p[:bl, 3 * bl:]], axis=1),
                             jnp.concatenate([zeros, p[bl:, :3 * bl], p[bl:, 3 * bl:]], axis=1)], axis=0)
        pv = _dot(p, v_ones)
        denom = pv[:, HEAD_DIM:] + jnp.exp2(sink - m)
        return pv[:, :HEAD_DIM] * (1.0 / denom)

    units = [(p, g) for p in range(pairs) for g in range(groups)]
    pending = scores(*units[0])
    for i, (p, g) in enumerate(units):
        s = pending
        if i + 1 < len(units):
            pending = scores(*units[i + 1])
        o = attend(s, sink_ref[kvh * groups + g], ops[p][1])
        o_ref[pl.ds(p * 2 * bl, 2 * bl), g * HEAD_DIM:(g + 1) * HEAD_DIM] = o.astype(BF16)


def _attention(z_l, z_c, sink, *, batch, seq, ctx_len, d_model, kv_width):
    bl = ATTN_BLOCK
    nb = seq // bl
    pairs = ATTN_PAIRS if nb % (2 * ATTN_PAIRS) == 0 else 1
    per_step = 2 * pairs
    assert nb % per_step == 0
    nsteps = nb // per_step
    n_kv = kv_width // HEAD_DIM
    groups = d_model // kv_width
    gw = groups * HEAD_DIM
    kcol, vcol = d_model // HEAD_DIM, (d_model + kv_width) // HEAD_DIM
    shifts = range(-1, per_step + 1)

    def kv_spec(col0, shift):
        def imap(b, h, n):
            return (b * nb + jnp.clip(per_step * n + shift, 0, nb - 1), col0 + h)
        return pl.BlockSpec((bl, HEAD_DIM), imap)

    ctx_k = pl.BlockSpec((ctx_len, HEAD_DIM), lambda b, h, n: (b, kcol + h))
    ctx_v = pl.BlockSpec((ctx_len, HEAD_DIM), lambda b, h, n: (b, vcol + h))
    blocks = (2 * _nbytes((per_step * bl, gw), BF16) + 2 * len(shifts) * _nbytes((bl, HEAD_DIM), BF16)
              + 2 * _nbytes((ctx_len, HEAD_DIM), BF16))
    temps = 8 * pairs * _nbytes((2 * bl, 4 * bl + ctx_len), F32)
    return pl.pallas_call(
        functools.partial(_attn_kernel, nsteps=nsteps, groups=groups, pairs=pairs),
        grid=(batch, n_kv, nsteps),
        in_specs=[pl.BlockSpec(memory_space=pltpu.SMEM),
                  pl.BlockSpec((per_step * bl, gw), lambda b, h, n: (b * nsteps + n, h))]
                 + [kv_spec(kcol, shift) for shift in shifts]
                 + [kv_spec(vcol, shift) for shift in shifts]
                 + [ctx_k, ctx_v],
        out_specs=pl.BlockSpec((per_step * bl, gw), lambda b, h, n: (b * nsteps + n, h)),
        out_shape=jax.ShapeDtypeStruct((batch * seq, d_model), BF16),
        compiler_params=_params(("parallel", "parallel", "arbitrary"), blocks, temps),
        name="window_attention",
    )(sink, *([z_l] * (1 + 2 * len(shifts))), z_c, z_c)


def _attn_ctx_kernel(sink_ref, q_ref, kx_ref, vx_ref, o_ref, *, groups):
    kvh = pl.program_id(1)
    rows = q_ref.shape[0]
    q = _stack_heads(q_ref[...], groups)
    s = _dot_nt(q, kx_ref[...]) * (HEAD_DIM ** -0.5)
    o = _softmax_pv(s, _sink_col(sink_ref, kvh, groups, rows), vx_ref[...])
    o_ref[...] = _unstack_heads(o, groups, rows).astype(BF16)


def _attention_ctx(qkv_c, sink, *, batch, ctx_len, d_model, kv_width):
    n_kv = kv_width // HEAD_DIM
    groups = d_model // kv_width
    gw = groups * HEAD_DIM
    kcol, vcol = d_model // HEAD_DIM, (d_model + kv_width) // HEAD_DIM
    blocks = 2 * _nbytes((ctx_len, gw), BF16) + 2 * _nbytes((ctx_len, HEAD_DIM), BF16)
    temps = 6 * _nbytes((groups * ctx_len, ctx_len), F32)
    return pl.pallas_call(
        functools.partial(_attn_ctx_kernel, groups=groups),
        grid=(batch, n_kv),
        in_specs=[pl.BlockSpec(memory_space=pltpu.SMEM),
                  pl.BlockSpec((ctx_len, gw), lambda b, h: (b, h)),
                  pl.BlockSpec((ctx_len, HEAD_DIM), lambda b, h: (b, kcol + h)),
                  pl.BlockSpec((ctx_len, HEAD_DIM), lambda b, h: (b, vcol + h))],
        out_specs=pl.BlockSpec((ctx_len, gw), lambda b, h: (b, h)),
        out_shape=jax.ShapeDtypeStruct((batch * ctx_len, d_model), BF16),
        compiler_params=_params(("parallel", "parallel"), blocks, temps),
        name="context_attention",
    )(sink, qkv_c, qkv_c, qkv_c)


CONV_PAD = SUBLANES
CONV_ROWS = 256


def _conv_seq(x_ref, w, o_ref, pad_ref, l2_flag, l2_scale):
    s = x_ref.shape[0]
    zeros = jnp.zeros((CONV_PAD, LANES), F32)
    pad_ref[pl.ds(0, CONV_PAD), :] = zeros
    pad_ref[pl.ds(CONV_PAD, s), :] = x_ref[...].astype(F32)
    pad_ref[pl.ds(CONV_PAD + s, CONV_PAD), :] = zeros
    rows = min(CONV_ROWS, s)

    def conv_silu(r0):
        y = jnp.zeros((rows, LANES), F32)
        for k in range(CONV_K):
            off = CONV_PAD - CONV_K // 2 + k
            y = y + pad_ref[pl.ds(r0 + off, rows), :] * w[k:k + 1, :]
        return _silu(y)

    def body_l2(it, carry):
        r0 = pl.multiple_of(it * rows, rows)
        y = conv_silu(r0)
        inv = lax.rsqrt(jnp.sum(y * y, axis=-1, keepdims=True) + EPS) * l2_scale
        o_ref[pl.ds(r0, rows), :] = (y * inv).astype(BF16)
        return carry

    def body_plain(it, carry):
        r0 = pl.multiple_of(it * rows, rows)
        o_ref[pl.ds(r0, rows), :] = conv_silu(r0).astype(BF16)
        return carry

    trips = s // rows
    unroll = 2 if trips % 2 == 0 else 1

    @pl.when(l2_flag)
    def _():
        lax.fori_loop(0, trips, body_l2, 0, unroll=unroll)

    @pl.when(jnp.logical_not(l2_flag))
    def _():
        lax.fori_loop(0, trips, body_plain, 0, unroll=unroll)


def _dnprep_kernel(xl_ref, xc_ref, w_ref, ol_ref, oc_ref, padl_ref, padc_ref, *, n_q):
    col = pl.program_id(1)
    l2_flag = col < 2 * n_q
    l2_scale = jnp.where(col < n_q, HEAD_DIM ** -0.5, 1.0)
    w = w_ref[...]
    _conv_seq(xl_ref, w, ol_ref, padl_ref, l2_flag, l2_scale)
    _conv_seq(xc_ref, w, oc_ref, padc_ref, l2_flag, l2_scale)


def _dn_prep(z_l, z_c, conv_w, *, col0, batch, seq, ctx_len, d_model):
    ncol = 3 * d_model // LANES
    blocks = 2 * (_nbytes((seq, LANES), BF16) + _nbytes((ctx_len, LANES), BF16)) + _nbytes((8, LANES), F32)
    scratch = _nbytes((seq + 2 * CONV_PAD, LANES), F32) + _nbytes((ctx_len + 2 * CONV_PAD, LANES), F32)
    return pl.pallas_call(
        functools.partial(_dnprep_kernel, n_q=d_model // LANES),
        grid=(batch, ncol),
        in_specs=[pl.BlockSpec((seq, LANES), lambda b, c: (b, col0 + c)),
                  pl.BlockSpec((ctx_len, LANES), lambda b, c: (b, col0 + c)),
                  pl.BlockSpec((CONV_K, LANES), lambda b, c: (0, c))],
        out_specs=[pl.BlockSpec((seq, LANES), lambda b, c: (b, c)),
                   pl.BlockSpec((ctx_len, LANES), lambda b, c: (b, c))],
        out_shape=[jax.ShapeDtypeStruct((batch * seq, 3 * d_model), BF16),
                   jax.ShapeDtypeStruct((batch * ctx_len, 3 * d_model), BF16)],
        scratch_shapes=[pltpu.VMEM((seq + 2 * CONV_PAD, LANES), F32),
                        pltpu.VMEM((ctx_len + 2 * CONV_PAD, LANES), F32)],
        compiler_params=_params(("parallel", "parallel"), blocks, scratch + 16 * _nbytes((CONV_ROWS, LANES), F32)),
        name="deltanet_prep",
    )(z_l, z_c, conv_w)


def _gates_kernel(ba_ref, alog_ref, dtb_ref, o_ref, *, n_heads):
    tm = ba_ref.shape[0]
    x = ba_ref[...]
    lane = lax.broadcasted_iota(jnp.int32, (CHUNK, LANES), 1)
    z = x + dtb_ref[...]
    softplus = jnp.maximum(z, 0.0) + jnp.log(1.0 + jnp.exp(-jnp.abs(z)))
    g = -jnp.exp(alog_ref[...]) * softplus
    beta = _sigmoid(x)
    r = lax.broadcasted_iota(jnp.int32, (CHUNK, CHUNK), 0)
    c = lax.broadcasted_iota(jnp.int32, (CHUNK, CHUNK), 1)
    tri_prefix = (c <= r).astype(BF16)
    tri_suffix = (c >= r).astype(BF16)
    for i in range(tm // CHUNK):
        rows = slice(i * CHUNK, (i + 1) * CHUNK)
        parts = _split3(g[rows])
        pre = sum(_dot(tri_prefix, p) for p in parts)
        suf = sum(_dot(tri_suffix, p) for p in parts)
        csum = jnp.where(lane < 3 * n_heads, pre, suf)
        o_ref[rows, :] = jnp.where(lane < 2 * n_heads, beta[rows], csum)


def _gates(ba, alog_vec, dtb_vec, *, n_heads):
    m = ba.shape[0]
    tm = _pick(m, (1024, 512, 256, 128))
    blocks = 2 * _nbytes((tm, LANES), F32)
    return pl.pallas_call(
        functools.partial(_gates_kernel, n_heads=n_heads),
        grid=(m // tm,),
        in_specs=[pl.BlockSpec((tm, LANES), lambda i: (i, 0)),
                  pl.BlockSpec((1, LANES), lambda i: (0, 0)),
                  pl.BlockSpec((1, LANES), lambda i: (0, 0))],
        out_specs=pl.BlockSpec((tm, LANES), lambda i: (i, 0)),
        out_shape=jax.ShapeDtypeStruct((m, LANES), F32),
        compiler_params=_params(("parallel",), blocks, 8 * _nbytes((tm, LANES), F32)),
        name="deltanet_gates",
    )(ba, alog_vec, dtb_vec)


def _dot_split(a, b):
    n = b[0].shape[1]
    lhs = jnp.concatenate(a, axis=1)
    half = jnp.concatenate(b, axis=1)
    out = _dot(lhs, jnp.concatenate([half, half], axis=0))
    return out[:, :n] + out[:, n:]


def _unit_tri_inverse_many(mats, hooks=()):
    n = mats[0].shape[0]
    row = lax.broadcasted_iota(jnp.int32, (n, n), 0)
    col = lax.broadcasted_iota(jnp.int32, (n, n), 1)
    splits = [_split2(a) for a in mats]
    eye = jnp.where(row == col, 1.0, 0.0)
    pair = (row // 2) == (col // 2)
    xs = [eye - jnp.where(pair, a, 0.0) for a in mats]
    s, level = 2, 0
    while s < n:
        couple = ((row // (2 * s)) == (col // (2 * s))) & ((row // s) != (col // s))
        zero16 = jnp.zeros((n, n), BF16)
        es = [(jnp.where(couple, hi, zero16), jnp.where(couple, lo, zero16)) for hi, lo in splits]
        xsp = [_split2(x) for x in xs]
        exs = [_split2(_dot_split(e, xp)) for e, xp in zip(es, xsp)]
        xs = [x - _dot_split(xp, ex) for x, xp, ex in zip(xs, xsp, exs)]
        if level < len(hooks):
            hooks[level]()
        s *= 2
        level += 1
    return xs


def _scan_prep_group(refs, chunks, dsts, sel, scr, hooks=()):
    u_s, w_s, qk_s, qd_s, kdt_s, gl_s = scr
    c = CHUNK
    group = len(chunks)
    per_chunk = refs if isinstance(refs, list) else [refs] * group
    hooks = list(hooks)
    if hooks:
        hooks.pop(0)()
    row = lax.broadcasted_iota(jnp.int32, (c, c), 0)
    col = lax.broadcasted_iota(jnp.int32, (c, c), 1)
    eye16 = (row == col).astype(BF16)
    r0s = [pl.multiple_of(ci * c, c) for ci in chunks]
    q16s = [r[0][pl.ds(r0, c), :] for r, r0 in zip(per_chunk, r0s)]
    k16s = [r[1][pl.ds(r0, c), :] for r, r0 in zip(per_chunk, r0s)]
    qs = [q.astype(F32) for q in q16s]
    ks = [k.astype(F32) for k in k16s]
    vs = [r[2][pl.ds(r0, c), :].astype(F32) for r, r0 in zip(per_chunk, r0s)]
    sel2 = jnp.concatenate([sel, sel], axis=0)
    parts = [_split3(r[3][pl.ds(r0, c), :]) for r, r0 in zip(per_chunk, r0s)]
    gates = [_dot(jnp.concatenate([hi, mid], axis=1), sel2) + _dot(lo, sel) for hi, mid, lo in parts]
    kks = [_dot_nt(k16, k16) for k16 in k16s]
    qks = [_dot_nt(q16, k16) for q16, k16 in zip(q16s, k16s)]
    probs = [(g, d) for g in range(group) for d in range(2)]
    betas = [gates[g][:, d * LANES:(d + 1) * LANES] for g, d in probs]
    ccols = [gates[g][:, (2 + d) * LANES:(3 + d) * LANES] for g, d in probs]
    decays, mats = [], []
    for (g, d), beta, ccol in zip(probs, betas, ccols):
        crow = per_chunk[g][4][d][0, :, pl.ds(r0s[g], c)]
        incl = (row >= col) if d == 0 else (row <= col)
        strict = (row > col) if d == 0 else (row < col)
        decay = jnp.exp(jnp.where(incl, ccol - crow, NEG_INF))
        decays.append(decay)
        mats.append(jnp.where(strict, beta * kks[g] * decay, 0.0))
    if hooks:
        hooks.pop(0)()
    xsp = [_split2(x) for x in _unit_tri_inverse_many(mats, hooks)]
    expcs = [jnp.exp(ccol) for ccol in ccols]
    rus = [_split2(vs[g] * beta) for (g, d), beta in zip(probs, betas)]
    us = [_dot_split(xp, ru) for xp, ru in zip(xsp, rus)]
    rws = [(ks[g] * beta * expc).astype(BF16) for (g, d), beta, expc in zip(probs, betas, expcs)]
    ws = [_dot(jnp.concatenate(xp, axis=1), jnp.concatenate([rw, rw], axis=0)) for xp, rw in zip(xsp, rws)]
    clasts = [ccol[c - 1:c, :] if d == 0 else ccol[0:1, :] for (g, d), ccol in zip(probs, ccols)]
    kdts = [_dot_nt(eye16, (ks[g] * jnp.exp(clast - ccol)).astype(BF16))
            for (g, d), clast, ccol in zip(probs, clasts, ccols)]
    for i, (g, d) in enumerate(probs):
        rows = pl.ds(pl.multiple_of(dsts[g] * c, c), c)
        u_s[d, rows, :] = us[i]
        w_s[d, rows, :] = ws[i].astype(BF16)
        qk_s[d, rows, :] = (qks[g] * decays[i]).astype(BF16)
        qd_s[d, rows, :] = (qs[g] * expcs[i]).astype(BF16)
        kdt_s[d, rows, :] = kdts[i].astype(BF16)
        gl_s[d, pl.ds(pl.multiple_of(dsts[g] * SUBLANES, SUBLANES), SUBLANES), :] = jnp.broadcast_to(
            jnp.exp(clasts[i]), (SUBLANES, LANES))


def _step_issue(chunks, states, scr):
    u_s, w_s, qk_s, qd_s, kdt_s, gl_s = scr
    c = CHUNK
    rows = [pl.ds(pl.multiple_of(ci * c, c), c) for ci in chunks]
    dirs = range(2)
    ws = [_dot(jnp.concatenate([w_s[d, rows[d], :], qd_s[d, rows[d], :]], axis=0), states[d].astype(BF16))
          for d in dirs]
    return dict(chunks=chunks, rows=rows, states=states, ws=ws)


def _step_finish(pending, scr):
    u_s, w_s, qk_s, qd_s, kdt_s, gl_s = scr
    c = CHUNK
    chunks, rows, states, ws = pending["chunks"], pending["rows"], pending["states"], pending["ws"]
    dirs = range(2)
    v_new = [(u_s[d, rows[d], :] - ws[d][:c]).astype(BF16) for d in dirs]
    outs = [ws[d][c:] + _dot(qk_s[d, rows[d], :], v_new[d]) for d in dirs]
    gls = [gl_s[d, pl.ds(pl.multiple_of(chunks[d] * SUBLANES, SUBLANES), SUBLANES), :][0:1, :] for d in dirs]
    new_states = [states[d] * gls[d] + _dot(kdt_s[d, rows[d], :], v_new[d]) for d in dirs]
    return tuple(new_states), outs


def _scan_step_pair(chunks, states, scr):
    return _step_finish(_step_issue(chunks, states, scr), scr)


PREP_GROUP = 4


def _scan_kernel(ql, kl, vl, pl_l, crfl, crbl, zl_prev, qx, kx, vx, px, crfx, crbx, zx, gain_ref,
                 outl_prev_ref, outx_ref, u_s, w_s, qk_s, qd_s, kdt_s, gl_s, o_s, st_s, *, n_heads):
    b, j = pl.program_id(0), pl.program_id(1)
    c = CHUNK
    ncx, ncl = qx.shape[0] // c, ql.shape[0] // c
    half = PREP_GROUP // 2
    n_groups = ncl // PREP_GROUP
    assert PREP_GROUP == 4 and ncx % 2 == 0 and ncl % PREP_GROUP == 0 and ncx <= PREP_GROUP
    cur = j % 2
    factor_refs = (u_s, w_s, qk_s, qd_s, kdt_s, gl_s)
    scr_cur, o_cur = tuple(r.at[cur] for r in factor_refs), o_s.at[cur]
    scr_prev, o_prev = tuple(r.at[1 - cur] for r in factor_refs), o_s.at[1 - cur]
    gain = gain_ref[...]

    def chunk_rows(i):
        return pl.ds(pl.multiple_of(i * c, c), c)

    @pl.when((b == 0) & (j == 0))
    def _():
        def body(i, carry):
            for r in factor_refs[:5]:
                for d in range(2):
                    r[1, d, chunk_rows(i), :] = jnp.zeros((c, LANES), r.dtype)
            o_s[1, chunk_rows(i), :] = jnp.zeros((c, LANES), F32)
            return carry
        lax.fori_loop(0, ncx + ncl, body, 0)
        gl_s[1] = jnp.zeros(gl_s.shape[1:], F32)
        st_s[...] = jnp.zeros(st_s.shape, F32)

    def finish(z_ref, out_ref, i, o):
        on = o * lax.rsqrt(jnp.mean(o * o, axis=-1, keepdims=True) + EPS) * gain
        out_ref[chunk_rows(i), :] = (on * _silu(z_ref[chunk_rows(i), :].astype(F32))).astype(BF16)

    def step_stages(scr, o_ref, base, n, it, box, key, z_ref=None, out_ref=None):
        pending = {}
        jf, jb = it, n - 1 - it

        def issue():
            pending.update(_step_issue((base + jf, base + jb), box[key], scr))

        def complete():
            box[key], outs = _step_finish(pending, scr)
            if z_ref is None:
                o_ref[chunk_rows(base + jf), :] = outs[0]
                o_ref[chunk_rows(base + jb), :] = outs[1]
            else:
                finish(z_ref, out_ref, jf, outs[0] + o_ref[chunk_rows(base + jf), :])
                finish(z_ref, out_ref, jb, outs[1] + o_ref[chunk_rows(base + jb), :])
        return [issue, complete]

    def run(stages):
        for stage in stages:
            stage()

    def front(i):
        return [half * i + k for k in range(half)]

    def drain_stages(p, box):
        if isinstance(p, int) and p < half:
            return step_stages(scr_prev, o_prev, ncx, ncl, front(n_groups - 1)[p], box, "prev")
        return step_stages(scr_prev, o_prev, ncx, ncl, ncl // 2 + p - half, box, "prev", zl_prev, outl_prev_ref)

    n_drain = half + ncl // 2

    @pl.when(j < n_heads)
    def _():
        r = lax.broadcasted_iota(jnp.int32, (LANES, 4 * LANES), 0)
        t = lax.broadcasted_iota(jnp.int32, (LANES, 4 * LANES), 1) // LANES
        sel = (r == t * n_heads + j).astype(BF16)
        zero = jnp.zeros((HEAD_DIM, HEAD_DIM), F32)
        box = {"cur": (zero, zero), "prev": (st_s[0], st_s[1])}

        refs_x = (qx, kx, vx, px, (crfx, crbx))
        refs_l = (ql, kl, vl, pl_l, (crfl, crbl))

        def latent_chunks(i):
            return front(i) + [ncl - 1 - ci for ci in reversed(front(i))]

        def prep_latent(i, hooks):
            chunks = latent_chunks(i)
            _scan_prep_group(refs_l, chunks, [ncx + ci for ci in chunks], sel, scr_cur, hooks)

        first = latent_chunks(0)
        _scan_prep_group([refs_x] * ncx + [refs_l] * len(first), list(range(ncx)) + first,
                         list(range(ncx)) + [ncx + ci for ci in first], sel, scr_cur,
                         [stage for p in range(2 * half) for stage in drain_stages(p, box)])
        for it in range(ncx):
            ends = (zx, outx_ref) if it >= ncx // 2 else ()
            run(step_stages(scr_cur, o_cur, 0, ncx, it, box, "cur", *ends))

        def overlapped(i, carry):
            box = {"cur": carry[:2], "prev": carry[2:]}
            hooks = []
            for k, it in enumerate(front(i - 1)):
                own = step_stages(scr_cur, o_cur, ncx, ncl, it, box, "cur")
                other = drain_stages(2 * half + half * (i - 1) + k, box)
                hooks += [own[0], other[0], own[1], other[1]]
            prep_latent(i, hooks)
            return tuple(box["cur"]) + tuple(box["prev"])

        carry = lax.fori_loop(1, n_groups, overlapped, tuple(box["cur"]) + tuple(box["prev"]))
        st_s[0] = carry[0]
        st_s[1] = carry[1]

    @pl.when(j == n_heads)
    def _():
        box = {"prev": (st_s[0], st_s[1])}
        for p in range(half):
            run(drain_stages(p, box))

        def body(p, states):
            inner = {"prev": states}
            run(drain_stages(p, inner))
            return tuple(inner["prev"])
        lax.fori_loop(half, n_drain, body, tuple(box["prev"]))


def _delta_scan(dn_l, dn_c, gates_l, gates_c, gates_lt, gates_ct, z_l, z_c, gain,
                *, zcol, batch, seq, ctx_len, d_model):
    n_heads = d_model // HEAD_DIM
    t = seq + ctx_len
    last = n_heads - 1

    def col_spec(rows, col0, lag=0):
        return pl.BlockSpec((rows, LANES), lambda b, j: (b, col0 + jnp.clip(j - lag, 0, last)))

    def gate_spec(rows):
        return pl.BlockSpec((rows, LANES), lambda b, j: (b, 0))

    def crow_spec(rows, d):
        return pl.BlockSpec((1, 1, rows), lambda b, j: ((2 + d) * n_heads + jnp.minimum(j, last), 0, b))

    def side(rows, z_lag):
        return [col_spec(rows, 0), col_spec(rows, n_heads), col_spec(rows, 2 * n_heads), gate_spec(rows),
                crow_spec(rows, 0), crow_spec(rows, 1), col_spec(rows, zcol, z_lag)]

    blocks = (_nbytes((t, LANES), F32) + 2 * _nbytes((SUBLANES, t), F32) + 5 * _nbytes((t, LANES), BF16))
    slots = 2
    scratch_shapes = [pltpu.VMEM((slots, 2, t, LANES), F32),
                      pltpu.VMEM((slots, 2, t, LANES), BF16),
                      pltpu.VMEM((slots, 2, t, LANES), BF16),
                      pltpu.VMEM((slots, 2, t, LANES), BF16),
                      pltpu.VMEM((slots, 2, t, LANES), BF16),
                      pltpu.VMEM((slots, 2, (t // CHUNK) * SUBLANES, LANES), F32),
                      pltpu.VMEM((slots, t, LANES), F32),
                      pltpu.VMEM((2, HEAD_DIM, HEAD_DIM), F32)]
    scratch = slots * (3 * _nbytes((t, LANES), F32) + 4 * 2 * _nbytes((t, LANES), BF16))
    return pl.pallas_call(
        functools.partial(_scan_kernel, n_heads=n_heads),
        grid=(batch, n_heads + 1),
        in_specs=side(seq, 1) + side(ctx_len, 0) + [pl.BlockSpec((1, LANES), lambda b, j: (0, 0))],
        out_specs=[pl.BlockSpec((seq, LANES), lambda b, j: (b, jnp.clip(j - 1, 0, last))),
                   pl.BlockSpec((ctx_len, LANES), lambda b, j: (b, jnp.minimum(j, last)))],
        out_shape=[jax.ShapeDtypeStruct((batch * seq, d_model), BF16),
                   jax.ShapeDtypeStruct((batch * ctx_len, d_model), BF16)],
        scratch_shapes=scratch_shapes,
        compiler_params=_params(("arbitrary", "arbitrary"), blocks, scratch + 4 * 1024 * 1024),
        name="delta_scan",
    )(dn_l, dn_l, dn_l, gates_l, gates_lt, gates_lt, z_l,
      dn_c, dn_c, dn_c, gates_c, gates_ct, gates_ct, z_c, gain)


def _merge_kernel(att_ref, dn_ref, ga_ref, gd_ref, wa_ref, wd_ref, o_ref):
    tm = o_ref.shape[0]
    sub = min(tm, INPROJ_SUB_ROWS)
    for r in range(tm // sub):
        rows = pl.ds(r * sub, sub)
        y = (_sigmoid(ga_ref[rows, :].astype(F32)) * _dot(att_ref[rows, :], wa_ref[...])
             + _sigmoid(gd_ref[rows, :].astype(F32)) * _dot(dn_ref[rows, :], wd_ref[...]))
        o_ref[rows, :] = y.astype(BF16)


def _merge(att, dn, z, w_attn, w_delta, layer, *, gate_col, tm, tn):
    m, d = att.shape
    ga0, gd0 = gate_col // tn, (gate_col + d) // tn
    blocks = (2 * _nbytes((tm, d), BF16) + 2 * _nbytes((tm, tn), BF16) + 2 * _nbytes((d, tn), BF16)
              + _nbytes((tm, tn), BF16))
    return pl.pallas_call(
        _merge_kernel,
        grid=(m // tm, d // tn),
        in_specs=[pl.BlockSpec((tm, d), lambda i, j: (i, 0)),
                  pl.BlockSpec((tm, d), lambda i, j: (i, 0)),
                  pl.BlockSpec((tm, tn), lambda i, j: (i, ga0 + j)),
                  pl.BlockSpec((tm, tn), lambda i, j: (i, gd0 + j)),
                  pl.BlockSpec((None, d, tn), lambda i, j: (layer, 0, j)),
                  pl.BlockSpec((None, d, tn), lambda i, j: (layer, 0, j))],
        out_specs=pl.BlockSpec((tm, tn), lambda i, j: (i, j)),
        out_shape=jax.ShapeDtypeStruct((m, d), BF16),
        compiler_params=_params(("parallel", "arbitrary"), blocks, 4 * _nbytes((tm, tn), F32)),
        name="merge",
    )(att, dn, z, z, w_attn, w_delta)


def _resid_kernel(a_ref, w_ref, x_ref, mod_ref, o_ref, *, gate_row):
    o_ref[...] = x_ref[...] + mod_ref[0, gate_row:gate_row + 1, :] * _dot(a_ref[...], w_ref[...])


def _resid_matmul(a, w, layer, x, mods, mod_row, gate_row, *, tm, tn, name):
    m, k = a.shape
    d = w.shape[2]
    blocks = (_nbytes((tm, k), BF16) + _nbytes((k, tn), BF16) + 2 * _nbytes((tm, tn), F32)
              + _nbytes((6, tn), F32))
    return pl.pallas_call(
        functools.partial(_resid_kernel, gate_row=gate_row),
        grid=(m // tm, d // tn),
        in_specs=[pl.BlockSpec((tm, k), lambda i, j: (i, 0)),
                  pl.BlockSpec((None, k, tn), lambda i, j: (layer, 0, j)),
                  pl.BlockSpec((tm, tn), lambda i, j: (i, j)),
                  pl.BlockSpec((1, 6, tn), lambda i, j: (mod_row(i, tm), 0, j))],
        out_specs=pl.BlockSpec((tm, tn), lambda i, j: (i, j)),
        out_shape=jax.ShapeDtypeStruct((m, d), F32),
        compiler_params=_params(("parallel", "arbitrary"), blocks, 2 * _nbytes((tm, tn), F32)),
        name=name,
    )(a, w, x, mods)


def _ffn_up_kernel(x_ref, mod_ref, g_ref, wg_ref, wu_ref, o_ref, h_ref):
    j = pl.program_id(1)
    tm = o_ref.shape[0]
    sub = min(tm, INPROJ_SUB_ROWS)

    def tile(fresh):
        for r in range(tm // sub):
            rows = pl.ds(r * sub, sub)
            if fresh:
                h = _norm_mod(x_ref[rows, :], g_ref[...], mod_ref[0, 3:4, :], mod_ref[0, 4:5, :]).astype(BF16)
                h_ref[rows, :] = h
            else:
                h = h_ref[rows, :]
            o_ref[rows, :] = (_silu(_dot(h, wg_ref[...])) * _dot(h, wu_ref[...])).astype(BF16)

    @pl.when(j == 0)
    def _():
        tile(True)

    @pl.when(j > 0)
    def _():
        tile(False)


def _ffn_up(x, mods, mod_row, gain, w_gate_up, layer, *, tm, tn):
    m, d = x.shape
    f = w_gate_up.shape[2] // 2
    nj = f // tn
    blocks = (_nbytes((tm, d), F32) + _nbytes((6, d), F32) + 2 * _nbytes((d, tn), BF16)
              + _nbytes((tm, tn), BF16))
    return pl.pallas_call(
        _ffn_up_kernel,
        grid=(m // tm, nj),
        in_specs=[pl.BlockSpec((tm, d), lambda i, j: (i, 0)),
                  pl.BlockSpec((1, 6, d), lambda i, j: (mod_row(i, tm), 0, 0)),
                  pl.BlockSpec((1, d), lambda i, j: (0, 0)),
                  pl.BlockSpec((None, d, tn), lambda i, j: (layer, 0, j)),
                  pl.BlockSpec((None, d, tn), lambda i, j: (layer, 0, nj + j))],
        out_specs=pl.BlockSpec((tm, tn), lambda i, j: (i, j)),
        out_shape=jax.ShapeDtypeStruct((m, f), BF16),
        scratch_shapes=[pltpu.VMEM((tm, d), BF16)],
        compiler_params=_params(("parallel", "arbitrary"), blocks,
                                _nbytes((tm, d), BF16) + 4 * _nbytes((tm, tn), F32)),
        name="ffn_up",
    )(x, mods, gain, w_gate_up, w_gate_up)


def _final_norm_kernel(x_ref, g_ref, o_ref):
    x = x_ref[...]
    o_ref[...] = x * lax.rsqrt(jnp.mean(x * x, axis=-1, keepdims=True) + EPS) * g_ref[...]


def _final_norm(x, gain, *, tm):
    m, d = x.shape
    return pl.pallas_call(
        _final_norm_kernel,
        grid=(m // tm,),
        in_specs=[pl.BlockSpec((tm, d), lambda i: (i, 0)), pl.BlockSpec((1, d), lambda i: (0, 0))],
        out_specs=pl.BlockSpec((tm, d), lambda i: (i, 0)),
        out_shape=jax.ShapeDtypeStruct((m, d), F32),
        compiler_params=_params(("parallel",), 2 * _nbytes((tm, d), F32), 2 * _nbytes((tm, d), F32)),
        name="final_norm",
    )(x, gain)


def _rope_tables(seq):
    quarter = HEAD_DIM // 4
    pos = jnp.arange(seq)
    row = (pos // GRID_W).astype(F32)
    col = (pos % GRID_W).astype(F32)
    inv_freq = ROPE_BASE ** (-jnp.arange(quarter, dtype=F32) / quarter)
    ang_r, ang_c = row[:, None] * inv_freq[None], col[:, None] * inv_freq[None]
    cos = jnp.concatenate([jnp.cos(ang_r), jnp.cos(ang_c)] * 2, axis=1)
    sin = jnp.concatenate([-jnp.sin(ang_r), -jnp.sin(ang_c), jnp.sin(ang_r), jnp.sin(ang_c)], axis=1)
    return cos, sin


def _rope_perm():
    quarter = HEAD_DIM // 4
    order = (0, 2, 1, 3)
    return jnp.concatenate([jnp.arange(quarter) + quarter * o for o in order])


def kernel(x, c, ctx, c_ctx, w_mod, b_mod, norm1, norm2, w_in, conv_w, attn_sink, a_log, dt_bias, dn_norm,
           w_attn_proj, w_delta_proj, w_out, w_gate_up, w_down, final_norm):
    batch, seq, d = x.shape
    ctx_len = ctx.shape[1]
    depth = w_mod.shape[0]
    n_heads = d // HEAD_DIM
    kv_width = (n_heads // 4) * HEAD_DIM
    qkv_width = d + 2 * kv_width
    assert seq % CHUNK == 0 and ctx_len % CHUNK == 0 and seq % ATTN_BLOCK == 0
    assert 4 * n_heads <= LANES and (batch * seq) % ctx_len == 0

    off_ba = qkv_width + 4 * d
    qk_width = d + kv_width
    w_qk = w_in[:, :, :qk_width].reshape(depth, d, qk_width // HEAD_DIM, HEAD_DIM)[..., _rope_perm()]
    w_main = jnp.concatenate([w_qk.reshape(depth, d, qk_width), w_in[:, :, qk_width:off_ba],
                              w_in[:, :, off_ba + 4 * n_heads:]], axis=2).astype(BF16)
    w_ba = jnp.pad(w_in[:, :, off_ba:off_ba + 4 * n_heads], ((0, 0), (0, 0), (0, LANES - 4 * n_heads))).astype(BF16)
    w_attn16, w_delta16, w_out16 = w_attn_proj.astype(BF16), w_delta_proj.astype(BF16), w_out.astype(BF16)
    w_gu16, w_down16 = w_gate_up.astype(BF16), w_down.astype(BF16)

    lane_pad = ((0, 0), (2 * n_heads, LANES - 4 * n_heads))
    alog_vec = jnp.pad(a_log.reshape(depth, 2 * n_heads), lane_pad).reshape(depth, 1, LANES)
    dtb_vec = jnp.pad(dt_bias.reshape(depth, 2 * n_heads), lane_pad).reshape(depth, 1, LANES)

    cos_l, sin_l = _rope_tables(seq)
    cos_l, sin_l = jnp.tile(cos_l, (batch, 1)), jnp.tile(sin_l, (batch, 1))
    cos_c = jnp.ones((batch * ctx_len, LANES), F32)
    sin_c = jnp.zeros((batch * ctx_len, LANES), F32)

    mod_rows = ((batch + 1 + SUBLANES - 1) // SUBLANES) * SUBLANES
    cc = jnp.concatenate([c, c_ctx[None], jnp.zeros((mod_rows - batch - 1, d), F32)], axis=0)
    mods_all = _modulation(cc, w_mod, b_mod).reshape(depth, mod_rows, 6, d)

    tm_l = _pick(seq, (1024, 512, 256, 128))
    tm_c = _pick(batch * ctx_len, (1024, 512, 256, 128))
    tn = _pick(kv_width, (512, 256, 128))
    tn_in = _pick(w_main.shape[2], (1024, 512, 256, 128))

    def row_l(i, tm):
        return (i * tm) // seq

    def row_c(i, tm):
        return batch

    xl = x.reshape(batch * seq, d)
    xc = ctx.reshape(batch * ctx_len, d)
    dims = dict(batch=batch, seq=seq, ctx_len=ctx_len, d_model=d)

    for l in range(depth):
        mods = mods_all[l]
        need_ctx = l < depth - 1
        g1, g2 = norm1[l][None], norm2[l][None]
        proj = functools.partial(_inproj, gain=g1, w_main=w_main, w_ba=w_ba, layer=l, tn=tn_in,
                                 qk_width=d + kv_width)
        z_l, ba_l = proj(xl, mods, row_l, cos=cos_l, sin=sin_l, tm=tm_l)
        z_c, ba_c = proj(xc, mods, row_c, cos=cos_c, sin=sin_c, tm=tm_c)

        att_l = _attention(z_l, z_c, attn_sink[l], kv_width=kv_width, **dims)

        dn_l, dn_c = _dn_prep(z_l, z_c, conv_w[l], col0=qkv_width // LANES, **dims)
        gates_l = _gates(ba_l, alog_vec[l], dtb_vec[l], n_heads=n_heads)
        gates_c = _gates(ba_c, alog_vec[l], dtb_vec[l], n_heads=n_heads)
        gates_lt = gates_l.T.reshape(LANES, 1, batch * seq)
        gates_ct = gates_c.T.reshape(LANES, 1, batch * ctx_len)
        do_l, do_c = _delta_scan(dn_l, dn_c, gates_l, gates_c, gates_lt, gates_ct, z_l, z_c,
                                 dn_norm[l][None], zcol=(qkv_width + 3 * d) // LANES, **dims)

        streams = [(xl, att_l, do_l, z_l, row_l, tm_l)]
        if need_ctx:
            att_c = _attention_ctx(z_c, attn_sink[l], batch=batch, ctx_len=ctx_len, d_model=d, kv_width=kv_width)
            streams.append((xc, att_c, do_c, z_c, row_c, tm_c))
        outs = []
        for xs, att, dn, z, row, tm in streams:
            y = _merge(att, dn, z, w_attn16, w_delta16, l, gate_col=qkv_width + 4 * d, tm=tm, tn=tn)
            xs = _resid_matmul(y, w_out16, l, xs, mods, row, 2, tm=tm, tn=tn, name="out_proj")
            act = _ffn_up(xs, mods, row, g2, w_gu16, l, tm=tm, tn=tn)
            xs = _resid_matmul(act, w_down16, l, xs, mods, row, 5, tm=tm, tn=tn, name="ffn_down")
            outs.append(xs)
        xl = outs[0]
        if need_ctx:
            xc = outs[1]

    return _final_norm(xl, final_norm[None], tm=tm_l).reshape(batch, seq, d)
```

```python
import functools

import jax
import jax.numpy as jnp
from jax import lax
from jax.experimental import pallas as pl
from jax.experimental.pallas import tpu as pltpu

F32 = jnp.float32
BF16 = jnp.bfloat16

HEAD_DIM = 128
WINDOW = 128
ATTN_BLOCK = 128
GRID_W = 64
CONV_K = 5
ROPE_BASE = 10000.0
EPS = 1e-6
NEG_INF = -1e30
LOG2_E = 1.4426950408889634
CHUNK = 128
LANES = 128
SUBLANES = 8
VMEM_CAP_BYTES = 60 * 1024 * 1024


def _vmem_limit(block_bytes, scratch_bytes=0):
    est = 2 * block_bytes + scratch_bytes
    return int(min(VMEM_CAP_BYTES, max(32 * 1024 * 1024, est * 3 // 2)))


def _params(sem, block_bytes, scratch_bytes=0):
    return pltpu.CompilerParams(dimension_semantics=sem,
                                vmem_limit_bytes=_vmem_limit(block_bytes, scratch_bytes))


def _pick(n, candidates):
    for c in candidates:
        if n % c == 0:
            return c
    raise ValueError(f"no tile in {candidates} divides {n}")


def _nbytes(shape, dtype):
    n = 1
    for s in shape:
        n *= s
    return n * jnp.dtype(dtype).itemsize


def _sigmoid(x):
    return 1.0 / (1.0 + jnp.exp(-x))


def _silu(x):
    return x * _sigmoid(x)


def _dot(a, b):
    return jnp.dot(a, b, preferred_element_type=F32)


def _dot_nt(a, b):
    return lax.dot_general(a, b, (((1,), (1,)), ((), ())), preferred_element_type=F32)


def _split2(x):
    hi = x.astype(BF16)
    lo = (x - hi.astype(F32)).astype(BF16)
    return hi, lo


def _split3(x):
    hi = x.astype(BF16)
    r = x - hi.astype(F32)
    mid = r.astype(BF16)
    lo = (r - mid.astype(F32)).astype(BF16)
    return hi, mid, lo


def _mod_kernel(c_ref, w_ref, b_ref, o_ref):
    s = _silu(c_ref[...])
    s_hi, s_lo = _split2(s)
    w = w_ref[0]
    w_hi, w_lo = _split2(w)
    acc = _dot(s_hi, w_hi) + _dot(s_hi, w_lo) + _dot(s_lo, w_hi)
    o_ref[0] = acc + b_ref[0]


def _modulation(cc, w_mod, b_mod):
    depth, d, n = w_mod.shape
    rows = cc.shape[0]
    tn = _pick(n, (512, 256, 128))
    blocks = _nbytes((rows, d), F32) + _nbytes((d, tn), F32) + _nbytes((1, tn), F32) + _nbytes((rows, tn), F32)
    return pl.pallas_call(
        _mod_kernel,
        grid=(depth, n // tn),
        in_specs=[pl.BlockSpec((rows, d), lambda l, j: (0, 0)),
                  pl.BlockSpec((1, d, tn), lambda l, j: (l, 0, j)),
                  pl.BlockSpec((1, 1, tn), lambda l, j: (l, 0, j))],
        out_specs=pl.BlockSpec((1, rows, tn), lambda l, j: (l, 0, j)),
        out_shape=jax.ShapeDtypeStruct((depth, rows, n), F32),
        compiler_params=_params(("parallel", "parallel"), blocks, 3 * _nbytes((d, tn), F32)),
        name="modulation",
    )(cc, w_mod, b_mod.reshape(depth, 1, n))


INPROJ_SUB_ROWS = 256


def _norm_mod(x, gain, shift, scale):
    y = x * lax.rsqrt(jnp.mean(x * x, axis=-1, keepdims=True) + EPS) * gain
    return y * (1.0 + scale) + shift


def _inproj_kernel(x_ref, mod_ref, g_ref, w_ref, wba_ref, cos_ref, sin_ref, z_ref, ba_ref, h_ref, *, qk_width):
    j = pl.program_id(1)
    tm, tn = z_ref.shape
    full, rem = divmod(qk_width, tn)
    sub = min(tm, INPROJ_SUB_ROWS)

    def rope(t, rows):
        width = t.shape[1]
        cos = jnp.concatenate([cos_ref[rows, :]] * (width // HEAD_DIM), axis=1)
        sin = jnp.concatenate([sin_ref[rows, :]] * (width // HEAD_DIM), axis=1)
        partner = jnp.concatenate([pltpu.roll(t[:, g * HEAD_DIM:(g + 1) * HEAD_DIM], HEAD_DIM // 2, 1)
                                   for g in range(width // HEAD_DIM)], axis=1)
        return t * cos + partner * sin

    def tile(epilogue, fresh=False):
        for r in range(tm // sub):
            rows = pl.ds(r * sub, sub)
            if fresh:
                h = _norm_mod(x_ref[rows, :], g_ref[...], mod_ref[0, 0:1, :], mod_ref[0, 1:2, :]).astype(BF16)
                h_ref[rows, :] = h
                ba_ref[rows, :] = _dot(h, wba_ref[...])
            else:
                h = h_ref[rows, :]
            z_ref[rows, :] = epilogue(_dot(h, w_ref[...]), rows).astype(BF16)

    assert full >= 1

    @pl.when(j == 0)
    def _():
        tile(rope, fresh=True)

    @pl.when((j > 0) & (j < full))
    def _():
        tile(rope)

    if rem:
        @pl.when(j == full)
        def _():
            tile(lambda acc, rows: jnp.concatenate([rope(acc[:, :rem], rows), acc[:, rem:]], axis=1))

    @pl.when(j >= full + (1 if rem else 0))
    def _():
        tile(lambda acc, rows: acc)


def _inproj(x, mods, mod_row, gain, w_main, w_ba, layer, cos, sin, *, tm, tn, qk_width):
    m, d = x.shape
    n = w_main.shape[2]
    blocks = (_nbytes((tm, d), F32) + _nbytes((6, d), F32) + _nbytes((d, tn), BF16) + _nbytes((d, LANES), BF16)
              + 2 * _nbytes((tm, LANES), F32) + _nbytes((tm, tn), BF16) + _nbytes((tm, LANES), F32))
    return pl.pallas_call(
        functools.partial(_inproj_kernel, qk_width=qk_width),
        grid=(m // tm, n // tn),
        in_specs=[pl.BlockSpec((tm, d), lambda i, j: (i, 0)),
                  pl.BlockSpec((1, 6, d), lambda i, j: (mod_row(i, tm), 0, 0)),
                  pl.BlockSpec((1, d), lambda i, j: (0, 0)),
                  pl.BlockSpec((None, d, tn), lambda i, j: (layer, 0, j)),
                  pl.BlockSpec((None, d, LANES), lambda i, j: (layer, 0, 0)),
                  pl.BlockSpec((tm, LANES), lambda i, j: (i, 0)),
                  pl.BlockSpec((tm, LANES), lambda i, j: (i, 0))],
        out_specs=[pl.BlockSpec((tm, tn), lambda i, j: (i, j)),
                   pl.BlockSpec((tm, LANES), lambda i, j: (i, 0))],
        out_shape=[jax.ShapeDtypeStruct((m, n), BF16),
                   jax.ShapeDtypeStruct((m, LANES), F32)],
        scratch_shapes=[pltpu.VMEM((tm, d), BF16)],
        compiler_params=_params(("parallel", "arbitrary"), blocks,
                                _nbytes((tm, d), BF16) + 3 * _nbytes((tm, tn), F32)),
        name="inproj",
    )(x, mods, gain, w_main, w_ba, cos, sin)


def _softmax_pv(s, sink_col, v):
    m = jnp.maximum(jnp.max(s, axis=-1, keepdims=True), sink_col)
    p = jnp.exp(s - m)
    denom = jnp.sum(p, axis=-1, keepdims=True) + jnp.exp(sink_col - m)
    return _dot((p * (1.0 / denom)).astype(BF16), v)


def _stack_heads(q, groups):
    return jnp.concatenate([q[:, g * HEAD_DIM:(g + 1) * HEAD_DIM] for g in range(groups)], axis=0)


def _unstack_heads(o, groups, rows):
    return jnp.concatenate([o[g * rows:(g + 1) * rows] for g in range(groups)], axis=1)


def _sink_col(sink_ref, kvh, groups, rows):
    return jnp.concatenate([jnp.full((rows, 1), sink_ref[kvh * groups + g], F32) for g in range(groups)], axis=0)


ATTN_PAIRS = 8


def _attn_kernel(sink_ref, q_ref, *refs, nsteps, groups, pairs):
    nwin = 2 * pairs + 2
    k_blocks, v_blocks = refs[:nwin], refs[nwin:2 * nwin]
    kx_ref, vx_ref, o_ref = refs[2 * nwin:]
    kvh, n = pl.program_id(1), pl.program_id(2)
    bl = ATTN_BLOCK
    row = lax.broadcasted_iota(jnp.int32, (2 * bl, 3 * bl), 0)
    col = lax.broadcasted_iota(jnp.int32, (2 * bl, 3 * bl), 1)
    sub = row // bl
    band = jnp.abs(col - bl - (row % bl)) <= WINDOW
    kx, vx = kx_ref[...], vx_ref[...]

    def operands(p):
        k = jnp.concatenate([r[...] for r in k_blocks[2 * p:2 * p + 4]] + [kx], axis=0)
        v = jnp.concatenate([r[...] for r in v_blocks[2 * p:2 * p + 4]] + [vx], axis=0)
        v_ones = jnp.concatenate([v, jnp.ones(v.shape, BF16)], axis=1)
        valid = band
        if p == 0:
            valid = valid & ((col >= bl) | (sub == 1) | (n > 0))
        if p == pairs - 1:
            valid = valid & ((col < 2 * bl) | (sub == 0) | (n < nsteps - 1))
        return k, v_ones, valid

    ops = [operands(p) for p in range(pairs)]

    def scores(p, g):
        k, _, valid = ops[p]
        q = q_ref[pl.ds(p * 2 * bl, 2 * bl), g * HEAD_DIM:(g + 1) * HEAD_DIM]
        s = _dot_nt(q, k) * (HEAD_DIM ** -0.5 * LOG2_E)
        local = jnp.concatenate([s[:bl, :3 * bl], s[bl:, bl:4 * bl]], axis=0)
        return jnp.concatenate([jnp.where(valid, local, NEG_INF), s[:, 4 * bl:]], axis=1)

    def attend(s, sink, v_ones):
        sink = sink * LOG2_E
        m = jnp.maximum(jnp.max(s, axis=-1, keepdims=True), sink)
        p = jnp.exp2(s - m).astype(BF16)
        zeros = jnp.zeros((bl, bl), BF16)
        p = jnp.concatenate([jnp.concatenate([p[:bl, :3 * bl], zeros, p[:bl, 3 * bl:]], axis=1),
                             jnp.concatenate([zeros, p[bl:, :3 * bl], p[bl:, 3 * bl:]], axis=1)], axis=0)
        pv = _dot(p, v_ones)
        denom = pv[:, HEAD_DIM:] + jnp.exp2(sink - m)
        return pv[:, :HEAD_DIM] * (1.0 / denom)

    units = [(p, g) for p in range(pairs) for g in range(groups)]
    pending = scores(*units[0])
    for i, (p, g) in enumerate(units):
        s = pending
        if i + 1 < len(units):
            pending = scores(*units[i + 1])
        o = attend(s, sink_ref[kvh * groups + g], ops[p][1])
        o_ref[pl.ds(p * 2 * bl, 2 * bl), g * HEAD_DIM:(g + 1) * HEAD_DIM] = o.astype(BF16)


def _attention(z_l, z_c, sink, *, batch, seq, ctx_len, d_model, kv_width):
    bl = ATTN_BLOCK
    nb = seq // bl
    pairs = ATTN_PAIRS if nb % (2 * ATTN_PAIRS) == 0 else 1
    per_step = 2 * pairs
    assert nb % per_step == 0
    nsteps = nb // per_step
    n_kv = kv_width // HEAD_DIM
    groups = d_model // kv_width
    gw = groups * HEAD_DIM
    kcol, vcol = d_model // HEAD_DIM, (d_model + kv_width) // HEAD_DIM
    shifts = range(-1, per_step + 1)

    def kv_spec(col0, shift):
        def imap(b, h, n):
            return (b * nb + jnp.clip(per_step * n + shift, 0, nb - 1), col0 + h)
        return pl.BlockSpec((bl, HEAD_DIM), imap)

    ctx_k = pl.BlockSpec((ctx_len, HEAD_DIM), lambda b, h, n: (b, kcol + h))
    ctx_v = pl.BlockSpec((ctx_len, HEAD_DIM), lambda b, h, n: (b, vcol + h))
    blocks = (2 * _nbytes((per_step * bl, gw), BF16) + 2 * len(shifts) * _nbytes((bl, HEAD_DIM), BF16)
              + 2 * _nbytes((ctx_len, HEAD_DIM), BF16))
    temps = 8 * pairs * _nbytes((2 * bl, 4 * bl + ctx_len), F32)
    return pl.pallas_call(
        functools.partial(_attn_kernel, nsteps=nsteps, groups=groups, pairs=pairs),
        grid=(batch, n_kv, nsteps),
        in_specs=[pl.BlockSpec(memory_space=pltpu.SMEM),
                  pl.BlockSpec((per_step * bl, gw), lambda b, h, n: (b * nsteps + n, h))]
                 + [kv_spec(kcol, shift) for shift in shifts]
                 + [kv_spec(vcol, shift) for shift in shifts]
                 + [ctx_k, ctx_v],
        out_specs=pl.BlockSpec((per_step * bl, gw), lambda b, h, n: (b * nsteps + n, h)),
        out_shape=jax.ShapeDtypeStruct((batch * seq, d_model), BF16),
        compiler_params=_params(("parallel", "parallel", "arbitrary"), blocks, temps),
        name="window_attention",
    )(sink, *([z_l] * (1 + 2 * len(shifts))), z_c, z_c)


def _attn_ctx_kernel(sink_ref, q_ref, kx_ref, vx_ref, o_ref, *, groups):
    kvh = pl.program_id(1)
    rows = q_ref.shape[0]
    q = _stack_heads(q_ref[...], groups)
    s = _dot_nt(q, kx_ref[...]) * (HEAD_DIM ** -0.5)
    o = _softmax_pv(s, _sink_col(sink_ref, kvh, groups, rows), vx_ref[...])
    o_ref[...] = _unstack_heads(o, groups, rows).astype(BF16)


def _attention_ctx(qkv_c, sink, *, batch, ctx_len, d_model, kv_width):
    n_kv = kv_width // HEAD_DIM
    groups = d_model // kv_width
    gw = groups * HEAD_DIM
    kcol, vcol = d_model // HEAD_DIM, (d_model + kv_width) // HEAD_DIM
    blocks = 2 * _nbytes((ctx_len, gw), BF16) + 2 * _nbytes((ctx_len, HEAD_DIM), BF16)
    temps = 6 * _nbytes((groups * ctx_len, ctx_len), F32)
    return pl.pallas_call(
        functools.partial(_attn_ctx_kernel, groups=groups),
        grid=(batch, n_kv),
        in_specs=[pl.BlockSpec(memory_space=pltpu.SMEM),
                  pl.BlockSpec((ctx_len, gw), lambda b, h: (b, h)),
                  pl.BlockSpec((ctx_len, HEAD_DIM), lambda b, h: (b, kcol + h)),
                  pl.BlockSpec((ctx_len, HEAD_DIM), lambda b, h: (b, vcol + h))],
        out_specs=pl.BlockSpec((ctx_len, gw), lambda b, h: (b, h)),
        out_shape=jax.ShapeDtypeStruct((batch * ctx_len, d_model), BF16),
        compiler_params=_params(("parallel", "parallel"), blocks, temps),
        name="context_attention",
    )(sink, qkv_c, qkv_c, qkv_c)


CONV_PAD = SUBLANES
CONV_ROWS = 256


def _conv_seq(x_ref, w_ref, o_ref, pad_ref, l2_flag, l2_scale, lane0):
    s = x_ref.shape[0]
    lanes = slice(lane0, lane0 + LANES)
    w = w_ref[:, lanes]
    zeros = jnp.zeros((CONV_PAD, LANES), F32)
    pad_ref[pl.ds(0, CONV_PAD), :] = zeros
    pad_ref[pl.ds(CONV_PAD, s), :] = x_ref[:, lanes].astype(F32)
    pad_ref[pl.ds(CONV_PAD + s, CONV_PAD), :] = zeros
    rows = min(CONV_ROWS, s)

    def conv_silu(r0):
        y = jnp.zeros((rows, LANES), F32)
        for k in range(CONV_K):
            off = CONV_PAD - CONV_K // 2 + k
            y = y + pad_ref[pl.ds(r0 + off, rows), :] * w[k:k + 1, :]
        return _silu(y)

    def body_l2(it, carry):
        r0 = pl.multiple_of(it * rows, rows)
        y = conv_silu(r0)
        inv = lax.rsqrt(jnp.sum(y * y, axis=-1, keepdims=True) + EPS) * l2_scale
        o_ref[pl.ds(r0, rows), lanes] = (y * inv).astype(BF16)
        return carry

    def body_plain(it, carry):
        r0 = pl.multiple_of(it * rows, rows)
        o_ref[pl.ds(r0, rows), lanes] = conv_silu(r0).astype(BF16)
        return carry

    trips = s // rows
    unroll = 2 if trips % 2 == 0 else 1

    @pl.when(l2_flag)
    def _():
        lax.fori_loop(0, trips, body_l2, 0, unroll=unroll)

    @pl.when(jnp.logical_not(l2_flag))
    def _():
        lax.fori_loop(0, trips, body_plain, 0, unroll=unroll)


PREP_COLS = 2


def _dnprep_kernel(xl_ref, xc_ref, w_ref, ol_ref, oc_ref, padl_ref, padc_ref, *, n_q, cols):
    col = pl.program_id(1) * cols
    l2_flag = col < 2 * n_q
    l2_scale = jnp.where(col < n_q, HEAD_DIM ** -0.5, 1.0)
    for h in range(cols):
        _conv_seq(xl_ref, w_ref, ol_ref, padl_ref, l2_flag, l2_scale, h * LANES)
        _conv_seq(xc_ref, w_ref, oc_ref, padc_ref, l2_flag, l2_scale, h * LANES)


def _dn_prep(z_l, z_c, conv_w, *, col0, batch, seq, ctx_len, d_model):
    n_q = d_model // LANES
    cols = PREP_COLS if n_q % PREP_COLS == 0 and col0 % PREP_COLS == 0 else 1
    ncol = 3 * n_q // cols
    width = cols * LANES
    first = col0 // cols
    blocks = 2 * (_nbytes((seq, width), BF16) + _nbytes((ctx_len, width), BF16)) + _nbytes((8, width), F32)
    scratch = _nbytes((seq + 2 * CONV_PAD, LANES), F32) + _nbytes((ctx_len + 2 * CONV_PAD, LANES), F32)
    return pl.pallas_call(
        functools.partial(_dnprep_kernel, n_q=n_q, cols=cols),
        grid=(batch, ncol),
        in_specs=[pl.BlockSpec((seq, width), lambda b, c: (b, first + c)),
                  pl.BlockSpec((ctx_len, width), lambda b, c: (b, first + c)),
                  pl.BlockSpec((CONV_K, width), lambda b, c: (0, c))],
        out_specs=[pl.BlockSpec((seq, width), lambda b, c: (b, c)),
                   pl.BlockSpec((ctx_len, width), lambda b, c: (b, c))],
        out_shape=[jax.ShapeDtypeStruct((batch * seq, 3 * d_model), BF16),
                   jax.ShapeDtypeStruct((batch * ctx_len, 3 * d_model), BF16)],
        scratch_shapes=[pltpu.VMEM((seq + 2 * CONV_PAD, LANES), F32),
                        pltpu.VMEM((ctx_len + 2 * CONV_PAD, LANES), F32)],
        compiler_params=_params(("parallel", "parallel"), blocks, scratch + 16 * _nbytes((CONV_ROWS, LANES), F32)),
        name="deltanet_prep",
    )(z_l, z_c, conv_w)


def _gates_kernel(ba_ref, alog_ref, dtb_ref, o_ref, *, n_heads):
    tm = ba_ref.shape[0]
    x = ba_ref[...]
    lane = lax.broadcasted_iota(jnp.int32, (CHUNK, LANES), 1)
    z = x + dtb_ref[...]
    softplus = jnp.maximum(z, 0.0) + jnp.log(1.0 + jnp.exp(-jnp.abs(z)))
    g = -jnp.exp(alog_ref[...]) * softplus
    beta = _sigmoid(x)
    r = lax.broadcasted_iota(jnp.int32, (CHUNK, CHUNK), 0)
    c = lax.broadcasted_iota(jnp.int32, (CHUNK, CHUNK), 1)
    tri_prefix = (c <= r).astype(BF16)
    tri_suffix = (c >= r).astype(BF16)
    for i in range(tm // CHUNK):
        rows = slice(i * CHUNK, (i + 1) * CHUNK)
        parts = _split3(g[rows])
        pre = sum(_dot(tri_prefix, p) for p in parts)
        suf = sum(_dot(tri_suffix, p) for p in parts)
        csum = jnp.where(lane < 3 * n_heads, pre, suf)
        o_ref[rows, :] = jnp.where(lane < 2 * n_heads, beta[rows], csum)


def _gates(ba, alog_vec, dtb_vec, *, n_heads):
    m = ba.shape[0]
    tm = _pick(m, (1024, 512, 256, 128))
    blocks = 2 * _nbytes((tm, LANES), F32)
    return pl.pallas_call(
        functools.partial(_gates_kernel, n_heads=n_heads),
        grid=(m // tm,),
        in_specs=[pl.BlockSpec((tm, LANES), lambda i: (i, 0)),
                  pl.BlockSpec((1, LANES), lambda i: (0, 0)),
                  pl.BlockSpec((1, LANES), lambda i: (0, 0))],
        out_specs=pl.BlockSpec((tm, LANES), lambda i: (i, 0)),
        out_shape=jax.ShapeDtypeStruct((m, LANES), F32),
        compiler_params=_params(("parallel",), blocks, 8 * _nbytes((tm, LANES), F32)),
        name="deltanet_gates",
    )(ba, alog_vec, dtb_vec)


def _dot_split(a, b):
    n = b[0].shape[1]
    lhs = jnp.concatenate(a, axis=1)
    half = jnp.concatenate(b, axis=1)
    out = _dot(lhs, jnp.concatenate([half, half], axis=0))
    return out[:, :n] + out[:, n:]


def _unit_tri_inverse_many(mats, hooks=()):
    n = mats[0].shape[0]
    row = lax.broadcasted_iota(jnp.int32, (n, n), 0)
    col = lax.broadcasted_iota(jnp.int32, (n, n), 1)
    splits = [_split2(a) for a in mats]
    eye = jnp.where(row == col, 1.0, 0.0)
    pair = (row // 2) == (col // 2)
    xs = [eye - jnp.where(pair, a, 0.0) for a in mats]
    s, level = 2, 0
    while s < n:
        couple = ((row // (2 * s)) == (col // (2 * s))) & ((row // s) != (col // s))
        zero16 = jnp.zeros((n, n), BF16)
        es = [(jnp.where(couple, hi, zero16), jnp.where(couple, lo, zero16)) for hi, lo in splits]
        xsp = [_split2(x) for x in xs]
        exs = [_split2(_dot_split(e, xp)) for e, xp in zip(es, xsp)]
        xs = [x - _dot_split(xp, ex) for x, xp, ex in zip(xs, xsp, exs)]
        if level < len(hooks):
            hooks[level]()
        s *= 2
        level += 1
    return xs


def _scan_prep_group(refs, chunks, dsts, sel, scr, hooks=()):
    u_s, w_s, qk_s, qd_s, kdt_s, gl_s = scr
    c = CHUNK
    group = len(chunks)
    per_chunk = refs if isinstance(refs, list) else [refs] * group
    hooks = list(hooks)
    if hooks:
        hooks.pop(0)()
    row = lax.broadcasted_iota(jnp.int32, (c, c), 0)
    col = lax.broadcasted_iota(jnp.int32, (c, c), 1)
    eye16 = (row == col).astype(BF16)
    r0s = [pl.multiple_of(ci * c, c) for ci in chunks]
    q16s = [r[0][pl.ds(r0, c), :] for r, r0 in zip(per_chunk, r0s)]
    k16s = [r[1][pl.ds(r0, c), :] for r, r0 in zip(per_chunk, r0s)]
    qs = [q.astype(F32) for q in q16s]
    ks = [k.astype(F32) for k in k16s]
    vs = [r[2][pl.ds(r0, c), :].astype(F32) for r, r0 in zip(per_chunk, r0s)]
    sel2 = jnp.concatenate([sel, sel], axis=0)
    parts = [_split3(r[3][pl.ds(r0, c), :]) for r, r0 in zip(per_chunk, r0s)]
    gates = [_dot(jnp.concatenate([hi, mid], axis=1), sel2) + _dot(lo, sel) for hi, mid, lo in parts]
    kks = [_dot_nt(k16, k16) for k16 in k16s]
    qks = [_dot_nt(q16, k16) for q16, k16 in zip(q16s, k16s)]
    probs = [(g, d) for g in range(group) for d in range(2)]
    betas = [gates[g][:, d * LANES:(d + 1) * LANES] for g, d in probs]
    ccols = [gates[g][:, (2 + d) * LANES:(3 + d) * LANES] for g, d in probs]
    decays, mats = [], []
    for (g, d), beta, ccol in zip(probs, betas, ccols):
        crow = per_chunk[g][4][d][0, :, pl.ds(r0s[g], c)]
        incl = (row >= col) if d == 0 else (row <= col)
        strict = (row > col) if d == 0 else (row < col)
        decay = jnp.exp(jnp.where(incl, ccol - crow, NEG_INF))
        decays.append(decay)
        mats.append(jnp.where(strict, beta * kks[g] * decay, 0.0))
    if hooks:
        hooks.pop(0)()
    xsp = [_split2(x) for x in _unit_tri_inverse_many(mats, hooks)]
    expcs = [jnp.exp(ccol) for ccol in ccols]
    rus = [_split2(vs[g] * beta) for (g, d), beta in zip(probs, betas)]
    us = [_dot_split(xp, ru) for xp, ru in zip(xsp, rus)]
    rws = [(ks[g] * beta * expc).astype(BF16) for (g, d), beta, expc in zip(probs, betas, expcs)]
    ws = [_dot(jnp.concatenate(xp, axis=1), jnp.concatenate([rw, rw], axis=0)) for xp, rw in zip(xsp, rws)]
    clasts = [ccol[c - 1:c, :] if d == 0 else ccol[0:1, :] for (g, d), ccol in zip(probs, ccols)]
    kdts = [_dot_nt(eye16, (ks[g] * jnp.exp(clast - ccol)).astype(BF16))
            for (g, d), clast, ccol in zip(probs, clasts, ccols)]
    for i, (g, d) in enumerate(probs):
        rows = pl.ds(pl.multiple_of(dsts[g] * c, c), c)
        u_s[d, rows, :] = us[i]
        w_s[d, rows, :] = ws[i].astype(BF16)
        qk_s[d, rows, :] = (qks[g] * decays[i]).astype(BF16)
        qd_s[d, rows, :] = (qs[g] * expcs[i]).astype(BF16)
        kdt_s[d, rows, :] = kdts[i].astype(BF16)
        gl_s[d, pl.ds(pl.multiple_of(dsts[g] * SUBLANES, SUBLANES), SUBLANES), :] = jnp.broadcast_to(
            jnp.exp(clasts[i]), (SUBLANES, LANES))


def _step_issue(chunks, states, scr):
    u_s, w_s, qk_s, qd_s, kdt_s, gl_s = scr
    c = CHUNK
    rows = [pl.ds(pl.multiple_of(ci * c, c), c) for ci in chunks]
    dirs = range(2)
    ws = [_dot(jnp.concatenate([w_s[d, rows[d], :], qd_s[d, rows[d], :]], axis=0), states[d].astype(BF16))
          for d in dirs]
    return dict(chunks=chunks, rows=rows, states=states, ws=ws)


def _step_finish(pending, scr):
    u_s, w_s, qk_s, qd_s, kdt_s, gl_s = scr
    c = CHUNK
    chunks, rows, states, ws = pending["chunks"], pending["rows"], pending["states"], pending["ws"]
    dirs = range(2)
    v_new = [(u_s[d, rows[d], :] - ws[d][:c]).astype(BF16) for d in dirs]
    outs = [ws[d][c:] + _dot(qk_s[d, rows[d], :], v_new[d]) for d in dirs]
    gls = [gl_s[d, pl.ds(pl.multiple_of(chunks[d] * SUBLANES, SUBLANES), SUBLANES), :][0:1, :] for d in dirs]
    new_states = [states[d] * gls[d] + _dot(kdt_s[d, rows[d], :], v_new[d]) for d in dirs]
    return tuple(new_states), outs


def _scan_step_pair(chunks, states, scr):
    return _step_finish(_step_issue(chunks, states, scr), scr)


PREP_GROUP = 4


def _scan_kernel(ql, kl, vl, pl_l, crfl, crbl, zl_prev, qx, kx, vx, px, crfx, crbx, zx, gain_ref,
                 outl_prev_ref, outx_ref, u_s, w_s, qk_s, qd_s, kdt_s, gl_s, o_s, st_s, *, n_heads):
    b, j = pl.program_id(0), pl.program_id(1)
    c = CHUNK
    ncx, ncl = qx.shape[0] // c, ql.shape[0] // c
    half = PREP_GROUP // 2
    n_groups = ncl // PREP_GROUP
    assert PREP_GROUP == 4 and ncx % 2 == 0 and ncl % PREP_GROUP == 0 and ncx <= PREP_GROUP
    cur = j % 2
    factor_refs = (u_s, w_s, qk_s, qd_s, kdt_s, gl_s)
    scr_cur, o_cur = tuple(r.at[cur] for r in factor_refs), o_s.at[cur]
    scr_prev, o_prev = tuple(r.at[1 - cur] for r in factor_refs), o_s.at[1 - cur]
    gain = gain_ref[...]

    def chunk_rows(i):
        return pl.ds(pl.multiple_of(i * c, c), c)

    @pl.when((b == 0) & (j == 0))
    def _():
        def body(i, carry):
            for r in factor_refs[:5]:
                for d in range(2):
                    r[1, d, chunk_rows(i), :] = jnp.zeros((c, LANES), r.dtype)
            o_s[1, chunk_rows(i), :] = jnp.zeros((c, LANES), F32)
            return carry
        lax.fori_loop(0, ncx + ncl, body, 0)
        gl_s[1] = jnp.zeros(gl_s.shape[1:], F32)
        st_s[...] = jnp.zeros(st_s.shape, F32)

    def finish(z_ref, out_ref, i, o):
        on = o * lax.rsqrt(jnp.mean(o * o, axis=-1, keepdims=True) + EPS) * gain
        out_ref[chunk_rows(i), :] = (on * _silu(z_ref[chunk_rows(i), :].astype(F32))).astype(BF16)

    def step_stages(scr, o_ref, base, n, it, box, key, z_ref=None, out_ref=None):
        pending = {}
        jf, jb = it, n - 1 - it

        def issue():
            pending.update(_step_issue((base + jf, base + jb), box[key], scr))

        def complete():
            box[key], outs = _step_finish(pending, scr)
            if z_ref is None:
                o_ref[chunk_rows(base + jf), :] = outs[0]
                o_ref[chunk_rows(base + jb), :] = outs[1]
            else:
                finish(z_ref, out_ref, jf, outs[0] + o_ref[chunk_rows(base + jf), :])
                finish(z_ref, out_ref, jb, outs[1] + o_ref[chunk_rows(base + jb), :])
        return [issue, complete]

    def run(stages):
        for stage in stages:
            stage()

    def front(i):
        return [half * i + k for k in range(half)]

    def drain_stages(p, box):
        if isinstance(p, int) and p < half:
            return step_stages(scr_prev, o_prev, ncx, ncl, front(n_groups - 1)[p], box, "prev")
        return step_stages(scr_prev, o_prev, ncx, ncl, ncl // 2 + p - half, box, "prev", zl_prev, outl_prev_ref)

    n_drain = half + ncl // 2

    @pl.when(j < n_heads)
    def _():
        r = lax.broadcasted_iota(jnp.int32, (LANES, 4 * LANES), 0)
        t = lax.broadcasted_iota(jnp.int32, (LANES, 4 * LANES), 1) // LANES
        sel = (r == t * n_heads + j).astype(BF16)
        zero = jnp.zeros((HEAD_DIM, HEAD_DIM), F32)
        box = {"cur": (zero, zero), "prev": (st_s[0], st_s[1])}

        refs_x = (qx, kx, vx, px, (crfx, crbx))
        refs_l = (ql, kl, vl, pl_l, (crfl, crbl))

        def latent_chunks(i):
            return front(i) + [ncl - 1 - ci for ci in reversed(front(i))]

        def prep_latent(i, hooks):
            chunks = latent_chunks(i)
            _scan_prep_group(refs_l, chunks, [ncx + ci for ci in chunks], sel, scr_cur, hooks)

        first = latent_chunks(0)
        _scan_prep_group([refs_x] * ncx + [refs_l] * len(first), list(range(ncx)) + first,
                         list(range(ncx)) + [ncx + ci for ci in first], sel, scr_cur,
                         [stage for p in range(2 * half) for stage in drain_stages(p, box)])
        for it in range(ncx):
            ends = (zx, outx_ref) if it >= ncx // 2 else ()
            run(step_stages(scr_cur, o_cur, 0, ncx, it, box, "cur", *ends))

        def overlapped(i, carry):
            box = {"cur": carry[:2], "prev": carry[2:]}
            hooks = []
            for k, it in enumerate(front(i - 1)):
                own = step_stages(scr_cur, o_cur, ncx, ncl, it, box, "cur")
                other = drain_stages(2 * half + half * (i - 1) + k, box)
                hooks += [own[0], other[0], own[1], other[1]]
            prep_latent(i, hooks)
            return tuple(box["cur"]) + tuple(box["prev"])

        carry = lax.fori_loop(1, n_groups, overlapped, tuple(box["cur"]) + tuple(box["prev"]))
        st_s[0] = carry[0]
        st_s[1] = carry[1]

    @pl.when(j == n_heads)
    def _():
        box = {"prev": (st_s[0], st_s[1])}
        for p in range(half):
            run(drain_stages(p, box))

        def body(p, states):
            inner = {"prev": states}
            run(drain_stages(p, inner))
            return tuple(inner["prev"])
        lax.fori_loop(half, n_drain, body, tuple(box["prev"]))


def _delta_scan(dn_l, dn_c, gates_l, gates_c, gates_lt, gates_ct, z_l, z_c, gain,
                *, zcol, batch, seq, ctx_len, d_model):
    n_heads = d_model // HEAD_DIM
    t = seq + ctx_len
    last = n_heads - 1

    def col_spec(rows, col0, lag=0):
        return pl.BlockSpec((rows, LANES), lambda b, j: (b, col0 + jnp.clip(j - lag, 0, last)))

    def gate_spec(rows):
        return pl.BlockSpec((rows, LANES), lambda b, j: (b, 0))

    def crow_spec(rows, d):
        return pl.BlockSpec((1, 1, rows), lambda b, j: ((2 + d) * n_heads + jnp.minimum(j, last), 0, b))

    def side(rows, z_lag):
        return [col_spec(rows, 0), col_spec(rows, n_heads), col_spec(rows, 2 * n_heads), gate_spec(rows),
                crow_spec(rows, 0), crow_spec(rows, 1), col_spec(rows, zcol, z_lag)]

    blocks = (_nbytes((t, LANES), F32) + 2 * _nbytes((SUBLANES, t), F32) + 5 * _nbytes((t, LANES), BF16))
    slots = 2
    scratch_shapes = [pltpu.VMEM((slots, 2, t, LANES), F32),
                      pltpu.VMEM((slots, 2, t, LANES), BF16),
                      pltpu.VMEM((slots, 2, t, LANES), BF16),
                      pltpu.VMEM((slots, 2, t, LANES), BF16),
                      pltpu.VMEM((slots, 2, t, LANES), BF16),
                      pltpu.VMEM((slots, 2, (t // CHUNK) * SUBLANES, LANES), F32),
                      pltpu.VMEM((slots, t, LANES), F32),
                      pltpu.VMEM((2, HEAD_DIM, HEAD_DIM), F32)]
    scratch = slots * (3 * _nbytes((t, LANES), F32) + 4 * 2 * _nbytes((t, LANES), BF16))
    return pl.pallas_call(
        functools.partial(_scan_kernel, n_heads=n_heads),
        grid=(batch, n_heads + 1),
        in_specs=side(seq, 1) + side(ctx_len, 0) + [pl.BlockSpec((1, LANES), lambda b, j: (0, 0))],
        out_specs=[pl.BlockSpec((seq, LANES), lambda b, j: (b, jnp.clip(j - 1, 0, last))),
                   pl.BlockSpec((ctx_len, LANES), lambda b, j: (b, jnp.minimum(j, last)))],
        out_shape=[jax.ShapeDtypeStruct((batch * seq, d_model), BF16),
                   jax.ShapeDtypeStruct((batch * ctx_len, d_model), BF16)],
        scratch_shapes=scratch_shapes,
        compiler_params=_params(("arbitrary", "arbitrary"), blocks, scratch + 4 * 1024 * 1024),
        name="delta_scan",
    )(dn_l, dn_l, dn_l, gates_l, gates_lt, gates_lt, z_l,
      dn_c, dn_c, dn_c, gates_c, gates_ct, gates_ct, z_c, gain)


def _merge_kernel(att_ref, dn_ref, ga_ref, gd_ref, wa_ref, wd_ref, o_ref):
    tm = o_ref.shape[0]
    sub = min(tm, INPROJ_SUB_ROWS)
    for r in range(tm // sub):
        rows = pl.ds(r * sub, sub)
        y = (_sigmoid(ga_ref[rows, :].astype(F32)) * _dot(att_ref[rows, :], wa_ref[...])
             + _sigmoid(gd_ref[rows, :].astype(F32)) * _dot(dn_ref[rows, :], wd_ref[...]))
        o_ref[rows, :] = y.astype(BF16)


def _merge(att, dn, z, w_attn, w_delta, layer, *, gate_col, tm, tn):
    m, d = att.shape
    ga0, gd0 = gate_col // tn, (gate_col + d) // tn
    blocks = (2 * _nbytes((tm, d), BF16) + 2 * _nbytes((tm, tn), BF16) + 2 * _nbytes((d, tn), BF16)
              + _nbytes((tm, tn), BF16))
    return pl.pallas_call(
        _merge_kernel,
        grid=(m // tm, d // tn),
        in_specs=[pl.BlockSpec((tm, d), lambda i, j: (i, 0)),
                  pl.BlockSpec((tm, d), lambda i, j: (i, 0)),
                  pl.BlockSpec((tm, tn), lambda i, j: (i, ga0 + j)),
                  pl.BlockSpec((tm, tn), lambda i, j: (i, gd0 + j)),
                  pl.BlockSpec((None, d, tn), lambda i, j: (layer, 0, j)),
                  pl.BlockSpec((None, d, tn), lambda i, j: (layer, 0, j))],
        out_specs=pl.BlockSpec((tm, tn), lambda i, j: (i, j)),
        out_shape=jax.ShapeDtypeStruct((m, d), BF16),
        compiler_params=_params(("parallel", "arbitrary"), blocks, 4 * _nbytes((tm, tn), F32)),
        name="merge",
    )(att, dn, z, z, w_attn, w_delta)


def _resid_kernel(a_ref, w_ref, x_ref, mod_ref, o_ref, *, gate_row):
    o_ref[...] = x_ref[...] + mod_ref[0, gate_row:gate_row + 1, :] * _dot(a_ref[...], w_ref[...])


def _resid_matmul(a, w, layer, x, mods, mod_row, gate_row, *, tm, tn, name):
    m, k = a.shape
    d = w.shape[2]
    blocks = (_nbytes((tm, k), BF16) + _nbytes((k, tn), BF16) + 2 * _nbytes((tm, tn), F32)
              + _nbytes((6, tn), F32))
    return pl.pallas_call(
        functools.partial(_resid_kernel, gate_row=gate_row),
        grid=(m // tm, d // tn),
        in_specs=[pl.BlockSpec((tm, k), lambda i, j: (i, 0)),
                  pl.BlockSpec((None, k, tn), lambda i, j: (layer, 0, j)),
                  pl.BlockSpec((tm, tn), lambda i, j: (i, j)),
                  pl.BlockSpec((1, 6, tn), lambda i, j: (mod_row(i, tm), 0, j))],
        out_specs=pl.BlockSpec((tm, tn), lambda i, j: (i, j)),
        out_shape=jax.ShapeDtypeStruct((m, d), F32),
        compiler_params=_params(("parallel", "arbitrary"), blocks, 2 * _nbytes((tm, tn), F32)),
        name=name,
    )(a, w, x, mods)


def _ffn_up_kernel(x_ref, mod_ref, g_ref, wg_ref, wu_ref, o_ref, h_ref):
    j = pl.program_id(1)
    tm = o_ref.shape[0]
    sub = min(tm, INPROJ_SUB_ROWS)

    def tile(fresh):
        for r in range(tm // sub):
            rows = pl.ds(r * sub, sub)
            if fresh:
                h = _norm_mod(x_ref[rows, :], g_ref[...], mod_ref[0, 3:4, :], mod_ref[0, 4:5, :]).astype(BF16)
                h_ref[rows, :] = h
            else:
                h = h_ref[rows, :]
            o_ref[rows, :] = (_silu(_dot(h, wg_ref[...])) * _dot(h, wu_ref[...])).astype(BF16)

    @pl.when(j == 0)
    def _():
        tile(True)

    @pl.when(j > 0)
    def _():
        tile(False)


def _ffn_up(x, mods, mod_row, gain, w_gate_up, layer, *, tm, tn):
    m, d = x.shape
    f = w_gate_up.shape[2] // 2
    nj = f // tn
    blocks = (_nbytes((tm, d), F32) + _nbytes((6, d), F32) + 2 * _nbytes((d, tn), BF16)
              + _nbytes((tm, tn), BF16))
    return pl.pallas_call(
        _ffn_up_kernel,
        grid=(m // tm, nj),
        in_specs=[pl.BlockSpec((tm, d), lambda i, j: (i, 0)),
                  pl.BlockSpec((1, 6, d), lambda i, j: (mod_row(i, tm), 0, 0)),
                  pl.BlockSpec((1, d), lambda i, j: (0, 0)),
                  pl.BlockSpec((None, d, tn), lambda i, j: (layer, 0, j)),
                  pl.BlockSpec((None, d, tn), lambda i, j: (layer, 0, nj + j))],
        out_specs=pl.BlockSpec((tm, tn), lambda i, j: (i, j)),
        out_shape=jax.ShapeDtypeStruct((m, f), BF16),
        scratch_shapes=[pltpu.VMEM((tm, d), BF16)],
        compiler_params=_params(("parallel", "arbitrary"), blocks,
                                _nbytes((tm, d), BF16) + 4 * _nbytes((tm, tn), F32)),
        name="ffn_up",
    )(x, mods, gain, w_gate_up, w_gate_up)


def _final_norm_kernel(x_ref, g_ref, o_ref):
    x = x_ref[...]
    o_ref[...] = x * lax.rsqrt(jnp.mean(x * x, axis=-1, keepdims=True) + EPS) * g_ref[...]


def _final_norm(x, gain, *, tm):
    m, d = x.shape
    return pl.pallas_call(
        _final_norm_kernel,
        grid=(m // tm,),
        in_specs=[pl.BlockSpec((tm, d), lambda i: (i, 0)), pl.BlockSpec((1, d), lambda i: (0, 0))],
        out_specs=pl.BlockSpec((tm, d), lambda i: (i, 0)),
        out_shape=jax.ShapeDtypeStruct((m, d), F32),
        compiler_params=_params(("parallel",), 2 * _nbytes((tm, d), F32), 2 * _nbytes((tm, d), F32)),
        name="final_norm",
    )(x, gain)


def _rope_tables(seq):
    quarter = HEAD_DIM // 4
    pos = jnp.arange(seq)
    row = (pos // GRID_W).astype(F32)
    col = (pos % GRID_W).astype(F32)
    inv_freq = ROPE_BASE ** (-jnp.arange(quarter, dtype=F32) / quarter)
    ang_r, ang_c = row[:, None] * inv_freq[None], col[:, None] * inv_freq[None]
    cos = jnp.concatenate([jnp.cos(ang_r), jnp.cos(ang_c)] * 2, axis=1)
    sin = jnp.concatenate([-jnp.sin(ang_r), -jnp.sin(ang_c), jnp.sin(ang_r), jnp.sin(ang_c)], axis=1)
    return cos, sin


def _rope_perm():
    quarter = HEAD_DIM // 4
    order = (0, 2, 1, 3)
    return jnp.concatenate([jnp.arange(quarter) + quarter * o for o in order])


def kernel(x, c, ctx, c_ctx, w_mod, b_mod, norm1, norm2, w_in, conv_w, attn_sink, a_log, dt_bias, dn_norm,
           w_attn_proj, w_delta_proj, w_out, w_gate_up, w_down, final_norm):
    batch, seq, d = x.shape
    ctx_len = ctx.shape[1]
    depth = w_mod.shape[0]
    n_heads = d // HEAD_DIM
    kv_width = (n_heads // 4) * HEAD_DIM
    qkv_width = d + 2 * kv_width
    assert seq % CHUNK == 0 and ctx_len % CHUNK == 0 and seq % ATTN_BLOCK == 0
    assert 4 * n_heads <= LANES and (batch * seq) % ctx_len == 0

    off_ba = qkv_width + 4 * d
    qk_width = d + kv_width
    w_qk = w_in[:, :, :qk_width].reshape(depth, d, qk_width // HEAD_DIM, HEAD_DIM)[..., _rope_perm()]
    w_main = jnp.concatenate([w_qk.reshape(depth, d, qk_width), w_in[:, :, qk_width:off_ba],
                              w_in[:, :, off_ba + 4 * n_heads:]], axis=2).astype(BF16)
    w_ba = jnp.pad(w_in[:, :, off_ba:off_ba + 4 * n_heads], ((0, 0), (0, 0), (0, LANES - 4 * n_heads))).astype(BF16)
    w_attn16, w_delta16, w_out16 = w_attn_proj.astype(BF16), w_delta_proj.astype(BF16), w_out.astype(BF16)
    w_gu16, w_down16 = w_gate_up.astype(BF16), w_down.astype(BF16)

    lane_pad = ((0, 0), (2 * n_heads, LANES - 4 * n_heads))
    alog_vec = jnp.pad(a_log.reshape(depth, 2 * n_heads), lane_pad).reshape(depth, 1, LANES)
    dtb_vec = jnp.pad(dt_bias.reshape(depth, 2 * n_heads), lane_pad).reshape(depth, 1, LANES)

    cos_l, sin_l = _rope_tables(seq)
    cos_l, sin_l = jnp.tile(cos_l, (batch, 1)), jnp.tile(sin_l, (batch, 1))
    cos_c = jnp.ones((batch * ctx_len, LANES), F32)
    sin_c = jnp.zeros((batch * ctx_len, LANES), F32)

    mod_rows = ((batch + 1 + SUBLANES - 1) // SUBLANES) * SUBLANES
    cc = jnp.concatenate([c, c_ctx[None], jnp.zeros((mod_rows - batch - 1, d), F32)], axis=0)
    mods_all = _modulation(cc, w_mod, b_mod).reshape(depth, mod_rows, 6, d)

    tm_l = _pick(seq, (1024, 512, 256, 128))
    tm_c = _pick(batch * ctx_len, (1024, 512, 256, 128))
    tn = _pick(kv_width, (512, 256, 128))
    tn_in = _pick(w_main.shape[2], (1024, 512, 256, 128))

    def row_l(i, tm):
        return (i * tm) // seq

    def row_c(i, tm):
        return batch

    xl = x.reshape(batch * seq, d)
    xc = ctx.reshape(batch * ctx_len, d)
    dims = dict(batch=batch, seq=seq, ctx_len=ctx_len, d_model=d)

    for l in range(depth):
        mods = mods_all[l]
        need_ctx = l < depth - 1
        g1, g2 = norm1[l][None], norm2[l][None]
        proj = functools.partial(_inproj, gain=g1, w_main=w_main, w_ba=w_ba, layer=l, tn=tn_in,
                                 qk_width=d + kv_width)
        z_l, ba_l = proj(xl, mods, row_l, cos=cos_l, sin=sin_l, tm=tm_l)
        z_c, ba_c = proj(xc, mods, row_c, cos=cos_c, sin=sin_c, tm=tm_c)

        att_l = _attention(z_l, z_c, attn_sink[l], kv_width=kv_width, **dims)

        dn_l, dn_c = _dn_prep(z_l, z_c, conv_w[l], col0=qkv_width // LANES, **dims)
        gates_l = _gates(ba_l, alog_vec[l], dtb_vec[l], n_heads=n_heads)
        gates_c = _gates(ba_c, alog_vec[l], dtb_vec[l], n_heads=n_heads)
        gates_lt = gates_l.T.reshape(LANES, 1, batch * seq)
        gates_ct = gates_c.T.reshape(LANES, 1, batch * ctx_len)
        do_l, do_c = _delta_scan(dn_l, dn_c, gates_l, gates_c, gates_lt, gates_ct, z_l, z_c,
                                 dn_norm[l][None], zcol=(qkv_width + 3 * d) // LANES, **dims)

        streams = [(xl, att_l, do_l, z_l, row_l, tm_l)]
        if need_ctx:
            att_c = _attention_ctx(z_c, attn_sink[l], batch=batch, ctx_len=ctx_len, d_model=d, kv_width=kv_width)
            streams.append((xc, att_c, do_c, z_c, row_c, tm_c))
        outs = []
        for xs, att, dn, z, row, tm in streams:
            y = _merge(att, dn, z, w_attn16, w_delta16, l, gate_col=qkv_width + 4 * d, tm=tm, tn=tn)
            xs = _resid_matmul(y, w_out16, l, xs, mods, row, 2, tm=tm, tn=tn, name="out_proj")
            act = _ffn_up(xs, mods, row, g2, w_gu16, l, tm=tm, tn=tn)
            xs = _resid_matmul(act, w_down16, l, xs, mods, row, 5, tm=tm, tn=tn, name="ffn_down")
            outs.append(xs)
        xl = outs[0]
        if need_ctx:
            xc = outs[1]

    return _final_norm(xl, final_norm[None], tm=tm_l).reshape(batch, seq, d)
```

```python
import functools

import jax
import jax.numpy as jnp
from jax import lax
from jax.experimental import pallas as pl
from jax.experimental.pallas import tpu as pltpu

F32 = jnp.float32
BF16 = jnp.bfloat16

HEAD_DIM = 128
WINDOW = 128
ATTN_BLOCK = 128
GRID_W = 64
CONV_K = 5
ROPE_BASE = 10000.0
EPS = 1e-6
NEG_INF = -1e30
LOG2_E = 1.4426950408889634
CHUNK = 128
LANES = 128
SUBLANES = 8
VMEM_CAP_BYTES = 60 * 1024 * 1024


def _vmem_limit(block_bytes, scratch_bytes=0):
    est = 2 * block_bytes + scratch_bytes
    return int(min(VMEM_CAP_BYTES, max(32 * 1024 * 1024, est * 3 // 2)))


def _params(sem, block_bytes, scratch_bytes=0):
    return pltpu.CompilerParams(dimension_semantics=sem,
                                vmem_limit_bytes=_vmem_limit(block_bytes, scratch_bytes))


def _pick(n, candidates):
    for c in candidates:
        if n % c == 0:
            return c
    raise ValueError(f"no tile in {candidates} divides {n}")


def _nbytes(shape, dtype):
    n = 1
    for s in shape:
        n *= s
    return n * jnp.dtype(dtype).itemsize


def _sigmoid(x):
    return 1.0 / (1.0 + jnp.exp(-x))


def _silu(x):
    return x * _sigmoid(x)


def _dot(a, b):
    return jnp.dot(a, b, preferred_element_type=F32)


def _dot_nt(a, b):
    return lax.dot_general(a, b, (((1,), (1,)), ((), ())), preferred_element_type=F32)


def _split2(x):
    hi = x.astype(BF16)
    lo = (x - hi.astype(F32)).astype(BF16)
    return hi, lo


def _split3(x):
    hi = x.astype(BF16)
    r = x - hi.astype(F32)
    mid = r.astype(BF16)
    lo = (r - mid.astype(F32)).astype(BF16)
    return hi, mid, lo


def _mod_kernel(c_ref, w_ref, b_ref, o_ref):
    s = _silu(c_ref[...])
    s_hi, s_lo = _split2(s)
    w = w_ref[0]
    w_hi, w_lo = _split2(w)
    acc = _dot(s_hi, w_hi) + _dot(s_hi, w_lo) + _dot(s_lo, w_hi)
    o_ref[0] = acc + b_ref[0]


def _modulation(cc, w_mod, b_mod):
    depth, d, n = w_mod.shape
    rows = cc.shape[0]
    tn = _pick(n, (512, 256, 128))
    blocks = _nbytes((rows, d), F32) + _nbytes((d, tn), F32) + _nbytes((1, tn), F32) + _nbytes((rows, tn), F32)
    return pl.pallas_call(
        _mod_kernel,
        grid=(depth, n // tn),
        in_specs=[pl.BlockSpec((rows, d), lambda l, j: (0, 0)),
                  pl.BlockSpec((1, d, tn), lambda l, j: (l, 0, j)),
                  pl.BlockSpec((1, 1, tn), lambda l, j: (l, 0, j))],
        out_specs=pl.BlockSpec((1, rows, tn), lambda l, j: (l, 0, j)),
        out_shape=jax.ShapeDtypeStruct((depth, rows, n), F32),
        compiler_params=_params(("parallel", "parallel"), blocks, 3 * _nbytes((d, tn), F32)),
        name="modulation",
    )(cc, w_mod, b_mod.reshape(depth, 1, n))


INPROJ_SUB_ROWS = 256


def _norm_mod(x, gain, shift, scale):
    y = x * lax.rsqrt(jnp.mean(x * x, axis=-1, keepdims=True) + EPS) * gain
    return y * (1.0 + scale) + shift


def _inproj_kernel(x_ref, mod_ref, g_ref, w_ref, wba_ref, cos_ref, sin_ref, z_ref, ba_ref, h_ref, *, qk_width):
    j = pl.program_id(1)
    tm, tn = z_ref.shape
    full, rem = divmod(qk_width, tn)
    sub = min(tm, INPROJ_SUB_ROWS)

    def rope(t, rows):
        width = t.shape[1]
        cos = jnp.concatenate([cos_ref[rows, :]] * (width // HEAD_DIM), axis=1)
        sin = jnp.concatenate([sin_ref[rows, :]] * (width // HEAD_DIM), axis=1)
        partner = jnp.concatenate([pltpu.roll(t[:, g * HEAD_DIM:(g + 1) * HEAD_DIM], HEAD_DIM // 2, 1)
                                   for g in range(width // HEAD_DIM)], axis=1)
        return t * cos + partner * sin

    def tile(epilogue, fresh=False):
        for r in range(tm // sub):
            rows = pl.ds(r * sub, sub)
            if fresh:
                h = _norm_mod(x_ref[rows, :], g_ref[...], mod_ref[0, 0:1, :], mod_ref[0, 1:2, :]).astype(BF16)
                h_ref[rows, :] = h
                ba_ref[rows, :] = _dot(h, wba_ref[...])
            else:
                h = h_ref[rows, :]
            z_ref[rows, :] = epilogue(_dot(h, w_ref[...]), rows).astype(BF16)

    assert full >= 1

    @pl.when(j == 0)
    def _():
        tile(rope, fresh=True)

    @pl.when((j > 0) & (j < full))
    def _():
        tile(rope)

    if rem:
        @pl.when(j == full)
        def _():
            tile(lambda acc, rows: jnp.concatenate([rope(acc[:, :rem], rows), acc[:, rem:]], axis=1))

    @pl.when(j >= full + (1 if rem else 0))
    def _():
        tile(lambda acc, rows: acc)


def _inproj(x, mods, mod_row, gain, w_main, w_ba, layer, cos, sin, *, tm, tn, qk_width):
    m, d = x.shape
    n = w_main.shape[2]
    blocks = (_nbytes((tm, d), F32) + _nbytes((6, d), F32) + _nbytes((d, tn), BF16) + _nbytes((d, LANES), BF16)
              + 2 * _nbytes((tm, LANES), F32) + _nbytes((tm, tn), BF16) + _nbytes((tm, LANES), F32))
    return pl.pallas_call(
        functools.partial(_inproj_kernel, qk_width=qk_width),
        grid=(m // tm, n // tn),
        in_specs=[pl.BlockSpec((tm, d), lambda i, j: (i, 0)),
                  pl.BlockSpec((1, 6, d), lambda i, j: (mod_row(i, tm), 0, 0)),
                  pl.BlockSpec((1, d), lambda i, j: (0, 0)),
                  pl.BlockSpec((None, d, tn), lambda i, j: (layer, 0, j)),
                  pl.BlockSpec((None, d, LANES), lambda i, j: (layer, 0, 0)),
                  pl.BlockSpec((tm, LANES), lambda i, j: (i, 0)),
                  pl.BlockSpec((tm, LANES), lambda i, j: (i, 0))],
        out_specs=[pl.BlockSpec((tm, tn), lambda i, j: (i, j)),
                   pl.BlockSpec((tm, LANES), lambda i, j: (i, 0))],
        out_shape=[jax.ShapeDtypeStruct((m, n), BF16),
                   jax.ShapeDtypeStruct((m, LANES), F32)],
        scratch_shapes=[pltpu.VMEM((tm, d), BF16)],
        compiler_params=_params(("parallel", "arbitrary"), blocks,
                                _nbytes((tm, d), BF16) + 3 * _nbytes((tm, tn), F32)),
        name="inproj",
    )(x, mods, gain, w_main, w_ba, cos, sin)


def _softmax_pv(s, sink_col, v):
    m = jnp.maximum(jnp.max(s, axis=-1, keepdims=True), sink_col)
    p = jnp.exp(s - m)
    denom = jnp.sum(p, axis=-1, keepdims=True) + jnp.exp(sink_col - m)
    return _dot((p * (1.0 / denom)).astype(BF16), v)


def _stack_heads(q, groups):
    return jnp.concatenate([q[:, g * HEAD_DIM:(g + 1) * HEAD_DIM] for g in range(groups)], axis=0)


def _unstack_heads(o, groups, rows):
    return jnp.concatenate([o[g * rows:(g + 1) * rows] for g in range(groups)], axis=1)


def _sink_col(sink_ref, kvh, groups, rows):
    return jnp.concatenate([jnp.full((rows, 1), sink_ref[kvh * groups + g], F32) for g in range(groups)], axis=0)


ATTN_PAIRS = 8


def _attn_kernel(sink_ref, q_ref, *refs, nsteps, groups, pairs):
    nwin = 2 * pairs + 2
    k_blocks, v_blocks = refs[:nwin], refs[nwin:2 * nwin]
    kx_ref, vx_ref, o_ref = refs[2 * nwin:]
    kvh, n = pl.program_id(1), pl.program_id(2)
    bl = ATTN_BLOCK
    row = lax.broadcasted_iota(jnp.int32, (2 * bl, 3 * bl), 0)
    col = lax.broadcasted_iota(jnp.int32, (2 * bl, 3 * bl), 1)
    sub = row // bl
    band = jnp.abs(col - bl - (row % bl)) <= WINDOW
    kx, vx = kx_ref[...], vx_ref[...]

    def operands(p):
        k = jnp.concatenate([r[...] for r in k_blocks[2 * p:2 * p + 4]] + [kx], axis=0)
        v = jnp.concatenate([r[...] for r in v_blocks[2 * p:2 * p + 4]] + [vx], axis=0)
        v_ones = jnp.concatenate([v, jnp.ones(v.shape, BF16)], axis=1)
        valid = band
        if p == 0:
            valid = valid & ((col >= bl) | (sub == 1) | (n > 0))
        if p == pairs - 1:
            valid = valid & ((col < 2 * bl) | (sub == 0) | (n < nsteps - 1))
        return k, v_ones, valid

    ops = [operands(p) for p in range(pairs)]

    def scores(p, g):
        k, _, valid = ops[p]
        q = q_ref[pl.ds(p * 2 * bl, 2 * bl), g * HEAD_DIM:(g + 1) * HEAD_DIM]
        s = _dot_nt(q, k) * (HEAD_DIM ** -0.5 * LOG2_E)
        local = jnp.concatenate([s[:bl, :3 * bl], s[bl:, bl:4 * bl]], axis=0)
        return jnp.concatenate([jnp.where(valid, local, NEG_INF), s[:, 4 * bl:]], axis=1)

    def attend(s, sink, v_ones):
        sink = sink * LOG2_E
        m = jnp.maximum(jnp.max(s, axis=-1, keepdims=True), sink)
        p = jnp.exp2(s - m).astype(BF16)
        zeros = jnp.zeros((bl, bl), BF16)
        p = jnp.concatenate([jnp.concatenate([p[:bl, :3 * bl], zeros, p[:bl, 3 * bl:]], axis=1),
                             jnp.concatenate([zeros, p[bl:, :3 * bl], p[bl:, 3 * bl:]], axis=1)], axis=0)
        pv = _dot(p, v_ones)
        denom = pv[:, HEAD_DIM:] + jnp.exp2(sink - m)
        return pv[:, :HEAD_DIM] * (1.0 / denom)

    units = [(p, g) for p in range(pairs) for g in range(groups)]
    pending = scores(*units[0])
    for i, (p, g) in enumerate(units):
        s = pending
        if i + 1 < len(units):
            pending = scores(*units[i + 1])
        o = attend(s, sink_ref[kvh * groups + g], ops[p][1])
        o_ref[pl.ds(p * 2 * bl, 2 * bl), g * HEAD_DIM:(g + 1) * HEAD_DIM] = o.astype(BF16)


def _attention(z_l, z_c, sink, *, batch, seq, ctx_len, d_model, kv_width):
    bl = ATTN_BLOCK
    nb = seq // bl
    pairs = ATTN_PAIRS if nb % (2 * ATTN_PAIRS) == 0 else 1
    per_step = 2 * pairs
    assert nb % per_step == 0
    nsteps = nb // per_step
    n_kv = kv_width // HEAD_DIM
    groups = d_model // kv_width
    gw = groups * HEAD_DIM
    kcol, vcol = d_model // HEAD_DIM, (d_model + kv_width) // HEAD_DIM
    shifts = range(-1, per_step + 1)

    def kv_spec(col0, shift):
        def imap(b, h, n):
            return (b * nb + jnp.clip(per_step * n + shift, 0, nb - 1), col0 + h)
        return pl.BlockSpec((bl, HEAD_DIM), imap)

    ctx_k = pl.BlockSpec((ctx_len, HEAD_DIM), lambda b, h, n: (b, kcol + h))
    ctx_v = pl.BlockSpec((ctx_len, HEAD_DIM), lambda b, h, n: (b, vcol + h))
    blocks = (2 * _nbytes((per_step * bl, gw), BF16) + 2 * len(shifts) * _nbytes((bl, HEAD_DIM), BF16)
              + 2 * _nbytes((ctx_len, HEAD_DIM), BF16))
    temps = 8 * pairs * _nbytes((2 * bl, 4 * bl + ctx_len), F32)
    return pl.pallas_call(
        functools.partial(_attn_kernel, nsteps=nsteps, groups=groups, pairs=pairs),
        grid=(batch, n_kv, nsteps),
        in_specs=[pl.BlockSpec(memory_space=pltpu.SMEM),
                  pl.BlockSpec((per_step * bl, gw), lambda b, h, n: (b * nsteps + n, h))]
                 + [kv_spec(kcol, shift) for shift in shifts]
                 + [kv_spec(vcol, shift) for shift in shifts]
                 + [ctx_k, ctx_v],
        out_specs=pl.BlockSpec((per_step * bl, gw), lambda b, h, n: (b * nsteps + n, h)),
        out_shape=jax.ShapeDtypeStruct((batch * seq, d_model), BF16),
        compiler_params=_params(("parallel", "parallel", "arbitrary"), blocks, temps),
        name="window_attention",
    )(sink, *([z_l] * (1 + 2 * len(shifts))), z_c, z_c)


def _attn_ctx_kernel(sink_ref, q_ref, kx_ref, vx_ref, o_ref, *, groups):
    kvh = pl.program_id(1)
    rows = q_ref.shape[0]
    q = _stack_heads(q_ref[...], groups)
    s = _dot_nt(q, kx_ref[...]) * (HEAD_DIM ** -0.5)
    o = _softmax_pv(s, _sink_col(sink_ref, kvh, groups, rows), vx_ref[...])
    o_ref[...] = _unstack_heads(o, groups, rows).astype(BF16)


def _attention_ctx(qkv_c, sink, *, batch, ctx_len, d_model, kv_width):
    n_kv = kv_width // HEAD_DIM
    groups = d_model // kv_width
    gw = groups * HEAD_DIM
    kcol, vcol = d_model // HEAD_DIM, (d_model + kv_width) // HEAD_DIM
    blocks = 2 * _nbytes((ctx_len, gw), BF16) + 2 * _nbytes((ctx_len, HEAD_DIM), BF16)
    temps = 6 * _nbytes((groups * ctx_len, ctx_len), F32)
    return pl.pallas_call(
        functools.partial(_attn_ctx_kernel, groups=groups),
        grid=(batch, n_kv),
        in_specs=[pl.BlockSpec(memory_space=pltpu.SMEM),
                  pl.BlockSpec((ctx_len, gw), lambda b, h: (b, h)),
                  pl.BlockSpec((ctx_len, HEAD_DIM), lambda b, h: (b, kcol + h)),
                  pl.BlockSpec((ctx_len, HEAD_DIM), lambda b, h: (b, vcol + h))],
        out_specs=pl.BlockSpec((ctx_len, gw), lambda b, h: (b, h)),
        out_shape=jax.ShapeDtypeStruct((batch * ctx_len, d_model), BF16),
        compiler_params=_params(("parallel", "parallel"), blocks, temps),
        name="context_attention",
    )(sink, qkv_c, qkv_c, qkv_c)


CONV_PAD = SUBLANES
CONV_ROWS = 256


def _conv_seq(x_ref, w_ref, o_ref, pad_ref, l2_flag, l2_scale, lane0):
    s = x_ref.shape[0]
    lanes = slice(lane0, lane0 + LANES)
    w = w_ref[:, lanes]
    zeros = jnp.zeros((CONV_PAD, LANES), F32)
    pad_ref[pl.ds(0, CONV_PAD), :] = zeros
    pad_ref[pl.ds(CONV_PAD, s), :] = x_ref[:, lanes].astype(F32)
    pad_ref[pl.ds(CONV_PAD + s, CONV_PAD), :] = zeros
    rows = min(CONV_ROWS, s)

    def conv_silu(r0):
        y = jnp.zeros((rows, LANES), F32)
        for k in range(CONV_K):
            off = CONV_PAD - CONV_K // 2 + k
            y = y + pad_ref[pl.ds(r0 + off, rows), :] * w[k:k + 1, :]
        return _silu(y)

    def body_l2(it, carry):
        r0 = pl.multiple_of(it * rows, rows)
        y = conv_silu(r0)
        inv = lax.rsqrt(jnp.sum(y * y, axis=-1, keepdims=True) + EPS) * l2_scale
        o_ref[pl.ds(r0, rows), lanes] = (y * inv).astype(BF16)
        return carry

    def body_plain(it, carry):
        r0 = pl.multiple_of(it * rows, rows)
        o_ref[pl.ds(r0, rows), lanes] = conv_silu(r0).astype(BF16)
        return carry

    trips = s // rows
    unroll = 2 if trips % 2 == 0 else 1

    @pl.when(l2_flag)
    def _():
        lax.fori_loop(0, trips, body_l2, 0, unroll=unroll)

    @pl.when(jnp.logical_not(l2_flag))
    def _():
        lax.fori_loop(0, trips, body_plain, 0, unroll=unroll)


PREP_COLS = 2


def _dnprep_kernel(xl_ref, xc_ref, w_ref, ol_ref, oc_ref, padl_ref, padc_ref, *, n_q, cols):
    col = pl.program_id(1) * cols
    l2_flag = col < 2 * n_q
    l2_scale = jnp.where(col < n_q, HEAD_DIM ** -0.5, 1.0)
    for h in range(cols):
        _conv_seq(xl_ref, w_ref, ol_ref, padl_ref, l2_flag, l2_scale, h * LANES)
        _conv_seq(xc_ref, w_ref, oc_ref, padc_ref, l2_flag, l2_scale, h * LANES)


def _dn_prep(z_l, z_c, conv_w, *, col0, batch, seq, ctx_len, d_model):
    n_q = d_model // LANES
    cols = PREP_COLS if n_q % PREP_COLS == 0 and col0 % PREP_COLS == 0 else 1
    ncol = 3 * n_q // cols
    width = cols * LANES
    first = col0 // cols
    blocks = 2 * (_nbytes((seq, width), BF16) + _nbytes((ctx_len, width), BF16)) + _nbytes((8, width), F32)
    scratch = _nbytes((seq + 2 * CONV_PAD, LANES), F32) + _nbytes((ctx_len + 2 * CONV_PAD, LANES), F32)
    return pl.pallas_call(
        functools.partial(_dnprep_kernel, n_q=n_q, cols=cols),
        grid=(batch, ncol),
        in_specs=[pl.BlockSpec((seq, width), lambda b, c: (b, first + c)),
                  pl.BlockSpec((ctx_len, width), lambda b, c: (b, first + c)),
                  pl.BlockSpec((CONV_K, width), lambda b, c: (0, c))],
        out_specs=[pl.BlockSpec((seq, width), lambda b, c: (b, c)),
                   pl.BlockSpec((ctx_len, width), lambda b, c: (b, c))],
        out_shape=[jax.ShapeDtypeStruct((batch * seq, 3 * d_model), BF16),
                   jax.ShapeDtypeStruct((batch * ctx_len, 3 * d_model), BF16)],
        scratch_shapes=[pltpu.VMEM((seq + 2 * CONV_PAD, LANES), F32),
                        pltpu.VMEM((ctx_len + 2 * CONV_PAD, LANES), F32)],
        compiler_params=_params(("parallel", "parallel"), blocks, scratch + 16 * _nbytes((CONV_ROWS, LANES), F32)),
        name="deltanet_prep",
    )(z_l, z_c, conv_w)


def _gates_kernel(ba_ref, alog_ref, dtb_ref, o_ref, *, n_heads):
    tm = ba_ref.shape[0]
    x = ba_ref[...]
    lane = lax.broadcasted_iota(jnp.int32, (CHUNK, LANES), 1)
    z = x + dtb_ref[...]
    softplus = jnp.maximum(z, 0.0) + jnp.log(1.0 + jnp.exp(-jnp.abs(z)))
    g = -jnp.exp(alog_ref[...]) * softplus
    beta = _sigmoid(x)
    r = lax.broadcasted_iota(jnp.int32, (CHUNK, CHUNK), 0)
    c = lax.broadcasted_iota(jnp.int32, (CHUNK, CHUNK), 1)
    tri_prefix = (c <= r).astype(BF16)
    tri_suffix = (c >= r).astype(BF16)
    for i in range(tm // CHUNK):
        rows = slice(i * CHUNK, (i + 1) * CHUNK)
        parts = _split3(g[rows])
        pre = sum(_dot(tri_prefix, p) for p in parts)
        suf = sum(_dot(tri_suffix, p) for p in parts)
        csum = jnp.where(lane < 3 * n_heads, pre, suf)
        o_ref[rows, :] = jnp.where(lane < 2 * n_heads, beta[rows], csum)


def _gates(ba, alog_vec, dtb_vec, *, n_heads):
    m = ba.shape[0]
    tm = _pick(m, (1024, 512, 256, 128))
    blocks = 2 * _nbytes((tm, LANES), F32)
    return pl.pallas_call(
        functools.partial(_gates_kernel, n_heads=n_heads),
        grid=(m // tm,),
        in_specs=[pl.BlockSpec((tm, LANES), lambda i: (i, 0)),
                  pl.BlockSpec((1, LANES), lambda i: (0, 0)),
                  pl.BlockSpec((1, LANES), lambda i: (0, 0))],
        out_specs=pl.BlockSpec((tm, LANES), lambda i: (i, 0)),
        out_shape=jax.ShapeDtypeStruct((m, LANES), F32),
        compiler_params=_params(("parallel",), blocks, 8 * _nbytes((tm, LANES), F32)),
        name="deltanet_gates",
    )(ba, alog_vec, dtb_vec)


def _dot_split(a, b):
    n = b[0].shape[1]
    lhs = jnp.concatenate(a, axis=1)
    half = jnp.concatenate(b, axis=1)
    out = _dot(lhs, jnp.concatenate([half, half], axis=0))
    return out[:, :n] + out[:, n:]


def _unit_tri_inverse_many(mats, hooks=()):
    n = mats[0].shape[0]
    row = lax.broadcasted_iota(jnp.int32, (n, n), 0)
    col = lax.broadcasted_iota(jnp.int32, (n, n), 1)
    splits = [_split2(a) for a in mats]
    eye = jnp.where(row == col, 1.0, 0.0)
    pair = (row // 2) == (col // 2)
    xs = [eye - jnp.where(pair, a, 0.0) for a in mats]
    s, level = 2, 0
    while s < n:
        couple = ((row // (2 * s)) == (col // (2 * s))) & ((row // s) != (col // s))
        zero16 = jnp.zeros((n, n), BF16)
        es = [(jnp.where(couple, hi, zero16), jnp.where(couple, lo, zero16)) for hi, lo in splits]
        xsp = [_split2(x) for x in xs]
        exs = [_split2(_dot_split(e, xp)) for e, xp in zip(es, xsp)]
        xs = [x - _dot_split(xp, ex) for x, xp, ex in zip(xs, xsp, exs)]
        if level < len(hooks):
            hooks[level]()
        s *= 2
        level += 1
    return xs


def _scan_prep_group(refs, chunks, dsts, sel, scr, hooks=()):
    u_s, w_s, qk_s, qd_s, kdt_s, gl_s = scr
    c = CHUNK
    group = len(chunks)
    per_chunk = refs if isinstance(refs, list) else [refs] * group
    hooks = list(hooks)
    if hooks:
        hooks.pop(0)()
    row = lax.broadcasted_iota(jnp.int32, (c, c), 0)
    col = lax.broadcasted_iota(jnp.int32, (c, c), 1)
    eye16 = (row == col).astype(BF16)
    r0s = [pl.multiple_of(ci * c, c) for ci in chunks]
    q16s = [r[0][pl.ds(r0, c), :] for r, r0 in zip(per_chunk, r0s)]
    k16s = [r[1][pl.ds(r0, c), :] for r, r0 in zip(per_chunk, r0s)]
    qs = [q.astype(F32) for q in q16s]
    ks = [k.astype(F32) for k in k16s]
    vs = [r[2][pl.ds(r0, c), :].astype(F32) for r, r0 in zip(per_chunk, r0s)]
    sel2 = jnp.concatenate([sel, sel], axis=0)
    parts = [_split3(r[3][pl.ds(r0, c), :]) for r, r0 in zip(per_chunk, r0s)]
    gates = [_dot(jnp.concatenate([hi, mid], axis=1), sel2) + _dot(lo, sel) for hi, mid, lo in parts]
    kks = [_dot_nt(k16, k16) for k16 in k16s]
    qks = [_dot_nt(q16, k16) for q16, k16 in zip(q16s, k16s)]
    probs = [(g, d) for g in range(group) for d in range(2)]
    betas = [gates[g][:, d * LANES:(d + 1) * LANES] for g, d in probs]
    ccols = [gates[g][:, (2 + d) * LANES:(3 + d) * LANES] for g, d in probs]
    decays, mats = [], []
    for (g, d), beta, ccol in zip(probs, betas, ccols):
        crow = per_chunk[g][4][d][0, :, pl.ds(r0s[g], c)]
        incl = (row >= col) if d == 0 else (row <= col)
        strict = (row > col) if d == 0 else (row < col)
        decay = jnp.exp(jnp.where(incl, ccol - crow, NEG_INF))
        decays.append(decay)
        mats.append(jnp.where(strict, beta * kks[g] * decay, 0.0))
    if hooks:
        hooks.pop(0)()
    xsp = [_split2(x) for x in _unit_tri_inverse_many(mats, hooks)]
    expcs = [jnp.exp(ccol) for ccol in ccols]
    rus = [_split2(vs[g] * beta) for (g, d), beta in zip(probs, betas)]
    us = [_dot_split(xp, ru) for xp, ru in zip(xsp, rus)]
    rws = [(ks[g] * beta * expc).astype(BF16) for (g, d), beta, expc in zip(probs, betas, expcs)]
    ws = [_dot(jnp.concatenate(xp, axis=1), jnp.concatenate([rw, rw], axis=0)) for xp, rw in zip(xsp, rws)]
    clasts = [ccol[c - 1:c, :] if d == 0 else ccol[0:1, :] for (g, d), ccol in zip(probs, ccols)]
    kdts = [_dot_nt(eye16, (ks[g] * jnp.exp(clast - ccol)).astype(BF16))
            for (g, d), clast, ccol in zip(probs, clasts, ccols)]
    for i, (g, d) in enumerate(probs):
        rows = pl.ds(pl.multiple_of(dsts[g] * c, c), c)
        u_s[d, rows, :] = us[i]
        w_s[d, rows, :] = ws[i].astype(BF16)
        qk_s[d, rows, :] = (qks[g] * decays[i]).astype(BF16)
        qd_s[d, rows, :] = (qs[g] * expcs[i]).astype(BF16)
        kdt_s[d, rows, :] = kdts[i].astype(BF16)
        gl_s[d, pl.ds(pl.multiple_of(dsts[g] * SUBLANES, SUBLANES), SUBLANES), :] = jnp.broadcast_to(
            jnp.exp(clasts[i]), (SUBLANES, LANES))


def _step_issue(chunks, states, scr):
    u_s, w_s, qk_s, qd_s, kdt_s, gl_s = scr
    c = CHUNK
    rows = [pl.ds(pl.multiple_of(ci * c, c), c) for ci in chunks]
    dirs = range(2)
    ws = [_dot(jnp.concatenate([w_s[d, rows[d], :], qd_s[d, rows[d], :]], axis=0), states[d].astype(BF16))
          for d in dirs]
    return dict(chunks=chunks, rows=rows, states=states, ws=ws)


def _step_finish(pending, scr):
    u_s, w_s, qk_s, qd_s, kdt_s, gl_s = scr
    c = CHUNK
    chunks, rows, states, ws = pending["chunks"], pending["rows"], pending["states"], pending["ws"]
    dirs = range(2)
    v_new = [(u_s[d, rows[d], :] - ws[d][:c]).astype(BF16) for d in dirs]
    outs = [ws[d][c:] + _dot(qk_s[d, rows[d], :], v_new[d]) for d in dirs]
    gls = [gl_s[d, pl.ds(pl.multiple_of(chunks[d] * SUBLANES, SUBLANES), SUBLANES), :][0:1, :] for d in dirs]
    new_states = [states[d] * gls[d] + _dot(kdt_s[d, rows[d], :], v_new[d]) for d in dirs]
    return tuple(new_states), outs


def _scan_step_pair(chunks, states, scr):
    return _step_finish(_step_issue(chunks, states, scr), scr)


PREP_GROUP = 4


def _scan_kernel(ql, kl, vl, pl_l, crfl, crbl, zl_prev, qx, kx, vx, px, crfx, crbx, zx, gain_ref,
                 outl_prev_ref, outx_ref, u_s, w_s, qk_s, qd_s, kdt_s, gl_s, o_s, st_s, *, n_heads):
    b, j = pl.program_id(0), pl.program_id(1)
    c = CHUNK
    ncx, ncl = qx.shape[0] // c, ql.shape[0] // c
    half = PREP_GROUP // 2
    n_groups = ncl // PREP_GROUP
    assert PREP_GROUP == 4 and ncx % 2 == 0 and ncl % PREP_GROUP == 0 and ncx <= PREP_GROUP
    cur = j % 2
    factor_refs = (u_s, w_s, qk_s, qd_s, kdt_s, gl_s)
    scr_cur, o_cur = tuple(r.at[cur] for r in factor_refs), o_s.at[cur]
    scr_prev, o_prev = tuple(r.at[1 - cur] for r in factor_refs), o_s.at[1 - cur]
    gain = gain_ref[...]

    def chunk_rows(i):
        return pl.ds(pl.multiple_of(i * c, c), c)

    @pl.when((b == 0) & (j == 0))
    def _():
        def body(i, carry):
            for r in factor_refs[:5]:
                for d in range(2):
                    r[1, d, chunk_rows(i), :] = jnp.zeros((c, LANES), r.dtype)
            o_s[1, chunk_rows(i), :] = jnp.zeros((c, LANES), F32)
            return carry
        lax.fori_loop(0, ncx + ncl, body, 0)
        gl_s[1] = jnp.zeros(gl_s.shape[1:], F32)
        st_s[...] = jnp.zeros(st_s.shape, F32)

    def finish(z_ref, out_ref, i, o):
        on = o * lax.rsqrt(jnp.mean(o * o, axis=-1, keepdims=True) + EPS) * gain
        out_ref[chunk_rows(i), :] = (on * _silu(z_ref[chunk_rows(i), :].astype(F32))).astype(BF16)

    def step_stages(scr, o_ref, base, n, it, box, key, z_ref=None, out_ref=None):
        pending = {}
        jf, jb = it, n - 1 - it

        def issue():
            pending.update(_step_issue((base + jf, base + jb), box[key], scr))

        def complete():
            box[key], outs = _step_finish(pending, scr)
            if z_ref is None:
                o_ref[chunk_rows(base + jf), :] = outs[0]
                o_ref[chunk_rows(base + jb), :] = outs[1]
            else:
                finish(z_ref, out_ref, jf, outs[0] + o_ref[chunk_rows(base + jf), :])
                finish(z_ref, out_ref, jb, outs[1] + o_ref[chunk_rows(base + jb), :])
        return [issue, complete]

    def run(stages):
        for stage in stages:
            stage()

    def front(i):
        return [half * i + k for k in range(half)]

    def drain_stages(p, box):
        if isinstance(p, int) and p < half:
            return step_stages(scr_prev, o_prev, ncx, ncl, front(n_groups - 1)[p], box, "prev")
        return step_stages(scr_prev, o_prev, ncx, ncl, ncl // 2 + p - half, box, "prev", zl_prev, outl_prev_ref)

    n_drain = half + ncl // 2

    @pl.when(j < n_heads)
    def _():
        r = lax.broadcasted_iota(jnp.int32, (LANES, 4 * LANES), 0)
        t = lax.broadcasted_iota(jnp.int32, (LANES, 4 * LANES), 1) // LANES
        sel = (r == t * n_heads + j).astype(BF16)
        zero = jnp.zeros((HEAD_DIM, HEAD_DIM), F32)
        box = {"cur": (zero, zero), "prev": (st_s[0], st_s[1])}

        refs_x = (qx, kx, vx, px, (crfx, crbx))
        refs_l = (ql, kl, vl, pl_l, (crfl, crbl))

        def latent_chunks(i):
            return front(i) + [ncl - 1 - ci for ci in reversed(front(i))]

        def prep_latent(i, hooks):
            chunks = latent_chunks(i)
            _scan_prep_group(refs_l, chunks, [ncx + ci for ci in chunks], sel, scr_cur, hooks)

        first = latent_chunks(0)
        _scan_prep_group([refs_x] * ncx + [refs_l] * len(first), list(range(ncx)) + first,
                         list(range(ncx)) + [ncx + ci for ci in first], sel, scr_cur,
                         [stage for p in range(2 * half) for stage in drain_stages(p, box)])
        for it in range(ncx):
            ends = (zx, outx_ref) if it >= ncx // 2 else ()
            run(step_stages(scr_cur, o_cur, 0, ncx, it, box, "cur", *ends))

        def overlapped(i, carry):
            box = {"cur": carry[:2], "prev": carry[2:]}
            hooks = []
            for k, it in enumerate(front(i - 1)):
                own = step_stages(scr_cur, o_cur, ncx, ncl, it, box, "cur")
                other = drain_stages(2 * half + half * (i - 1) + k, box)
                hooks += [own[0], other[0], own[1], other[1]]
            prep_latent(i, hooks)
            return tuple(box["cur"]) + tuple(box["prev"])

        carry = lax.fori_loop(1, n_groups, overlapped, tuple(box["cur"]) + tuple(box["prev"]))
        st_s[0] = carry[0]
        st_s[1] = carry[1]

    @pl.when(j == n_heads)
    def _():
        box = {"prev": (st_s[0], st_s[1])}
        for p in range(half):
            run(drain_stages(p, box))

        def body(p, states):
            inner = {"prev": states}
            run(drain_stages(p, inner))
            return tuple(inner["prev"])
        lax.fori_loop(half, n_drain, body, tuple(box["prev"]))


def _delta_scan(dn_l, dn_c, gates_l, gates_c, gates_lt, gates_ct, z_l, z_c, gain,
                *, zcol, batch, seq, ctx_len, d_model):
    n_heads = d_model // HEAD_DIM
    t = seq + ctx_len
    last = n_heads - 1

    def col_spec(rows, col0, lag=0):
        return pl.BlockSpec((rows, LANES), lambda b, j: (b, col0 + jnp.clip(j - lag, 0, last)))

    def gate_spec(rows):
        return pl.BlockSpec((rows, LANES), lambda b, j: (b, 0))

    def crow_spec(rows, d):
        return pl.BlockSpec((1, 1, rows), lambda b, j: ((2 + d) * n_heads + jnp.minimum(j, last), 0, b))

    def side(rows, z_lag):
        return [col_spec(rows, 0), col_spec(rows, n_heads), col_spec(rows, 2 * n_heads), gate_spec(rows),
                crow_spec(rows, 0), crow_spec(rows, 1), col_spec(rows, zcol, z_lag)]

    blocks = (_nbytes((t, LANES), F32) + 2 * _nbytes((SUBLANES, t), F32) + 5 * _nbytes((t, LANES), BF16))
    slots = 2
    scratch_shapes = [pltpu.VMEM((slots, 2, t, LANES), F32),
                      pltpu.VMEM((slots, 2, t, LANES), BF16),
                      pltpu.VMEM((slots, 2, t, LANES), BF16),
                      pltpu.VMEM((slots, 2, t, LANES), BF16),
                      pltpu.VMEM((slots, 2, t, LANES), BF16),
                      pltpu.VMEM((slots, 2, (t // CHUNK) * SUBLANES, LANES), F32),
                      pltpu.VMEM((slots, t, LANES), F32),
                      pltpu.VMEM((2, HEAD_DIM, HEAD_DIM), F32)]
    scratch = slots * (3 * _nbytes((t, LANES), F32) + 4 * 2 * _nbytes((t, LANES), BF16))
    return pl.pallas_call(
        functools.partial(_scan_kernel, n_heads=n_heads),
        grid=(batch, n_heads + 1),
        in_specs=side(seq, 1) + side(ctx_len, 0) + [pl.BlockSpec((1, LANES), lambda b, j: (0, 0))],
        out_specs=[pl.BlockSpec((seq, LANES), lambda b, j: (b, jnp.clip(j - 1, 0, last))),
                   pl.BlockSpec((ctx_len, LANES), lambda b, j: (b, jnp.minimum(j, last)))],
        out_shape=[jax.ShapeDtypeStruct((batch * seq, d_model), BF16),
                   jax.ShapeDtypeStruct((batch * ctx_len, d_model), BF16)],
        scratch_shapes=scratch_shapes,
        compiler_params=_params(("arbitrary", "arbitrary"), blocks, scratch + 4 * 1024 * 1024),
        name="delta_scan",
    )(dn_l, dn_l, dn_l, gates_l, gates_lt, gates_lt, z_l,
      dn_c, dn_c, dn_c, gates_c, gates_ct, gates_ct, z_c, gain)


def _merge_kernel(att_ref, dn_ref, ga_ref, gd_ref, wa_ref, wd_ref, o_ref):
    tm = o_ref.shape[0]
    sub = min(tm, INPROJ_SUB_ROWS)
    for r in range(tm // sub):
        rows = pl.ds(r * sub, sub)
        y = (_sigmoid(ga_ref[rows, :].astype(F32)) * _dot(att_ref[rows, :], wa_ref[...])
             + _sigmoid(gd_ref[rows, :].astype(F32)) * _dot(dn_ref[rows, :], wd_ref[...]))
        o_ref[rows, :] = y.astype(BF16)


def _merge(att, dn, z, w_attn, w_delta, layer, *, gate_col, tm, tn):
    m, d = att.shape
    ga0, gd0 = gate_col // tn, (gate_col + d) // tn
    blocks = (2 * _nbytes((tm, d), BF16) + 2 * _nbytes((tm, tn), BF16) + 2 * _nbytes((d, tn), BF16)
              + _nbytes((tm, tn), BF16))
    return pl.pallas_call(
        _merge_kernel,
        grid=(m // tm, d // tn),
        in_specs=[pl.BlockSpec((tm, d), lambda i, j: (i, 0)),
                  pl.BlockSpec((tm, d), lambda i, j: (i, 0)),
                  pl.BlockSpec((tm, tn), lambda i, j: (i, ga0 + j)),
                  pl.BlockSpec((tm, tn), lambda i, j: (i, gd0 + j)),
                  pl.BlockSpec((None, d, tn), lambda i, j: (layer, 0, j)),
                  pl.BlockSpec((None, d, tn), lambda i, j: (layer, 0, j))],
        out_specs=pl.BlockSpec((tm, tn), lambda i, j: (i, j)),
        out_shape=jax.ShapeDtypeStruct((m, d), BF16),
        compiler_params=_params(("parallel", "arbitrary"), blocks, 4 * _nbytes((tm, tn), F32)),
        name="merge",
    )(att, dn, z, z, w_attn, w_delta)


def _resid_kernel(a_ref, w_ref, x_ref, mod_ref, o_ref, *, gate_row):
    o_ref[...] = x_ref[...] + mod_ref[0, gate_row:gate_row + 1, :] * _dot(a_ref[...], w_ref[...])


def _resid_matmul(a, w, layer, x, mods, mod_row, gate_row, *, tm, tn, name):
    m, k = a.shape
    d = w.shape[2]
    blocks = (_nbytes((tm, k), BF16) + _nbytes((k, tn), BF16) + 2 * _nbytes((tm, tn), F32)
              + _nbytes((6, tn), F32))
    return pl.pallas_call(
        functools.partial(_resid_kernel, gate_row=gate_row),
        grid=(m // tm, d // tn),
        in_specs=[pl.BlockSpec((tm, k), lambda i, j: (i, 0)),
                  pl.BlockSpec((None, k, tn), lambda i, j: (layer, 0, j)),
                  pl.BlockSpec((tm, tn), lambda i, j: (i, j)),
                  pl.BlockSpec((1, 6, tn), lambda i, j: (mod_row(i, tm), 0, j))],
        out_specs=pl.BlockSpec((tm, tn), lambda i, j: (i, j)),
        out_shape=jax.ShapeDtypeStruct((m, d), F32),
        compiler_params=_params(("parallel", "arbitrary"), blocks, 2 * _nbytes((tm, tn), F32)),
        name=name,
    )(a, w, x, mods)


def _ffn_up_kernel(x_ref, mod_ref, g_ref, wg_ref, wu_ref, o_ref, h_ref):
    j = pl.program_id(1)
    tm = o_ref.shape[0]
    sub = min(tm, INPROJ_SUB_ROWS)

    def tile(fresh):
        for r in range(tm // sub):
            rows = pl.ds(r * sub, sub)
            if fresh:
                h = _norm_mod(x_ref[rows, :], g_ref[...], mod_ref[0, 3:4, :], mod_ref[0, 4:5, :]).astype(BF16)
                h_ref[rows, :] = h
            else:
                h = h_ref[rows, :]
            o_ref[rows, :] = (_silu(_dot(h, wg_ref[...])) * _dot(h, wu_ref[...])).astype(BF16)

    @pl.when(j == 0)
    def _():
        tile(True)

    @pl.when(j > 0)
    def _():
        tile(False)


def _ffn_up(x, mods, mod_row, gain, w_gate_up, layer, *, tm, tn):
    m, d = x.shape
    f = w_gate_up.shape[2] // 2
    nj = f // tn
    blocks = (_nbytes((tm, d), F32) + _nbytes((6, d), F32) + 2 * _nbytes((d, tn), BF16)
              + _nbytes((tm, tn), BF16))
    return pl.pallas_call(
        _ffn_up_kernel,
        grid=(m // tm, nj),
        in_specs=[pl.BlockSpec((tm, d), lambda i, j: (i, 0)),
                  pl.BlockSpec((1, 6, d), lambda i, j: (mod_row(i, tm), 0, 0)),
                  pl.BlockSpec((1, d), lambda i, j: (0, 0)),
                  pl.BlockSpec((None, d, tn), lambda i, j: (layer, 0, j)),
                  pl.BlockSpec((None, d, tn), lambda i, j: (layer, 0, nj + j))],
        out_specs=pl.BlockSpec((tm, tn), lambda i, j: (i, j)),
        out_shape=jax.ShapeDtypeStruct((m, f), BF16),
        scratch_shapes=[pltpu.VMEM((tm, d), BF16)],
        compiler_params=_params(("parallel", "arbitrary"), blocks,
                                _nbytes((tm, d), BF16) + 4 * _nbytes((tm, tn), F32)),
        name="ffn_up",
    )(x, mods, gain, w_gate_up, w_gate_up)


def _final_norm_kernel(x_ref, g_ref, o_ref):
    x = x_ref[...]
    o_ref[...] = x * lax.rsqrt(jnp.mean(x * x, axis=-1, keepdims=True) + EPS) * g_ref[...]


def _final_norm(x, gain, *, tm):
    m, d = x.shape
    return pl.pallas_call(
        _final_norm_kernel,
        grid=(m // tm,),
        in_specs=[pl.BlockSpec((tm, d), lambda i: (i, 0)), pl.BlockSpec((1, d), lambda i: (0, 0))],
        out_specs=pl.BlockSpec((tm, d), lambda i: (i, 0)),
        out_shape=jax.ShapeDtypeStruct((m, d), F32),
        compiler_params=_params(("parallel",), 2 * _nbytes((tm, d), F32), 2 * _nbytes((tm, d), F32)),
        name="final_norm",
    )(x, gain)


def _rope_tables(seq):
    quarter = HEAD_DIM // 4
    pos = jnp.arange(seq)
    row = (pos // GRID_W).astype(F32)
    col = (pos % GRID_W).astype(F32)
    inv_freq = ROPE_BASE ** (-jnp.arange(quarter, dtype=F32) / quarter)
    ang_r, ang_c = row[:, None] * inv_freq[None], col[:, None] * inv_freq[None]
    cos = jnp.concatenate([jnp.cos(ang_r), jnp.cos(ang_c)] * 2, axis=1)
    sin = jnp.concatenate([-jnp.sin(ang_r), -jnp.sin(ang_c), jnp.sin(ang_r), jnp.sin(ang_c)], axis=1)
    return cos, sin


def _rope_perm():
    quarter = HEAD_DIM // 4
    order = (0, 2, 1, 3)
    return jnp.concatenate([jnp.arange(quarter) + quarter * o for o in order])


def kernel(x, c, ctx, c_ctx, w_mod, b_mod, norm1, norm2, w_in, conv_w, attn_sink, a_log, dt_bias, dn_norm,
           w_attn_proj, w_delta_proj, w_out, w_gate_up, w_down, final_norm):
    batch, seq, d = x.shape
    ctx_len = ctx.shape[1]
    depth = w_mod.shape[0]
    n_heads = d // HEAD_DIM
    kv_width = (n_heads // 4) * HEAD_DIM
    qkv_width = d + 2 * kv_width
    assert seq % CHUNK == 0 and ctx_len % CHUNK == 0 and seq % ATTN_BLOCK == 0
    assert 4 * n_heads <= LANES and (batch * seq) % ctx_len == 0

    off_ba = qkv_width + 4 * d
    qk_width = d + kv_width
    w_qk = w_in[:, :, :qk_width].reshape(depth, d, qk_width // HEAD_DIM, HEAD_DIM)[..., _rope_perm()]
    w_main = jnp.concatenate([w_qk.reshape(depth, d, qk_width), w_in[:, :, qk_width:off_ba],
                              w_in[:, :, off_ba + 4 * n_heads:]], axis=2).astype(BF16)
    w_ba = jnp.pad(w_in[:, :, off_ba:off_ba + 4 * n_heads], ((0, 0), (0, 0), (0, LANES - 4 * n_heads))).astype(BF16)
    w_attn16, w_delta16, w_out16 = w_attn_proj.astype(BF16), w_delta_proj.astype(BF16), w_out.astype(BF16)
    w_gu16, w_down16 = w_gate_up.astype(BF16), w_down.astype(BF16)

    lane_pad = ((0, 0), (2 * n_heads, LANES - 4 * n_heads))
    alog_vec = jnp.pad(a_log.reshape(depth, 2 * n_heads), lane_pad).reshape(depth, 1, LANES)
    dtb_vec = jnp.pad(dt_bias.reshape(depth, 2 * n_heads), lane_pad).reshape(depth, 1, LANES)

    cos_l, sin_l = _rope_tables(seq)
    cos_l, sin_l = jnp.tile(cos_l, (batch, 1)), jnp.tile(sin_l, (batch, 1))
    cos_c = jnp.ones((batch * ctx_len, LANES), F32)
    sin_c = jnp.zeros((batch * ctx_len, LANES), F32)

    mod_rows = ((batch + 1 + SUBLANES - 1) // SUBLANES) * SUBLANES
    cc = jnp.concatenate([c, c_ctx[None], jnp.zeros((mod_rows - batch - 1, d), F32)], axis=0)
    mods_all = _modulation(cc, w_mod, b_mod).reshape(depth, mod_rows, 6, d)

    tm_l = _pick(seq, (1024, 512, 256, 128))
    tm_c = _pick(batch * ctx_len, (1024, 512, 256, 128))
    tn = _pick(kv_width, (512, 256, 128))
    tn_in = _pick(w_main.shape[2], (1536, 1024, 512, 256, 128))
    tn_wide = 2 * tn if (qkv_width + 4 * d) % (2 * tn) == 0 and d % (2 * tn) == 0 else tn

    def row_l(i, tm):
        return (i * tm) // seq

    def row_c(i, tm):
        return batch

    xl = x.reshape(batch * seq, d)
    xc = ctx.reshape(batch * ctx_len, d)
    dims = dict(batch=batch, seq=seq, ctx_len=ctx_len, d_model=d)

    for l in range(depth):
        mods = mods_all[l]
        need_ctx = l < depth - 1
        g1, g2 = norm1[l][None], norm2[l][None]
        proj = functools.partial(_inproj, gain=g1, w_main=w_main, w_ba=w_ba, layer=l, tn=tn_in,
                                 qk_width=d + kv_width)
        z_l, ba_l = proj(xl, mods, row_l, cos=cos_l, sin=sin_l, tm=tm_l)
        z_c, ba_c = proj(xc, mods, row_c, cos=cos_c, sin=sin_c, tm=tm_c)

        att_l = _attention(z_l, z_c, attn_sink[l], kv_width=kv_width, **dims)

        dn_l, dn_c = _dn_prep(z_l, z_c, conv_w[l], col0=qkv_width // LANES, **dims)
        gates_l = _gates(ba_l, alog_vec[l], dtb_vec[l], n_heads=n_heads)
        gates_c = _gates(ba_c, alog_vec[l], dtb_vec[l], n_heads=n_heads)
        gates_lt = gates_l.T.reshape(LANES, 1, batch * seq)
        gates_ct = gates_c.T.reshape(LANES, 1, batch * ctx_len)
        do_l, do_c = _delta_scan(dn_l, dn_c, gates_l, gates_c, gates_lt, gates_ct, z_l, z_c,
                                 dn_norm[l][None], zcol=(qkv_width + 3 * d) // LANES, **dims)

        streams = [(xl, att_l, do_l, z_l, row_l, tm_l)]
        if need_ctx:
            att_c = _attention_ctx(z_c, attn_sink[l], batch=batch, ctx_len=ctx_len, d_model=d, kv_width=kv_width)
            streams.append((xc, att_c, do_c, z_c, row_c, tm_c))
        outs = []
        for xs, att, dn, z, row, tm in streams:
            y = _merge(att, dn, z, w_attn16, w_delta16, l, gate_col=qkv_width + 4 * d, tm=tm, tn=tn_wide)
            xs = _resid_matmul(y, w_out16, l, xs, mods, row, 2, tm=tm, tn=tn_wide, name="out_proj")
            act = _ffn_up(xs, mods, row, g2, w_gu16, l, tm=tm, tn=tn)
            xs = _resid_matmul(act, w_down16, l, xs, mods, row, 5, tm=tm, tn=tn, name="ffn_down")
            outs.append(xs)
        xl = outs[0]
        if need_ctx:
            xc = outs[1]

    return _final_norm(xl, final_norm[None], tm=tm_l).reshape(batch, seq, d)
```
